```python
import math
import jax
import jax.numpy as jnp
from jax import lax
import numpy as np

D_MODEL = 2048
BATCH = 2
SEQ = 4096
DEPTH = 4

GRID_W = 64
CTX_LEN = 256
N_MIXERS = 4
NORM_EPS = 1e-6
D_FF = 4 * D_MODEL
N_ADA = 6
FNO_GROUPS = 4
GMLP_CHUNK = 128
GMLP_HEADS = 16
GMLP_WIDTH = D_MODEL
DIFF_HEADS = 8
DIFF_HEAD_DIM = D_MODEL // DIFF_HEADS // 2
DIFF_BLOCK_Q = 128
ROPE_THETA = 10000.0
GLA_HEADS = 4
GLA_DK = D_MODEL // 2
GLA_DV = D_MODEL
GLA_DK_HEAD = GLA_DK // GLA_HEADS
GLA_DV_HEAD = GLA_DV // GLA_HEADS
GLA_GATE_RANK = 16
GLA_TAU = 16.0
GLA_CHUNK = 64
GLA_IN = 2 * GLA_DK + 2 * GLA_DV + 2 * GLA_GATE_RANK

kernel_name = 'hybrid_interleaved_fourier_gmlp_diffattn_gla_dit'


def _layer_count(kind):
    return len(range(kind, DEPTH, N_MIXERS))


def _rmsnorm(x, g, eps=NORM_EPS):
    xf = x.astype(jnp.float32)
    y = xf * lax.rsqrt(jnp.mean(xf * xf, axis=-1, keepdims=True) + eps)
    return (y * g.astype(jnp.float32)).astype(x.dtype)


def _layernorm(x, g, b, eps=1e-5):
    xf = x.astype(jnp.float32)
    mu = jnp.mean(xf, axis=-1, keepdims=True)
    xc = xf - mu
    var = jnp.mean(xc * xc, axis=-1, keepdims=True)
    return (xc * lax.rsqrt(var + eps) * g.astype(jnp.float32) + b.astype(jnp.float32)).astype(x.dtype)


def _modulate(x, g, shift, scale):
    return _rmsnorm(x, g) * (1 + scale) + shift


def _sq_relu_mlp(h, w_in, w_out):
    return jnp.square(jax.nn.relu(h @ w_in)) @ w_out


def fourier_mix(h, w_out):
    b, n, d = h.shape
    hg = h.astype(jnp.float32).reshape(b, n, FNO_GROUPS, d // FNO_GROUPS)
    f = jnp.fft.fftn(hg, axes=(1, 3), norm='ortho').real
    return f.reshape(b, n, d).astype(h.dtype) @ w_out


def chunk_gmlp(h, w_in, ln_g, ln_b, w_s, b_s, w_out):
    b, n, _ = h.shape
    z = jax.nn.gelu(h @ w_in, approximate=False)
    u, v = jnp.split(z, 2, axis=-1)
    v = _layernorm(v, ln_g, ln_b)
    vc = v.reshape(b, n // GMLP_CHUNK, GMLP_CHUNK, GMLP_HEADS, GMLP_WIDTH // GMLP_HEADS)
    s = jnp.einsum('hpq,bcqhe->bcphe', w_s, vc) + b_s.T[:, :, None]
    return (u * s.reshape(b, n, GMLP_WIDTH)) @ w_out


def _axial_rope(n, head_dim):
    rows = n // GRID_W
    row = jnp.broadcast_to(jnp.arange(rows)[:, None], (rows, GRID_W)).reshape(-1).astype(jnp.float32)
    col = jnp.broadcast_to(jnp.arange(GRID_W)[None, :], (rows, GRID_W)).reshape(-1).astype(jnp.float32)
    n_freq = head_dim // 4
    inv = ROPE_THETA ** (-jnp.arange(n_freq, dtype=jnp.float32) / n_freq)
    ang = jnp.concatenate([row[:, None] * inv, col[:, None] * inv], axis=-1)
    return jnp.cos(ang), jnp.sin(ang)


def _apply_rope(x, cos, sin):
    x1, x2 = jnp.split(x, 2, axis=-1)
    cos = cos.astype(x.dtype)
    sin = sin.astype(x.dtype)
    return jnp.concatenate([x1 * cos - x2 * sin, x1 * sin + x2 * cos], axis=-1)


def diff_attention(hx, hc, w_in, lam_vecs, subln_g, w_out, lam_init):
    b, n, d = hx.shape
    nh, hd = DIFF_HEADS, DIFF_HEAD_DIM

    def qkv(h):
        l = h.shape[1]
        q, k, v = jnp.split(h @ w_in, 3, axis=-1)
        q = q.reshape(b, l, nh, 2, hd).transpose(0, 2, 3, 1, 4)
        k = k.reshape(b, l, nh, 2, hd).transpose(0, 2, 3, 1, 4)
        v = v.reshape(b, l, nh, 2 * hd).transpose(0, 2, 1, 3)
        return q, k, v

    qx, kx, vx = qkv(hx)
    qc, kc, vc = qkv(hc)
    cos, sin = _axial_rope(n, hd)
    qx = _apply_rope(qx, cos, sin)
    kx = _apply_rope(kx, cos, sin)
    lam = (jnp.exp(jnp.sum(lam_vecs[0].astype(jnp.float32) * lam_vecs[1].astype(jnp.float32)))
           - jnp.exp(jnp.sum(lam_vecs[2].astype(jnp.float32) * lam_vecs[3].astype(jnp.float32)))
           + lam_init)
    scale = hd ** -0.5

    def attend(q, k, v):
        s = jnp.einsum('bhcqd,bhckd->bhcqk', q, k, preferred_element_type=jnp.float32) * scale
        p = jax.nn.softmax(s, axis=-1)
        a = p[:, :, 0] - lam * p[:, :, 1]
        return jnp.einsum('bhqk,bhkv->bhqv', a.astype(v.dtype), v)

    oc = attend(qc, kc, vc)
    k_all = jnp.concatenate([kx, kc], axis=3)
    v_all = jnp.concatenate([vx, vc], axis=2)
    nb = n // DIFF_BLOCK_Q
    q_blocks = jnp.moveaxis(qx.reshape(b, nh, 2, nb, DIFF_BLOCK_Q, hd), 3, 0)
    ox = lax.map(lambda qb: attend(qb, k_all, v_all), q_blocks)
    ox = jnp.moveaxis(ox, 0, 2).reshape(b, nh, n, 2 * hd)

    def finish(o):
        o = _rmsnorm(o, subln_g) * (1 - lam_init)
        return o.transpose(0, 2, 1, 3).reshape(b, o.shape[2], d) @ w_out

    return finish(ox), finish(oc)


def _gla_scan(q, k, v, g, s0):
    b, nh, l, _ = q.shape
    nc = l // GLA_CHUNK
    mask = jnp.tril(jnp.ones((GLA_CHUNK, GLA_CHUNK), dtype=bool))

    def to_chunks(t):
        return jnp.moveaxis(t.reshape(b, nh, nc, GLA_CHUNK, t.shape[-1]), 2, 0)

    def step(s, inp):
        qc, kc, vc, gc = inp
        bcum = jnp.cumsum(gc, axis=-2)
        blast = bcum[..., -1:, :]
        q_dec = qc * jnp.exp(bcum)
        k_inv = kc * jnp.exp(-bcum)
        k_tail = kc * jnp.exp(blast - bcum)
        att = jnp.where(mask, jnp.einsum('bhtd,bhsd->bhts', q_dec, k_inv), 0.0)
        o = jnp.einsum('bhtd,bhdv->bhtv', q_dec, s) + jnp.einsum('bhts,bhsv->bhtv', att, vc)
        s_new = jnp.exp(blast)[..., 0, :, None] * s + jnp.einsum('bhsd,bhsv->bhdv', k_tail, vc)
        return s_new, o

    s_final, o = lax.scan(step, s0, (to_chunks(q), to_chunks(k), to_chunks(v), to_chunks(g)))
    return jnp.moveaxis(o, 0, 2).reshape(b, nh, l, v.shape[-1]), s_final


def _gla_final_state(k, v, g):
    bcum = jnp.cumsum(g, axis=2)
    return jnp.einsum('bhsd,bhsv->bhdv', k * jnp.exp(bcum[:, :, -1:] - bcum), v)


def gla_mix(hx, hc, w_in, w_gate_up, b_gate, norm_g, w_out, need_ctx_out):
    b = hx.shape[0]
    splits = [GLA_DK, 2 * GLA_DK, 2 * GLA_DK + GLA_DV, 2 * GLA_DK + 2 * GLA_DV,
              2 * GLA_DK + 2 * GLA_DV + GLA_GATE_RANK]

    def project(h):
        l = h.shape[1]

        def heads(t, e):
            return t.reshape(b, l, GLA_HEADS, e).transpose(0, 2, 1, 3).astype(jnp.float32)

        q, k, v, r, gd_f, gd_b = jnp.split(h @ w_in, splits, axis=-1)

        def log_gate(gd, j):
            z = (gd @ w_gate_up[j] + b_gate[j]).astype(jnp.float32)
            return heads(jax.nn.log_sigmoid(z) / GLA_TAU, GLA_DK_HEAD)

        q = heads(q, GLA_DK_HEAD) * (GLA_DK_HEAD ** -0.5)
        return q, heads(k, GLA_DK_HEAD), heads(v, GLA_DV_HEAD), r, log_gate(gd_f, 0), log_gate(gd_b, 1)

    def flip(t):
        return jnp.flip(t, axis=2)

    def finish(o, r):
        o = _rmsnorm(o.astype(r.dtype), norm_g)
        l = o.shape[2]
        o = o.transpose(0, 2, 1, 3).reshape(b, l, GLA_DV)
        return (o * jax.nn.silu(r)) @ w_out

    qx, kx, vx, rx, gfx, gbx = project(hx)
    qc, kc, vc, rc, gfc, gbc = project(hc)
    zeros = jnp.zeros((b, GLA_HEADS, GLA_DK_HEAD, GLA_DV_HEAD), jnp.float32)
    if need_ctx_out:
        oc_f, sc_f = _gla_scan(qc, kc, vc, gfc, zeros)
        oc_b, sc_b = _gla_scan(flip(qc), flip(kc), flip(vc), flip(gbc), zeros)
        oc = finish(oc_f + flip(oc_b), rc)
    else:
        sc_f = _gla_final_state(kc, vc, gfc)
        sc_b = _gla_final_state(flip(kc), flip(vc), flip(gbc))
        oc = None
    ox_f, _ = _gla_scan(qx, kx, vx, gfx, sc_f)
    ox_b, _ = _gla_scan(flip(qx), flip(kx), flip(vx), flip(gbx), sc_b)
    return finish(ox_f + flip(ox_b), rx), oc


def setup_inputs(seed: int = 0) -> dict:
    key = jax.random.key(seed)
    keys = iter(jax.random.split(key, 32))
    f32 = jnp.float32
    d = D_MODEL

    def nrm(shape, scale):
        return jax.random.normal(next(keys), shape, f32) * scale

    def gain(shape):
        return 1.0 + 0.05 * jax.random.normal(next(keys), shape, f32)

    n_a, n_b, n_c, n_d = (_layer_count(m) for m in range(N_MIXERS))
    return {
        'x': nrm((BATCH, SEQ, d), 1.0),
        'c': nrm((BATCH, d), 1.0),
        'ctx': nrm((BATCH, CTX_LEN, d), 1.0),
        'c_ctx': nrm((d,), 1.0),
        'w_ada': nrm((DEPTH, d, N_ADA * d), 0.5 * d ** -0.5),
        'b_ada': nrm((DEPTH, N_ADA * d), 0.02),
        'g_norm_mix': gain((DEPTH, d)),
        'g_norm_ffn': gain((DEPTH, d)),
        'w_ffn_in': nrm((DEPTH, d, D_FF), d ** -0.5),
        'w_ffn_out': nrm((DEPTH, D_FF, d), D_FF ** -0.5),
        'g_final': gain((d,)),
        'fno_w_out': nrm((n_a, d, d), d ** -0.5),
        'gmlp_w_in': nrm((n_b, d, 2 * GMLP_WIDTH), d ** -0.5),
        'gmlp_ln_g': gain((n_b, GMLP_WIDTH)),
        'gmlp_ln_b': nrm((n_b, GMLP_WIDTH), 0.02),
        'gmlp_w_s': nrm((n_b, GMLP_HEADS, GMLP_CHUNK, GMLP_CHUNK), GMLP_CHUNK ** -0.5),
        'gmlp_b_s': gain((n_b, GMLP_HEADS, GMLP_CHUNK)),
        'gmlp_w_out': nrm((n_b, GMLP_WIDTH, d), GMLP_WIDTH ** -0.5),
        'diff_w_in': nrm((n_c, d, 3 * d), d ** -0.5),
        'diff_lambda': nrm((n_c, 4, DIFF_HEAD_DIM), 0.1),
        'diff_subln_g': gain((n_c, 2 * DIFF_HEAD_DIM)),
        'diff_w_out': nrm((n_c, d, d), d ** -0.5),
        'gla_w_in': nrm((n_d, d, GLA_IN), d ** -0.5),
        'gla_w_gate_up': nrm((n_d, 2, GLA_GATE_RANK, GLA_DK), GLA_GATE_RANK ** -0.5),
        'gla_b_gate': nrm((n_d, 2, GLA_DK), 0.1),
        'gla_norm_g': gain((n_d, GLA_DV_HEAD)),
        'gla_w_out': nrm((n_d, GLA_DV, d), GLA_DV ** -0.5),
    }


def reference(x, c, ctx, c_ctx, w_ada, b_ada, g_norm_mix, g_norm_ffn, w_ffn_in, w_ffn_out, g_final,
              fno_w_out, gmlp_w_in, gmlp_ln_g, gmlp_ln_b, gmlp_w_s, gmlp_b_s, gmlp_w_out,
              diff_w_in, diff_lambda, diff_subln_g, diff_w_out,
              gla_w_in, gla_w_gate_up, gla_b_gate, gla_norm_g, gla_w_out):
    silu_c = jax.nn.silu(c)
    silu_cc = jax.nn.silu(c_ctx)
    for i in range(DEPTH):
        kind, j = i % N_MIXERS, i // N_MIXERS
        last = i == DEPTH - 1
        mod_x = (silu_c @ w_ada[i] + b_ada[i])[:, None, :]
        mod_c = silu_cc @ w_ada[i] + b_ada[i]
        sh1, sc1, ga1, sh2, sc2, ga2 = jnp.split(mod_x, N_ADA, axis=-1)
        ch1, cs1, cg1, ch2, cs2, cg2 = jnp.split(mod_c, N_ADA, axis=-1)
        hx = _modulate(x, g_norm_mix[i], sh1, sc1)
        hc = _modulate(ctx, g_norm_mix[i], ch1, cs1)
        if kind == 0:
            ox, oc = fourier_mix(hx, fno_w_out[j]), fourier_mix(hc, fno_w_out[j])
        elif kind == 1:
            gp = (gmlp_w_in[j], gmlp_ln_g[j], gmlp_ln_b[j], gmlp_w_s[j], gmlp_b_s[j], gmlp_w_out[j])
            ox, oc = chunk_gmlp(hx, *gp), chunk_gmlp(hc, *gp)
        elif kind == 2:
            lam_init = 0.8 - 0.6 * math.exp(-0.3 * i)
            ox, oc = diff_attention(hx, hc, diff_w_in[j], diff_lambda[j], diff_subln_g[j],
                                    diff_w_out[j], lam_init)
        else:
            ox, oc = gla_mix(hx, hc, gla_w_in[j], gla_w_gate_up[j], gla_b_gate[j], gla_norm_g[j],
                             gla_w_out[j], not last)
        x = x + ga1 * ox
        x = x + ga2 * _sq_relu_mlp(_modulate(x, g_norm_ffn[i], sh2, sc2), w_ffn_in[i], w_ffn_out[i])
        if not last:
            ctx = ctx + cg1 * oc
            ctx = ctx + cg2 * _sq_relu_mlp(_modulate(ctx, g_norm_ffn[i], ch2, cs2), w_ffn_in[i], w_ffn_out[i])
    return _rmsnorm(x, g_final)
```

```python
import functools
import math

import numpy as np
import jax
import jax.numpy as jnp
from jax import lax
from jax.experimental import pallas as pl
from jax.experimental.pallas import tpu as pltpu

D_MODEL = 2048
BATCH = 2
SEQ = 4096
DEPTH = 4
GRID_W = 64
CTX_LEN = 256
N_MIXERS = 4
NORM_EPS = 1e-6
D_FF = 4 * D_MODEL
N_ADA = 6
FNO_GROUPS = 4
GMLP_CHUNK = 128
GMLP_HEADS = 16
GMLP_WIDTH = D_MODEL
DIFF_HEADS = 8
DIFF_HEAD_DIM = D_MODEL // DIFF_HEADS // 2
ROPE_THETA = 10000.0
GLA_HEADS = 4
GLA_DK = D_MODEL // 2
GLA_DV = D_MODEL
GLA_DK_HEAD = GLA_DK // GLA_HEADS
GLA_DV_HEAD = GLA_DV // GLA_HEADS
GLA_GATE_RANK = 16
GLA_TAU = 16.0
GLA_CHUNK = 64
LN_EPS = 1e-5

T_X = BATCH * SEQ
T_C = BATCH * CTX_LEN
T_ALL = T_X + T_C
TM = 512
NT_X = T_X // TM
NT_ALL = T_ALL // TM
TILES_PER_BATCH = SEQ // TM
MOD_ROWS = 16
CTX_GROUP = BATCH
LANE = 128

BF16 = jnp.bfloat16
F32 = jnp.float32
VMEM_LIMIT = 56 * 1024 * 1024


def _params(semantics):
    return pltpu.CompilerParams(dimension_semantics=semantics, vmem_limit_bytes=VMEM_LIMIT)


def _group_of_tile(i):
    return jnp.minimum(i // TILES_PER_BATCH, CTX_GROUP)


def _mod_spec(slot, width=D_MODEL, col=None):
    if col is None:
        return pl.BlockSpec((1, 1, width), lambda i, j: (_group_of_tile(i) * N_ADA + slot, 0, 0))
    return pl.BlockSpec((1, 1, width), lambda i, j: (_group_of_tile(i) * N_ADA + slot, 0, col(j)))


def _modulated(x, g, shift, scale):
    ms = jnp.mean(x * x, axis=-1, keepdims=True)
    y = x * lax.rsqrt(ms + NORM_EPS) * g
    return y * (1.0 + scale) + shift


def _ada_kernel(c_ref, w_ref, b_ref, o_ref):
    c = c_ref[...]
    s = (c * jax.nn.sigmoid(c)).astype(BF16)
    w = w_ref[0].astype(BF16)
    o_ref[0] = jnp.dot(s, w, preferred_element_type=F32) + b_ref[0]


def _ada_table(c8, w_ada, b_ada):
    tn = 1024
    n_out = N_ADA * D_MODEL
    return pl.pallas_call(
        _ada_kernel,
        grid=(DEPTH, n_out // tn),
        in_specs=[
            pl.BlockSpec((MOD_ROWS, D_MODEL), lambda l, j: (0, 0)),
            pl.BlockSpec((1, D_MODEL, tn), lambda l, j: (l, 0, j)),
            pl.BlockSpec((1, 1, tn), lambda l, j: (l, 0, j)),
        ],
        out_specs=pl.BlockSpec((1, MOD_ROWS, tn), lambda l, j: (l, 0, j)),
        out_shape=jax.ShapeDtypeStruct((DEPTH, MOD_ROWS, n_out), F32),
        compiler_params=_params(("arbitrary", "arbitrary")),
        name="ada_table",
    )(c8, w_ada, b_ada.reshape(DEPTH, 1, n_out))


def _gelu_exact(a):
    return 0.5 * a * (1.0 + lax.erf(a * (1.0 / math.sqrt(2.0))))


def _modmm_kernel(x_ref, g_ref, sh_ref, sc_ref, w_ref, *rest, epilogue, tn):
    if epilogue == "rope":
        cs_ref, sn_ref, o_ref, h_ref = rest
    else:
        o_ref, h_ref = rest
    i = pl.program_id(0)
    j = pl.program_id(1)

    @pl.when(j == 0)
    def _():
        h = _modulated(x_ref[...], g_ref[...], sh_ref[0], sc_ref[0])
        h_ref[...] = h.astype(BF16)

    acc = jnp.dot(h_ref[...], w_ref[...], preferred_element_type=F32)
    if epilogue == "gelu":
        o_ref[...] = _gelu_exact(acc).astype(o_ref.dtype)
    elif epilogue == "rope":
        rotate = jnp.logical_and(j < (2 * D_MODEL) // tn, i < NT_X)

        @pl.when(rotate)
        def _():
            cs = cs_ref[...]
            sn = sn_ref[...]
            for blk in range(tn // LANE):
                a = acc[:, blk * LANE:(blk + 1) * LANE]
                swapped = pltpu.roll(a, LANE // 2, 1)
                o_ref[:, blk * LANE:(blk + 1) * LANE] = (a * cs + swapped * sn).astype(o_ref.dtype)

        @pl.when(jnp.logical_not(rotate))
        def _():
            o_ref[...] = acc.astype(o_ref.dtype)
    else:
        o_ref[...] = acc.astype(o_ref.dtype)


def _modmm(x, g, mod, slot_shift, slot_scale, w, *, epilogue="plain", tn=512, rope=None, out_dtype=BF16):
    k, n_out = w.shape
    in_specs = [
        pl.BlockSpec((TM, k), lambda i, j: (i, 0)),
        pl.BlockSpec((1, k), lambda i, j: (0, 0)),
        _mod_spec(slot_shift),
        _mod_spec(slot_scale),
        pl.BlockSpec((k, tn), lambda i, j: (0, j)),
    ]
    args = [x, g.reshape(1, k), mod, mod, w]
    if epilogue == "rope":
        pos = lambda i, j: (jnp.where(i < NT_X, i % TILES_PER_BATCH, 0), 0)
        in_specs += [pl.BlockSpec((TM, LANE), pos), pl.BlockSpec((TM, LANE), pos)]
        args += list(rope)
    return pl.pallas_call(
        functools.partial(_modmm_kernel, epilogue=epilogue, tn=tn),
        grid=(NT_ALL, n_out // tn),
        in_specs=in_specs,
        out_specs=pl.BlockSpec((TM, tn), lambda i, j: (i, j)),
        out_shape=jax.ShapeDtypeStruct((T_ALL, n_out), out_dtype),
        scratch_shapes=[pltpu.VMEM((TM, k), BF16)],
        compiler_params=_params(("arbitrary", "arbitrary")),
        name="modmm_" + epilogue,
    )(*args)


def _resmm_kernel(a_ref, w_ref, res_ref, gate_ref, o_ref):
    acc = jnp.dot(a_ref[...], w_ref[...], preferred_element_type=F32)
    o_ref[...] = res_ref[...] + gate_ref[0] * acc


def _resmm(a, w, res, mod, slot_gate, *, n_tiles, tn=512):
    k, n_out = w.shape
    return pl.pallas_call(
        _resmm_kernel,
        grid=(n_tiles, n_out // tn),
        in_specs=[
            pl.BlockSpec((TM, k), lambda i, j: (i, 0)),
            pl.BlockSpec((k, tn), lambda i, j: (0, j)),
            pl.BlockSpec((TM, tn), lambda i, j: (i, j)),
            _mod_spec(slot_gate, width=tn, col=lambda j: j),
        ],
        out_specs=pl.BlockSpec((TM, tn), lambda i, j: (i, j)),
        out_shape=jax.ShapeDtypeStruct((n_tiles * TM, n_out), F32),
        compiler_params=_params(("arbitrary", "arbitrary")),
        name="resmm",
    )(a, w, res, mod)


def _ffn_kernel(x_ref, g_ref, sh_ref, sc_ref, gate_ref, w1_ref, w2_ref, *rest, final_norm):
    if final_norm:
        gf_ref, o_ref, h_ref, acc_ref = rest
    else:
        o_ref, h_ref, acc_ref = rest
    f = pl.program_id(1)

    @pl.when(f == 0)
    def _():
        h = _modulated(x_ref[...], g_ref[...], sh_ref[0], sc_ref[0])
        h_ref[...] = h.astype(BF16)
        acc_ref[...] = jnp.zeros_like(acc_ref)

    a = jnp.dot(h_ref[...], w1_ref[...], preferred_element_type=F32)
    a = jnp.square(jnp.maximum(a, 0.0)).astype(BF16)
    acc_ref[...] += jnp.dot(a, w2_ref[...], preferred_element_type=F32)

    @pl.when(f == pl.num_programs(1) - 1)
    def _():
        y = x_ref[...] + gate_ref[0] * acc_ref[...]
        if final_norm:
            ms = jnp.mean(y * y, axis=-1, keepdims=True)
            y = y * lax.rsqrt(ms + NORM_EPS) * gf_ref[...]
        o_ref[...] = y


def _ffn(x, g, mod, w1, w2, *, n_tiles, g_final=None, tf=512):
    final_norm = g_final is not None
    in_specs = [
        pl.BlockSpec((TM, D_MODEL), lambda i, f: (i, 0)),
        pl.BlockSpec((1, D_MODEL), lambda i, f: (0, 0)),
        _mod_spec(3),
        _mod_spec(4),
        _mod_spec(5),
        pl.BlockSpec((D_MODEL, tf), lambda i, f: (0, f)),
        pl.BlockSpec((tf, D_MODEL), lambda i, f: (f, 0)),
    ]
    args = [x, g.reshape(1, D_MODEL), mod, mod, mod, w1, w2]
    if final_norm:
        in_specs.append(pl.BlockSpec((1, D_MODEL), lambda i, f: (0, 0)))
        args.append(g_final.reshape(1, D_MODEL))
    return pl.pallas_call(
        functools.partial(_ffn_kernel, final_norm=final_norm),
        grid=(n_tiles, D_FF // tf),
        in_specs=in_specs,
        out_specs=pl.BlockSpec((TM, D_MODEL), lambda i, f: (i, 0)),
        out_shape=jax.ShapeDtypeStruct((n_tiles * TM, D_MODEL), F32),
        scratch_shapes=[pltpu.VMEM((TM, D_MODEL), BF16), pltpu.VMEM((TM, D_MODEL), F32)],
        compiler_params=_params(("arbitrary", "arbitrary")),
        name="ffn_final" if final_norm else "ffn",
    )(*args)


def _dft_tables(n):
    idx = np.arange(n, dtype=np.int64)
    ang = 2.0 * np.pi * ((idx[:, None] * idx[None, :]) % n).astype(np.float64) / n
    return np.cos(ang) / math.sqrt(n), np.sin(ang) / math.sqrt(n)


def _channel_dft_matrix():
    dg = D_MODEL // FNO_GROUPS
    c, s = _dft_tables(dg)
    eye = jnp.eye(FNO_GROUPS, dtype=F32)
    cc = jnp.kron(eye, jnp.asarray(c, F32))
    ss = jnp.kron(eye, jnp.asarray(s, F32))
    return jnp.concatenate([cc, ss], axis=1).astype(BF16)


def _seq_dft_matrices(n):
    if n <= 512:
        c, s = _dft_tables(n)
        return jnp.asarray(c, F32).astype(BF16), jnp.asarray(-s, F32).astype(BF16)
    r = 64
    q = n // r
    rows = np.arange(n, dtype=np.int64)
    ang_hi = 2.0 * np.pi * ((rows[:, None] * (r * np.arange(q))[None, :]) % n) / n
    ang_lo = 2.0 * np.pi * ((rows[:, None] * np.arange(r)[None, :]) % n) / n
    ch = jnp.asarray(np.cos(ang_hi) / math.sqrt(n), F32)[:, :, None]
    sh = jnp.asarray(np.sin(ang_hi) / math.sqrt(n), F32)[:, :, None]
    cl = jnp.asarray(np.cos(ang_lo), F32)[:, None, :]
    sl = jnp.asarray(np.sin(ang_lo), F32)[:, None, :]
    c = (ch * cl - sh * sl).reshape(n, n)
    s_neg = (-(sh * cl + ch * sl)).reshape(n, n)
    return c.astype(BF16), s_neg.astype(BF16)


def _seqdft_kernel(c_ref, s_ref, a_ref, b_ref, o_ref, acc_ref):
    kk = pl.program_id(3)

    @pl.when(kk == 0)
    def _():
        acc_ref[...] = jnp.zeros_like(acc_ref)

    acc_ref[...] += (jnp.dot(c_ref[...], a_ref[...], preferred_element_type=F32)
                     + jnp.dot(s_ref[...], b_ref[...], preferred_element_type=F32))

    @pl.when(kk == pl.num_programs(3) - 1)
    def _():
        o_ref[...] = acc_ref[...].astype(o_ref.dtype)


def _seqdft(ab, cmat, smat, *, n, row0):
    tmm = min(n, 1024)
    tk = min(n, 512)
    tn = 512
    nk = n // tk
    col_b = D_MODEL // tn
    return pl.pallas_call(
        _seqdft_kernel,
        grid=(BATCH, n // tmm, D_MODEL // tn, nk),
        in_specs=[
            pl.BlockSpec((tmm, tk), lambda b, m, j, k: (m, k)),
            pl.BlockSpec((tmm, tk), lambda b, m, j, k: (m, k)),
            pl.BlockSpec((tk, tn), lambda b, m, j, k: (row0 // tk + b * nk + k, j)),
            pl.BlockSpec((tk, tn), lambda b, m, j, k: (row0 // tk + b * nk + k, col_b + j)),
        ],
        out_specs=pl.BlockSpec((tmm, tn), lambda b, m, j, k: (b * (n // tmm) + m, j)),
        out_shape=jax.ShapeDtypeStruct((BATCH * n, D_MODEL), BF16),
        scratch_shapes=[pltpu.VMEM((tmm, tn), F32)],
        compiler_params=_params(("arbitrary",) * 4),
        name="seqdft",
    )(cmat, smat, ab, ab)


SGU_ROWS = 256


def _sgu_kernel(u_ref, v_ref, lg_ref, lb_ref, ws_ref, bs_ref, o_ref):
    v = v_ref[...].astype(F32)
    mu = jnp.mean(v, axis=-1, keepdims=True)
    vc = v - mu
    var = jnp.mean(vc * vc, axis=-1, keepdims=True)
    vn = (vc * lax.rsqrt(var + LN_EPS) * lg_ref[...] + lb_ref[...]).astype(BF16)
    we = GMLP_WIDTH // GMLP_HEADS
    for c in range(SGU_ROWS // GMLP_CHUNK):
        r0 = c * GMLP_CHUNK
        for h in range(GMLP_HEADS):
            s = jnp.dot(ws_ref[h], vn[r0:r0 + GMLP_CHUNK, h * we:(h + 1) * we], preferred_element_type=F32)
            s = s + bs_ref[:, h:h + 1]
            u = u_ref[r0:r0 + GMLP_CHUNK, h * we:(h + 1) * we].astype(F32)
            o_ref[r0:r0 + GMLP_CHUNK, h * we:(h + 1) * we] = (u * s).astype(o_ref.dtype)


def _sgu(z, ln_g, ln_b, w_s, b_s_t):
    return pl.pallas_call(
        _sgu_kernel,
        grid=(T_ALL // SGU_ROWS,),
        in_specs=[
            pl.BlockSpec((SGU_ROWS, GMLP_WIDTH), lambda i: (i, 0)),
            pl.BlockSpec((SGU_ROWS, GMLP_WIDTH), lambda i: (i, 1)),
            pl.BlockSpec((1, GMLP_WIDTH), lambda i: (0, 0)),
            pl.BlockSpec((1, GMLP_WIDTH), lambda i: (0, 0)),
            pl.BlockSpec((GMLP_HEADS, GMLP_CHUNK, GMLP_CHUNK), lambda i: (0, 0, 0)),
            pl.BlockSpec((GMLP_CHUNK, GMLP_HEADS), lambda i: (0, 0)),
        ],
        out_specs=pl.BlockSpec((SGU_ROWS, GMLP_WIDTH), lambda i: (i, 0)),
        out_shape=jax.ShapeDtypeStruct((T_ALL, GMLP_WIDTH), BF16),
        compiler_params=_params(("arbitrary",)),
        name="sgu",
    )(z, z, ln_g.reshape(1, -1), ln_b.reshape(1, -1), w_s, b_s_t)


def _dot_nt(a, b):
    return lax.dot_general(a, b, (((1,), (1,)), ((), ())), preferred_element_type=F32)


def _attn_kernel(lam_ref, g_ref, q_ref, *rest, latent, lam_init):
    if latent:
        kx_ref, vx_ref, kc_ref, vc_ref, o_ref = rest
    else:
        kc_ref, vc_ref, o_ref = rest
    hd = DIFF_HEAD_DIM
    lv = lam_ref[...]
    lam = (jnp.exp(jnp.sum(lv[0:1] * lv[1:2], axis=-1, keepdims=True))
           - jnp.exp(jnp.sum(lv[2:3] * lv[3:4], axis=-1, keepdims=True)) + lam_init)
    scale = hd ** -0.5
    q = q_ref[...]
    probs = []
    for comp in range(2):
        qc = q[:, comp * hd:(comp + 1) * hd]
        s_c = _dot_nt(qc, kc_ref[:, comp * hd:(comp + 1) * hd]) * scale
        m = jnp.max(s_c, axis=-1, keepdims=True)
        if latent:
            s_x = _dot_nt(qc, kx_ref[:, comp * hd:(comp + 1) * hd]) * scale
            m = jnp.maximum(m, jnp.max(s_x, axis=-1, keepdims=True))
            e_x = jnp.exp(s_x - m)
        e_c = jnp.exp(s_c - m)
        l = jnp.sum(e_c, axis=-1, keepdims=True)
        if latent:
            l = l + jnp.sum(e_x, axis=-1, keepdims=True)
        inv = 1.0 / l
        probs.append((e_x * inv if latent else None, e_c * inv))
    a_c = (probs[0][1] - lam * probs[1][1]).astype(BF16)
    o = jnp.dot(a_c, vc_ref[...], preferred_element_type=F32)
    if latent:
        a_x = (probs[0][0] - lam * probs[1][0]).astype(BF16)
        o = o + jnp.dot(a_x, vx_ref[...], preferred_element_type=F32)
    ms = jnp.mean(o * o, axis=-1, keepdims=True)
    o = o * lax.rsqrt(ms + NORM_EPS) * g_ref[...] * (1.0 - lam_init)
    o_ref[...] = o.astype(o_ref.dtype)


def _diff_attention(qkv, lam_vecs, subln_g, lam_init, *, latent, tq=256):
    hw = 2 * DIFF_HEAD_DIM
    k_col = D_MODEL // hw
    v_col = 2 * D_MODEL // hw
    n_q = SEQ if latent else CTX_LEN
    q_row0 = 0 if latent else T_X
    nq = n_q // tq
    in_specs = [
        pl.BlockSpec((4, DIFF_HEAD_DIM), lambda b, h, i: (0, 0)),
        pl.BlockSpec((1, hw), lambda b, h, i: (0, 0)),
        pl.BlockSpec((tq, hw), lambda b, h, i: (q_row0 // tq + b * nq + i, h)),
    ]
    args = [lam_vecs, subln_g.reshape(1, hw), qkv]
    if latent:
        in_specs += [
            pl.BlockSpec((SEQ, hw), lambda b, h, i: (b, k_col + h)),
            pl.BlockSpec((SEQ, hw), lambda b, h, i: (b, v_col + h)),
        ]
        args += [qkv, qkv]
    in_specs += [
        pl.BlockSpec((CTX_LEN, hw), lambda b, h, i: (T_X // CTX_LEN + b, k_col + h)),
        pl.BlockSpec((CTX_LEN, hw), lambda b, h, i: (T_X // CTX_LEN + b, v_col + h)),
    ]
    args += [qkv, qkv]
    return pl.pallas_call(
        functools.partial(_attn_kernel, latent=latent, lam_init=lam_init),
        grid=(BATCH, DIFF_HEADS, nq),
        in_specs=in_specs,
        out_specs=pl.BlockSpec((tq, hw), lambda b, h, i: (b * nq + i, h)),
        out_shape=jax.ShapeDtypeStruct((BATCH * n_q, D_MODEL), BF16),
        compiler_params=_params(("arbitrary",) * 3),
        name="diffattn_latent" if latent else "diffattn_ctx",
    )(*args)


GLA_CTX_CHUNKS = CTX_LEN // GLA_CHUNK
GLA_X_CHUNKS = SEQ // GLA_CHUNK
GLA_STEPS = GLA_CTX_CHUNKS + GLA_X_CHUNKS


def _gla_scan_kernel(q_ref, k_ref, v_ref, gd_ref, wup_ref, bg_ref, tri_ref, o_ref, st_ref):
    step = pl.program_id(3)

    @pl.when(step == 0)
    def _():
        st_ref[...] = jnp.zeros_like(st_ref)

    tri = tri_ref[0]
    z = jnp.dot(gd_ref[...], wup_ref[0], preferred_element_type=F32) + bg_ref[0]
    g = (jnp.minimum(z, 0.0) - jnp.log(1.0 + jnp.exp(-jnp.abs(z)))) * (1.0 / GLA_TAU)
    bcum = jnp.dot(tri, g, preferred_element_type=F32, precision=lax.Precision.HIGHEST)
    blast = jnp.sum(g, axis=0, keepdims=True)
    q = q_ref[...].astype(F32) * (GLA_DK_HEAD ** -0.5)
    k = k_ref[...].astype(F32)
    v = v_ref[...]
    q_dec = (q * jnp.exp(bcum)).astype(BF16)
    k_inv = (k * jnp.exp(-bcum)).astype(BF16)
    k_tail = (k * jnp.exp(blast - bcum)).astype(BF16)
    att = jnp.where(tri > 0.0, _dot_nt(q_dec, k_inv), 0.0).astype(BF16)
    st = st_ref[...]
    o = _dot_nt(q_dec, st.astype(BF16)) + jnp.dot(att, v, preferred_element_type=F32)
    o_ref[0] = o.astype(o_ref.dtype)
    upd = lax.dot_general(v, k_tail, (((0,), (0,)), ((), ())), preferred_element_type=F32)
    st_ref[...] = jnp.exp(blast) * st + upd


def _gla_scan(proj, gates, wup_pad, b_gate, tri):
    dk, dv = GLA_DK_HEAD, GLA_DV_HEAD

    def row_block(b, d, s):
        c_idx = jnp.where(d == 0, s, GLA_CTX_CHUNKS - 1 - s)
        x_idx = jnp.where(d == 0, s - GLA_CTX_CHUNKS, GLA_STEPS - 1 - s)
        return jnp.where(s < GLA_CTX_CHUNKS,
                         T_X // GLA_CHUNK + b * GLA_CTX_CHUNKS + c_idx,
                         b * GLA_X_CHUNKS + x_idx)

    return pl.pallas_call(
        _gla_scan_kernel,
        grid=(BATCH, GLA_HEADS, 2, GLA_STEPS),
        in_specs=[
            pl.BlockSpec((GLA_CHUNK, dk), lambda b, h, d, s: (row_block(b, d, s), h)),
            pl.BlockSpec((GLA_CHUNK, dk), lambda b, h, d, s: (row_block(b, d, s), GLA_DK // dk + h)),
            pl.BlockSpec((GLA_CHUNK, dv), lambda b, h, d, s: (row_block(b, d, s), 2 * GLA_DK // dv + h)),
            pl.BlockSpec((GLA_CHUNK, LANE), lambda b, h, d, s: (row_block(b, d, s), 0)),
            pl.BlockSpec((1, LANE, dk), lambda b, h, d, s: (d, 0, h)),
            pl.BlockSpec((1, 1, dk), lambda b, h, d, s: (d, 0, h)),
            pl.BlockSpec((1, GLA_CHUNK, GLA_CHUNK), lambda b, h, d, s: (d, 0, 0)),
        ],
        out_specs=pl.BlockSpec((1, GLA_CHUNK, dv), lambda b, h, d, s: (d, row_block(b, d, s), h)),
        out_shape=jax.ShapeDtypeStruct((2, T_ALL, GLA_DV), BF16),
        scratch_shapes=[pltpu.VMEM((dv, dk), F32)],
        compiler_params=_params(("arbitrary",) * 4),
        name="gla_scan",
    )(proj, proj, proj, gates, wup_pad, b_gate, tri)


def _gla_finish_kernel(o_ref, r_ref, g_ref, y_ref):
    o = o_ref[0].astype(F32) + o_ref[1].astype(F32)
    r = r_ref[...].astype(F32)
    gate = r * jax.nn.sigmoid(r)
    dv = GLA_DV_HEAD
    for h in range(GLA_HEADS):
        oh = o[:, h * dv:(h + 1) * dv]
        ms = jnp.mean(oh * oh, axis=-1, keepdims=True)
        y = oh * lax.rsqrt(ms + NORM_EPS) * g_ref[...]
        y_ref[:, h * dv:(h + 1) * dv] = (y * gate[:, h * dv:(h + 1) * dv]).astype(y_ref.dtype)


def _gla_finish(o_dirs, proj, norm_g, *, n_tiles):
    r_col = (2 * GLA_DK + GLA_DV) // GLA_DV
    return pl.pallas_call(
        _gla_finish_kernel,
        grid=(n_tiles,),
        in_specs=[
            pl.BlockSpec((2, TM, GLA_DV), lambda i: (0, i, 0)),
            pl.BlockSpec((TM, GLA_DV), lambda i: (i, r_col)),
            pl.BlockSpec((1, GLA_DV_HEAD), lambda i: (0, 0)),
        ],
        out_specs=pl.BlockSpec((TM, GLA_DV), lambda i: (i, 0)),
        out_shape=jax.ShapeDtypeStruct((n_tiles * TM, GLA_DV), BF16),
        compiler_params=_params(("arbitrary",)),
        name="gla_finish",
    )(o_dirs, proj, norm_g.reshape(1, -1))


def _rope_tables():
    hd = DIFF_HEAD_DIM
    pos = np.arange(SEQ)
    row = (pos // GRID_W).astype(np.float32)
    col = (pos % GRID_W).astype(np.float32)
    n_freq = hd // 4
    inv = (ROPE_THETA ** (-np.arange(n_freq, dtype=np.float32) / n_freq)).astype(np.float32)
    ang = np.concatenate([row[:, None] * inv, col[:, None] * inv], axis=-1).astype(np.float64)
    cos, sin = np.cos(ang), np.sin(ang)
    cs = np.concatenate([cos, cos], axis=-1).astype(np.float32)
    sn = np.concatenate([-sin, sin], axis=-1).astype(np.float32)
    return jnp.asarray(cs), jnp.asarray(sn)


def _gla_tri():
    lower = np.tril(np.ones((GLA_CHUNK, GLA_CHUNK), np.float32))
    return jnp.asarray(np.stack([lower, lower.T]))


def _mixer(i, tok, mod, g_mix, w, n_tiles):
    kind = i % N_MIXERS
    if kind == 0:
        ab = _modmm(tok, g_mix, mod, 0, 1, _channel_dft_matrix())
        zx = _seqdft(ab, *_seq_dft_matrices(SEQ), n=SEQ, row0=0)
        zc = _seqdft(ab, *_seq_dft_matrices(CTX_LEN), n=CTX_LEN, row0=T_X)
        return jnp.concatenate([zx, zc], axis=0), w["fno_w_out"]
    if kind == 1:
        z = _modmm(tok, g_mix, mod, 0, 1, w["gmlp_w_in"].astype(BF16), epilogue="gelu")
        mixed = _sgu(z, w["gmlp_ln_g"], w["gmlp_ln_b"], w["gmlp_w_s"].astype(BF16), w["gmlp_b_s"].T)
        return mixed, w["gmlp_w_out"]
    if kind == 2:
        lam_init = 0.8 - 0.6 * math.exp(-0.3 * i)
        qkv = _modmm(tok, g_mix, mod, 0, 1, w["diff_w_in"].astype(BF16), epilogue="rope", rope=_rope_tables())
        ox = _diff_attention(qkv, w["diff_lambda"], w["diff_subln_g"], lam_init, latent=True)
        oc = _diff_attention(qkv, w["diff_lambda"], w["diff_subln_g"], lam_init, latent=False)
        return jnp.concatenate([ox, oc], axis=0), w["diff_w_out"]
    n_main = 2 * GLA_DK + 2 * GLA_DV
    w_in = w["gla_w_in"]
    proj = _modmm(tok, g_mix, mod, 0, 1, w_in[:, :n_main].astype(BF16))
    w_gate = jnp.pad(w_in[:, n_main:], ((0, 0), (0, LANE - 2 * GLA_GATE_RANK))).astype(BF16)
    gates = _modmm(tok, g_mix, mod, 0, 1, w_gate, tn=LANE)
    wup = w["gla_w_gate_up"]
    wup_pad = jnp.zeros((2, LANE, GLA_DK), F32)
    wup_pad = wup_pad.at[0, :GLA_GATE_RANK].set(wup[0])
    wup_pad = wup_pad.at[1, GLA_GATE_RANK:2 * GLA_GATE_RANK].set(wup[1])
    o_dirs = _gla_scan(proj, gates, wup_pad.astype(BF16), w["gla_b_gate"].reshape(2, 1, GLA_DK), _gla_tri())
    return _gla_finish(o_dirs, proj, w["gla_norm_g"], n_tiles=n_tiles), w["gla_w_out"]


def _layer(i, tok, mod, w, g_final=None):
    n_tiles = NT_X if i == DEPTH - 1 else NT_ALL
    mixed, w_out = _mixer(i, tok, mod, w["g_norm_mix"], w, n_tiles)
    tok = _resmm(mixed, w_out.astype(BF16), tok, mod, 2, n_tiles=n_tiles)
    return _ffn(tok, w["g_norm_ffn"], mod, w["w_ffn_in"].astype(BF16), w["w_ffn_out"].astype(BF16),
                n_tiles=n_tiles, g_final=g_final)


def kernel(x, c, ctx, c_ctx, w_ada, b_ada, g_norm_mix, g_norm_ffn, w_ffn_in, w_ffn_out, g_final, fno_w_out, gmlp_w_in, gmlp_ln_g, gmlp_ln_b, gmlp_w_s, gmlp_b_s, gmlp_w_out, diff_w_in, diff_lambda, diff_subln_g, diff_w_out, gla_w_in, gla_w_gate_up, gla_b_gate, gla_norm_g, gla_w_out):
    tok = jnp.concatenate([x.reshape(T_X, D_MODEL), ctx.reshape(T_C, D_MODEL)], axis=0)
    c_rows = jnp.concatenate([c, c_ctx[None, :], jnp.zeros((MOD_ROWS - BATCH - 1, D_MODEL), F32)], axis=0)
    mod_all = _ada_table(c_rows, w_ada, b_ada).reshape(DEPTH, MOD_ROWS * N_ADA, 1, D_MODEL)
    mixer_weights = (
        dict(fno_w_out=fno_w_out),
        dict(gmlp_w_in=gmlp_w_in, gmlp_ln_g=gmlp_ln_g, gmlp_ln_b=gmlp_ln_b, gmlp_w_s=gmlp_w_s,
             gmlp_b_s=gmlp_b_s, gmlp_w_out=gmlp_w_out),
        dict(diff_w_in=diff_w_in, diff_lambda=diff_lambda, diff_subln_g=diff_subln_g, diff_w_out=diff_w_out),
        dict(gla_w_in=gla_w_in, gla_w_gate_up=gla_w_gate_up, gla_b_gate=gla_b_gate, gla_norm_g=gla_norm_g,
             gla_w_out=gla_w_out),
    )
    for i in range(DEPTH):
        kind, j = i % N_MIXERS, i // N_MIXERS
        w = {name: arr[j] for name, arr in mixer_weights[kind].items()}
        w.update(g_norm_mix=g_norm_mix[i], g_norm_ffn=g_norm_ffn[i], w_ffn_in=w_ffn_in[i], w_ffn_out=w_ffn_out[i])
        tok = _layer(i, tok, mod_all[i], w, g_final=g_final if i == DEPTH - 1 else None)
    return tok.reshape(BATCH, SEQ, D_MODEL)
```

```python
import functools
import math

import numpy as np
import jax
import jax.numpy as jnp
from jax import lax
from jax.experimental import pallas as pl
from jax.experimental.pallas import tpu as pltpu

D_MODEL = 2048
BATCH = 2
SEQ = 4096
DEPTH = 4
GRID_W = 64
CTX_LEN = 256
N_MIXERS = 4
NORM_EPS = 1e-6
D_FF = 4 * D_MODEL
N_ADA = 6
FNO_GROUPS = 4
GMLP_CHUNK = 128
GMLP_HEADS = 16
GMLP_WIDTH = D_MODEL
DIFF_HEADS = 8
DIFF_HEAD_DIM = D_MODEL // DIFF_HEADS // 2
ROPE_THETA = 10000.0
GLA_HEADS = 4
GLA_DK = D_MODEL // 2
GLA_DV = D_MODEL
GLA_DK_HEAD = GLA_DK // GLA_HEADS
GLA_DV_HEAD = GLA_DV // GLA_HEADS
GLA_GATE_RANK = 16
GLA_TAU = 16.0
GLA_CHUNK = 64
LN_EPS = 1e-5

T_X = BATCH * SEQ
T_C = BATCH * CTX_LEN
T_ALL = T_X + T_C
TM = 512
NT_X = T_X // TM
NT_ALL = T_ALL // TM
TILES_PER_BATCH = SEQ // TM
MOD_ROWS = 16
CTX_GROUP = BATCH
LANE = 128
MM_TN = 2048
FFN_TF = 1024

BF16 = jnp.bfloat16
F32 = jnp.float32
VMEM_LIMIT = 56 * 1024 * 1024


def _params(semantics):
    return pltpu.CompilerParams(dimension_semantics=semantics, vmem_limit_bytes=VMEM_LIMIT)


def _group_of_tile(i):
    return jnp.minimum(i // TILES_PER_BATCH, CTX_GROUP)


def _mod_spec(slot):
    return pl.BlockSpec((1, 1, D_MODEL), lambda i, *_: (_group_of_tile(i) * N_ADA + slot, 0, 0))


def _modulated(x, g, shift, scale):
    ms = jnp.mean(x * x, axis=-1, keepdims=True)
    y = x * lax.rsqrt(ms + NORM_EPS) * g
    return y * (1.0 + scale) + shift


def _ada_kernel(c_ref, w_ref, b_ref, o_ref):
    c = c_ref[...]
    s = (c * jax.nn.sigmoid(c)).astype(BF16)
    w = w_ref[0].astype(BF16)
    o_ref[0] = jnp.dot(s, w, preferred_element_type=F32) + b_ref[0]


def _ada_table(c8, w_ada, b_ada):
    tn = 1024
    n_out = N_ADA * D_MODEL
    return pl.pallas_call(
        _ada_kernel,
        grid=(DEPTH, n_out // tn),
        in_specs=[
            pl.BlockSpec((MOD_ROWS, D_MODEL), lambda l, j: (0, 0)),
            pl.BlockSpec((1, D_MODEL, tn), lambda l, j: (l, 0, j)),
            pl.BlockSpec((1, 1, tn), lambda l, j: (l, 0, j)),
        ],
        out_specs=pl.BlockSpec((1, MOD_ROWS, tn), lambda l, j: (l, 0, j)),
        out_shape=jax.ShapeDtypeStruct((DEPTH, MOD_ROWS, n_out), F32),
        compiler_params=_params(("arbitrary", "arbitrary")),
        name="ada_table",
    )(c8, w_ada, b_ada.reshape(DEPTH, 1, n_out))


def _gelu_exact(a):
    return 0.5 * a * (1.0 + lax.erf(a * (1.0 / math.sqrt(2.0))))


def _modmm_kernel(x_ref, g_ref, sh_ref, sc_ref, w_ref, *rest, epilogue, tn):
    if epilogue == "rope":
        cs_ref, sn_ref, o_ref, h_ref = rest
    else:
        o_ref, h_ref = rest
    i = pl.program_id(0)
    j = pl.program_id(1)

    @pl.when(j == 0)
    def _():
        h = _modulated(x_ref[...], g_ref[...], sh_ref[0], sc_ref[0])
        h_ref[...] = h.astype(BF16)

    acc = jnp.dot(h_ref[...], w_ref[...], preferred_element_type=F32)
    if epilogue == "gelu":
        o_ref[...] = _gelu_exact(acc).astype(o_ref.dtype)
    elif epilogue == "rope":
        rotate = jnp.logical_and(j < (2 * D_MODEL) // tn, i < NT_X)

        @pl.when(rotate)
        def _():
            cs = cs_ref[...]
            sn = sn_ref[...]
            for blk in range(tn // LANE):
                a = acc[:, blk * LANE:(blk + 1) * LANE]
                swapped = pltpu.roll(a, LANE // 2, 1)
                o_ref[:, blk * LANE:(blk + 1) * LANE] = (a * cs + swapped * sn).astype(o_ref.dtype)

        @pl.when(jnp.logical_not(rotate))
        def _():
            o_ref[...] = acc.astype(o_ref.dtype)
    else:
        o_ref[...] = acc.astype(o_ref.dtype)


def _modmm(x, g, mod, slot_shift, slot_scale, w, *, epilogue="plain", rope=None, out_dtype=BF16):
    k, n_out = w.shape
    tn = min(MM_TN, n_out)
    in_specs = [
        pl.BlockSpec((TM, k), lambda i, j: (i, 0)),
        pl.BlockSpec((1, k), lambda i, j: (0, 0)),
        _mod_spec(slot_shift),
        _mod_spec(slot_scale),
        pl.BlockSpec((k, tn), lambda i, j: (0, j)),
    ]
    args = [x, g.reshape(1, k), mod, mod, w]
    if epilogue == "rope":
        pos = lambda i, j: (jnp.where(i < NT_X, i % TILES_PER_BATCH, 0), 0)
        in_specs += [pl.BlockSpec((TM, LANE), pos), pl.BlockSpec((TM, LANE), pos)]
        args += list(rope)
    return pl.pallas_call(
        functools.partial(_modmm_kernel, epilogue=epilogue, tn=tn),
        grid=(NT_ALL, n_out // tn),
        in_specs=in_specs,
        out_specs=pl.BlockSpec((TM, tn), lambda i, j: (i, j)),
        out_shape=jax.ShapeDtypeStruct((T_ALL, n_out), out_dtype),
        scratch_shapes=[pltpu.VMEM((TM, k), BF16)],
        compiler_params=_params(("arbitrary", "arbitrary")),
        name="modmm_" + epilogue,
    )(*args)


def _resmm_kernel(a_ref, w_ref, res_ref, gate_ref, o_ref):
    acc = jnp.dot(a_ref[...], w_ref[...], preferred_element_type=F32)
    o_ref[...] = res_ref[...] + gate_ref[0] * acc


def _resmm(a, w, res, mod, slot_gate, *, n_tiles):
    k, n_out = w.shape
    return pl.pallas_call(
        _resmm_kernel,
        grid=(n_tiles,),
        in_specs=[
            pl.BlockSpec((TM, k), lambda i: (i, 0)),
            pl.BlockSpec((k, n_out), lambda i: (0, 0)),
            pl.BlockSpec((TM, n_out), lambda i: (i, 0)),
            _mod_spec(slot_gate),
        ],
        out_specs=pl.BlockSpec((TM, n_out), lambda i: (i, 0)),
        out_shape=jax.ShapeDtypeStruct((n_tiles * TM, n_out), F32),
        compiler_params=_params(("arbitrary",)),
        name="resmm",
    )(a, w, res, mod)


def _ffn_kernel(x_ref, g_ref, sh_ref, sc_ref, gate_ref, w1_ref, w2_ref, *rest, final_norm):
    if final_norm:
        gf_ref, o_ref, h_ref, acc_ref = rest
    else:
        o_ref, h_ref, acc_ref = rest
    f = pl.program_id(1)

    @pl.when(f == 0)
    def _():
        h = _modulated(x_ref[...], g_ref[...], sh_ref[0], sc_ref[0])
        h_ref[...] = h.astype(BF16)
        acc_ref[...] = jnp.zeros_like(acc_ref)

    a = jnp.dot(h_ref[...], w1_ref[0, 0], preferred_element_type=F32)
    a = jnp.square(jnp.maximum(a, 0.0)).astype(BF16)
    acc_ref[...] += jnp.dot(a, w2_ref[0], preferred_element_type=F32)

    @pl.when(f == pl.num_programs(1) - 1)
    def _():
        y = x_ref[...] + gate_ref[0] * acc_ref[...]
        if final_norm:
            ms = jnp.mean(y * y, axis=-1, keepdims=True)
            y = y * lax.rsqrt(ms + NORM_EPS) * gf_ref[...]
        o_ref[...] = y


def _ffn_weights(w_ffn_in, w_ffn_out):
    nf = D_FF // FFN_TF
    w1 = w_ffn_in.reshape(DEPTH, D_MODEL, nf, FFN_TF).transpose(0, 2, 1, 3).astype(BF16)
    return w1, w_ffn_out.astype(BF16)


def _ffn(x, g, mod, w1, w2, layer, *, n_tiles, g_final=None):
    final_norm = g_final is not None
    tf = FFN_TF
    in_specs = [
        pl.BlockSpec((TM, D_MODEL), lambda i, f: (i, 0)),
        pl.BlockSpec((1, D_MODEL), lambda i, f: (0, 0)),
        _mod_spec(3),
        _mod_spec(4),
        _mod_spec(5),
        pl.BlockSpec((1, 1, D_MODEL, tf), lambda i, f: (layer, f, 0, 0)),
        pl.BlockSpec((1, tf, D_MODEL), lambda i, f: (layer, f, 0)),
    ]
    args = [x, g.reshape(1, D_MODEL), mod, mod, mod, w1, w2]
    if final_norm:
        in_specs.append(pl.BlockSpec((1, D_MODEL), lambda i, f: (0, 0)))
        args.append(g_final.reshape(1, D_MODEL))
    return pl.pallas_call(
        functools.partial(_ffn_kernel, final_norm=final_norm),
        grid=(n_tiles, D_FF // tf),
        in_specs=in_specs,
        out_specs=pl.BlockSpec((TM, D_MODEL), lambda i, f: (i, 0)),
        out_shape=jax.ShapeDtypeStruct((n_tiles * TM, D_MODEL), F32),
        scratch_shapes=[pltpu.VMEM((TM, D_MODEL), BF16), pltpu.VMEM((TM, D_MODEL), F32)],
        compiler_params=_params(("arbitrary", "arbitrary")),
        name="ffn_final" if final_norm else "ffn",
    )(*args)


def _dft_tables(n):
    idx = np.arange(n, dtype=np.int64)
    ang = 2.0 * np.pi * ((idx[:, None] * idx[None, :]) % n).astype(np.float64) / n
    return np.cos(ang) / math.sqrt(n), np.sin(ang) / math.sqrt(n)


def _channel_dft_matrix():
    dg = D_MODEL // FNO_GROUPS
    c, s = _dft_tables(dg)
    eye = jnp.eye(FNO_GROUPS, dtype=F32)
    cc = jnp.kron(eye, jnp.asarray(c, F32))
    ss = jnp.kron(eye, jnp.asarray(s, F32))
    return jnp.concatenate([cc, ss], axis=1).astype(BF16)


def _seq_dft_matrices(n):
    if n <= 512:
        c, s = _dft_tables(n)
        return jnp.asarray(c, F32).astype(BF16), jnp.asarray(-s, F32).astype(BF16)
    r = 64
    q = n // r
    cols = np.arange(n, dtype=np.int64)
    ang_hi = 2.0 * np.pi * (((r * np.arange(q))[:, None] * cols[None, :]) % n) / n
    ang_lo = 2.0 * np.pi * ((np.arange(r)[:, None] * cols[None, :]) % n) / n
    ch = jnp.asarray(np.cos(ang_hi) / math.sqrt(n), F32)[:, None, :]
    sh = jnp.asarray(np.sin(ang_hi) / math.sqrt(n), F32)[:, None, :]
    cl = jnp.asarray(np.cos(ang_lo), F32)[None, :, :]
    sl = jnp.asarray(np.sin(ang_lo), F32)[None, :, :]
    c = (ch * cl - sh * sl).astype(BF16).reshape(n, n)
    s_neg = (-(sh * cl + ch * sl)).astype(BF16).reshape(n, n)
    return c, s_neg


def _seqdft_kernel(c_ref, s_ref, a_ref, b_ref, o_ref, acc_ref):
    kk = pl.program_id(3)

    @pl.when(kk == 0)
    def _():
        acc_ref[...] = jnp.zeros_like(acc_ref)

    acc_ref[...] += (jnp.dot(c_ref[...], a_ref[...], preferred_element_type=F32)
                     + jnp.dot(s_ref[...], b_ref[...], preferred_element_type=F32))

    @pl.when(kk == pl.num_programs(3) - 1)
    def _():
        o_ref[...] = acc_ref[...].astype(o_ref.dtype)


def _seqdft(ab, cmat, smat, *, n, row0):
    tmm = min(n, 2048)
    tk = min(n, 512)
    tn = 1024
    nk = n // tk
    col_b = D_MODEL // tn
    return pl.pallas_call(
        _seqdft_kernel,
        grid=(BATCH, n // tmm, D_MODEL // tn, nk),
        in_specs=[
            pl.BlockSpec((tmm, tk), lambda b, m, j, k: (m, k)),
            pl.BlockSpec((tmm, tk), lambda b, m, j, k: (m, k)),
            pl.BlockSpec((tk, tn), lambda b, m, j, k: (row0 // tk + b * nk + k, j)),
            pl.BlockSpec((tk, tn), lambda b, m, j, k: (row0 // tk + b * nk + k, col_b + j)),
        ],
        out_specs=pl.BlockSpec((tmm, tn), lambda b, m, j, k: (b * (n // tmm) + m, j)),
        out_shape=jax.ShapeDtypeStruct((BATCH * n, D_MODEL), BF16),
        scratch_shapes=[pltpu.VMEM((tmm, tn), F32)],
        compiler_params=_params(("arbitrary",) * 4),
        name="seqdft",
    )(cmat, smat, ab, ab)


SGU_ROWS = 256


def _sgu_kernel(u_ref, v_ref, lg_ref, lb_ref, ws_ref, bs_ref, o_ref):
    v = v_ref[...].astype(F32)
    mu = jnp.mean(v, axis=-1, keepdims=True)
    vc = v - mu
    var = jnp.mean(vc * vc, axis=-1, keepdims=True)
    vn = (vc * lax.rsqrt(var + LN_EPS) * lg_ref[...] + lb_ref[...]).astype(BF16)
    we = GMLP_WIDTH // GMLP_HEADS
    for c in range(SGU_ROWS // GMLP_CHUNK):
        r0 = c * GMLP_CHUNK
        for h in range(GMLP_HEADS):
            s = jnp.dot(ws_ref[h], vn[r0:r0 + GMLP_CHUNK, h * we:(h + 1) * we], preferred_element_type=F32)
            s = s + bs_ref[:, h:h + 1]
            u = u_ref[r0:r0 + GMLP_CHUNK, h * we:(h + 1) * we].astype(F32)
            o_ref[r0:r0 + GMLP_CHUNK, h * we:(h + 1) * we] = (u * s).astype(o_ref.dtype)


def _sgu(z, ln_g, ln_b, w_s, b_s_t):
    return pl.pallas_call(
        _sgu_kernel,
        grid=(T_ALL // SGU_ROWS,),
        in_specs=[
            pl.BlockSpec((SGU_ROWS, GMLP_WIDTH), lambda i: (i, 0)),
            pl.BlockSpec((SGU_ROWS, GMLP_WIDTH), lambda i: (i, 1)),
            pl.BlockSpec((1, GMLP_WIDTH), lambda i: (0, 0)),
            pl.BlockSpec((1, GMLP_WIDTH), lambda i: (0, 0)),
            pl.BlockSpec((GMLP_HEADS, GMLP_CHUNK, GMLP_CHUNK), lambda i: (0, 0, 0)),
            pl.BlockSpec((GMLP_CHUNK, GMLP_HEADS), lambda i: (0, 0)),
        ],
        out_specs=pl.BlockSpec((SGU_ROWS, GMLP_WIDTH), lambda i: (i, 0)),
        out_shape=jax.ShapeDtypeStruct((T_ALL, GMLP_WIDTH), BF16),
        compiler_params=_params(("arbitrary",)),
        name="sgu",
    )(z, z, ln_g.reshape(1, -1), ln_b.reshape(1, -1), w_s, b_s_t)


def _dot_nt(a, b):
    return lax.dot_general(a, b, (((1,), (1,)), ((), ())), preferred_element_type=F32)


ATTN_KEY_CHUNK = 256


def _attn_kernel(lam_ref, g_ref, q_ref, *rest, latent, lam_init):
    if latent:
        kx_ref, vx_ref, kc_ref, vc_ref, o_ref, s_ref = rest
        chunks = [(kx_ref, vx_ref, r) for r in range(0, SEQ, ATTN_KEY_CHUNK)]
    else:
        kc_ref, vc_ref, o_ref, s_ref = rest
        chunks = []
    chunks += [(kc_ref, vc_ref, r) for r in range(0, CTX_LEN, ATTN_KEY_CHUNK)]
    hd = DIFF_HEAD_DIM
    kc = ATTN_KEY_CHUNK
    lv = lam_ref[...]
    lam = (jnp.exp(jnp.sum(lv[0:1] * lv[1:2], axis=-1, keepdims=True))
           - jnp.exp(jnp.sum(lv[2:3] * lv[3:4], axis=-1, keepdims=True)) + lam_init)
    exp2_scale = (hd ** -0.5) * math.log2(math.e)
    q = q_ref[...]
    comps = (0, 1)
    qs = [q[:, c * hd:(c + 1) * hd] for c in comps]
    m = [None, None]
    for j, (k_ref, _, r0) in enumerate(chunks):
        for c in comps:
            s = _dot_nt(qs[c], k_ref[r0:r0 + kc, c * hd:(c + 1) * hd])
            s_ref[c, :, j * kc:(j + 1) * kc] = s
            sm = jnp.maximum(s[:, :LANE], s[:, LANE:])
            m[c] = sm if m[c] is None else jnp.maximum(m[c], sm)
    m = [jnp.max(mc, axis=-1, keepdims=True) for mc in m]
    l = [None, None]
    acc = [None, None]

    def exp_chunk(j, c):
        e = jnp.exp2((s_ref[c, :, j * kc:(j + 1) * kc] - m[c]) * exp2_scale)
        ep = e[:, :LANE] + e[:, LANE:]
        l[c] = ep if l[c] is None else l[c] + ep
        return e.astype(BF16)

    ready = [exp_chunk(0, c) for c in comps]
    for j, (_, v_ref, r0) in enumerate(chunks):
        upcoming = [exp_chunk(j + 1, c) for c in comps] if j + 1 < len(chunks) else None
        for c in comps:
            pv = jnp.dot(ready[c], v_ref[r0:r0 + kc, :], preferred_element_type=F32)
            acc[c] = pv if acc[c] is None else acc[c] + pv
        ready = upcoming
    outs = [acc[c] * (1.0 / jnp.sum(l[c], axis=-1, keepdims=True)) for c in comps]
    o = outs[0] - lam * outs[1]
    ms = jnp.mean(o * o, axis=-1, keepdims=True)
    o = o * lax.rsqrt(ms + NORM_EPS) * g_ref[...] * (1.0 - lam_init)
    o_ref[...] = o.astype(o_ref.dtype)


def _diff_attention(qkv, lam_vecs, subln_g, lam_init, *, latent, tq=256):
    hw = 2 * DIFF_HEAD_DIM
    k_col = D_MODEL // hw
    v_col = 2 * D_MODEL // hw
    n_q = SEQ if latent else CTX_LEN
    q_row0 = 0 if latent else T_X
    nq = n_q // tq
    in_specs = [
        pl.BlockSpec((4, DIFF_HEAD_DIM), lambda b, h, i: (0, 0)),
        pl.BlockSpec((1, hw), lambda b, h, i: (0, 0)),
        pl.BlockSpec((tq, hw), lambda b, h, i: (q_row0 // tq + b * nq + i, h)),
    ]
    args = [lam_vecs, subln_g.reshape(1, hw), qkv]
    if latent:
        in_specs += [
            pl.BlockSpec((SEQ, hw), lambda b, h, i: (b, k_col + h)),
            pl.BlockSpec((SEQ, hw), lambda b, h, i: (b, v_col + h)),
        ]
        args += [qkv, qkv]
    in_specs += [
        pl.BlockSpec((CTX_LEN, hw), lambda b, h, i: (T_X // CTX_LEN + b, k_col + h)),
        pl.BlockSpec((CTX_LEN, hw), lambda b, h, i: (T_X // CTX_LEN + b, v_col + h)),
    ]
    args += [qkv, qkv]
    return pl.pallas_call(
        functools.partial(_attn_kernel, latent=latent, lam_init=lam_init),
        grid=(BATCH, DIFF_HEADS, nq),
        in_specs=in_specs,
        out_specs=pl.BlockSpec((tq, hw), lambda b, h, i: (b * nq + i, h)),
        out_shape=jax.ShapeDtypeStruct((BATCH * n_q, D_MODEL), BF16),
        scratch_shapes=[pltpu.VMEM((2, tq, (SEQ if latent else 0) + CTX_LEN), F32)],
        compiler_params=_params(("arbitrary",) * 3),
        name="diffattn_latent" if latent else "diffattn_ctx",
    )(*args)


GLA_CTX_CHUNKS = CTX_LEN // GLA_CHUNK
GLA_X_CHUNKS = SEQ // GLA_CHUNK
GLA_STEPS = GLA_CTX_CHUNKS + GLA_X_CHUNKS


def _split3_bf16(a):
    hi = a.astype(BF16)
    r = a - hi.astype(F32)
    mid = r.astype(BF16)
    lo = (r - mid.astype(F32)).astype(BF16)
    return hi, mid, lo


def _gla_scan_kernel(qf_ref, kf_ref, vf_ref, gf_ref, qb_ref, kb_ref, vb_ref, gb_ref, wup_ref, bg_ref, tri_ref,
                     of_ref, ob_ref, st_ref):
    @pl.when(pl.program_id(1) == 0)
    def _():
        st_ref[...] = jnp.zeros_like(st_ref)

    dk, dv = GLA_DK_HEAD, GLA_DV_HEAD
    dirs = (0, 1)
    chains = [(d, h) for h in range(GLA_HEADS) for d in dirs]
    q_refs, k_refs, v_refs, g_refs, o_refs = (qf_ref, qb_ref), (kf_ref, kb_ref), (vf_ref, vb_ref), (gf_ref, gb_ref), (of_ref, ob_ref)
    tri = [tri_ref[d] for d in dirs]
    mask = [t.astype(F32) > 0.0 for t in tri]
    z = [jnp.dot(g_refs[d][...], wup_ref[d], preferred_element_type=F32) + bg_ref[d] for d in dirs]
    g = [(jnp.minimum(zd, 0.0) - jnp.log(1.0 + jnp.exp(-jnp.abs(zd)))) * (1.0 / GLA_TAU) for zd in z]
    parts = [_split3_bf16(gd) for gd in g]
    bcum = [sum(jnp.dot(tri[d], part, preferred_element_type=F32) for part in parts[d]) for d in dirs]
    blast = [jnp.sum(gd, axis=0, keepdims=True) for gd in g]
    q = [q_refs[d][...].astype(F32) * (dk ** -0.5) for d in dirs]
    k = [k_refs[d][...].astype(F32) for d in dirs]
    q_dec = [(q[d] * jnp.exp(bcum[d])).astype(BF16) for d in dirs]
    k_inv = [(k[d] * jnp.exp(-bcum[d])).astype(BF16) for d in dirs]
    k_tail = [(k[d] * jnp.exp(blast[d] - bcum[d])).astype(BF16) for d in dirs]
    decay = [jnp.exp(blast[d]) for d in dirs]
    ks = lambda h: slice(h * dk, (h + 1) * dk)
    vs = lambda h: slice(h * dv, (h + 1) * dv)
    st = {c: st_ref[c[0], c[1]] for c in chains}
    att = {(d, h): _dot_nt(q_dec[d][:, ks(h)], k_inv[d][:, ks(h)]) for d, h in chains}
    inter = {(d, h): _dot_nt(q_dec[d][:, ks(h)], st[d, h].astype(BF16)) for d, h in chains}
    att = {(d, h): jnp.where(mask[d], att[d, h], 0.0).astype(BF16) for d, h in chains}
    v = {(d, h): v_refs[d][:, vs(h)] for d, h in chains}
    upd = {c: lax.dot_general(v[c], k_tail[c[0]][:, ks(c[1])], (((0,), (0,)), ((), ())),
                              preferred_element_type=F32) for c in chains}
    out = {c: inter[c] + jnp.dot(att[c], v[c], preferred_element_type=F32) for c in chains}
    for d, h in chains:
        o_refs[d][:, vs(h)] = out[d, h].astype(o_refs[d].dtype)
        st_ref[d, h] = decay[d][:, ks(h)] * st[d, h] + upd[d, h]


def _gla_scan(proj, gates, wup_pad, b_gate, tri):
    def row_block(d):
        def index(b, s):
            c_idx = s if d == 0 else GLA_CTX_CHUNKS - 1 - s
            x_idx = s - GLA_CTX_CHUNKS if d == 0 else GLA_STEPS - 1 - s
            return jnp.where(s < GLA_CTX_CHUNKS,
                             T_X // GLA_CHUNK + b * GLA_CTX_CHUNKS + c_idx,
                             b * GLA_X_CHUNKS + x_idx)
        return index

    def chunk_specs(d):
        rb = row_block(d)
        return [
            pl.BlockSpec((GLA_CHUNK, GLA_DK), lambda b, s: (rb(b, s), 0)),
            pl.BlockSpec((GLA_CHUNK, GLA_DK), lambda b, s: (rb(b, s), 1)),
            pl.BlockSpec((GLA_CHUNK, GLA_DV), lambda b, s: (rb(b, s), 2 * GLA_DK // GLA_DV)),
            pl.BlockSpec((GLA_CHUNK, LANE), lambda b, s: (rb(b, s), 0)),
        ]

    whole = lambda shape: pl.BlockSpec(shape, lambda b, s: (0,) * len(shape))
    out_sds = jax.ShapeDtypeStruct((T_ALL, GLA_DV), BF16)
    return pl.pallas_call(
        _gla_scan_kernel,
        grid=(BATCH, GLA_STEPS),
        in_specs=chunk_specs(0) + chunk_specs(1) + [
            whole((2, LANE, GLA_DK)), whole((2, 1, GLA_DK)), whole((2, GLA_CHUNK, GLA_CHUNK))],
        out_specs=[pl.BlockSpec((GLA_CHUNK, GLA_DV), lambda b, s: (row_block(0)(b, s), 0)),
                   pl.BlockSpec((GLA_CHUNK, GLA_DV), lambda b, s: (row_block(1)(b, s), 0))],
        out_shape=[out_sds, out_sds],
        scratch_shapes=[pltpu.VMEM((2, GLA_HEADS, GLA_DV_HEAD, GLA_DK_HEAD), F32)],
        compiler_params=_params(("arbitrary",) * 2),
        name="gla_scan",
    )(proj, proj, proj, gates, proj, proj, proj, gates, wup_pad, b_gate, tri)


def _gla_finish_kernel(of_ref, ob_ref, r_ref, g_ref, y_ref):
    o = of_ref[...].astype(F32) + ob_ref[...].astype(F32)
    r = r_ref[...].astype(F32)
    gate = r * jax.nn.sigmoid(r)
    dv = GLA_DV_HEAD
    for h in range(GLA_HEADS):
        oh = o[:, h * dv:(h + 1) * dv]
        ms = jnp.mean(oh * oh, axis=-1, keepdims=True)
        y = oh * lax.rsqrt(ms + NORM_EPS) * g_ref[...]
        y_ref[:, h * dv:(h + 1) * dv] = (y * gate[:, h * dv:(h + 1) * dv]).astype(y_ref.dtype)


def _gla_finish(o_fwd, o_bwd, proj, norm_g, *, n_tiles):
    r_col = (2 * GLA_DK + GLA_DV) // GLA_DV
    return pl.pallas_call(
        _gla_finish_kernel,
        grid=(n_tiles,),
        in_specs=[
            pl.BlockSpec((TM, GLA_DV), lambda i: (i, 0)),
            pl.BlockSpec((TM, GLA_DV), lambda i: (i, 0)),
            pl.BlockSpec((TM, GLA_DV), lambda i: (i, r_col)),
            pl.BlockSpec((1, GLA_DV_HEAD), lambda i: (0, 0)),
        ],
        out_specs=pl.BlockSpec((TM, GLA_DV), lambda i: (i, 0)),
        out_shape=jax.ShapeDtypeStruct((n_tiles * TM, GLA_DV), BF16),
        compiler_params=_params(("arbitrary",)),
        name="gla_finish",
    )(o_fwd, o_bwd, proj, norm_g.reshape(1, -1))


def _rope_tables():
    hd = DIFF_HEAD_DIM
    pos = np.arange(SEQ)
    row = (pos // GRID_W).astype(np.float32)
    col = (pos % GRID_W).astype(np.float32)
    n_freq = hd // 4
    inv = (ROPE_THETA ** (-np.arange(n_freq, dtype=np.float32) / n_freq)).astype(np.float32)
    ang = np.concatenate([row[:, None] * inv, col[:, None] * inv], axis=-1).astype(np.float64)
    cos, sin = np.cos(ang), np.sin(ang)
    cs = np.concatenate([cos, cos], axis=-1).astype(np.float32)
    sn = np.concatenate([-sin, sin], axis=-1).astype(np.float32)
    return jnp.asarray(cs), jnp.asarray(sn)


def _gla_tri():
    lower = np.tril(np.ones((GLA_CHUNK, GLA_CHUNK), np.float32))
    return jnp.asarray(np.stack([lower, lower.T])).astype(BF16)


def _mixer(i, tok, mod, g_mix, w, n_tiles):
    kind = i % N_MIXERS
    if kind == 0:
        ab = _modmm(tok, g_mix, mod, 0, 1, _channel_dft_matrix())
        zx = _seqdft(ab, *_seq_dft_matrices(SEQ), n=SEQ, row0=0)
        zc = _seqdft(ab, *_seq_dft_matrices(CTX_LEN), n=CTX_LEN, row0=T_X)
        return jnp.concatenate([zx, zc], axis=0), w["fno_w_out"]
    if kind == 1:
        z = _modmm(tok, g_mix, mod, 0, 1, w["gmlp_w_in"].astype(BF16), epilogue="gelu")
        mixed = _sgu(z, w["gmlp_ln_g"], w["gmlp_ln_b"], w["gmlp_w_s"].astype(BF16), w["gmlp_b_s"].T)
        return mixed, w["gmlp_w_out"]
    if kind == 2:
        lam_init = 0.8 - 0.6 * math.exp(-0.3 * i)
        qkv = _modmm(tok, g_mix, mod, 0, 1, w["diff_w_in"].astype(BF16), epilogue="rope", rope=_rope_tables())
        ox = _diff_attention(qkv, w["diff_lambda"], w["diff_subln_g"], lam_init, latent=True)
        oc = _diff_attention(qkv, w["diff_lambda"], w["diff_subln_g"], lam_init, latent=False)
        return jnp.concatenate([ox, oc], axis=0), w["diff_w_out"]
    n_main = 2 * GLA_DK + 2 * GLA_DV
    w_in = w["gla_w_in"]
    proj = _modmm(tok, g_mix, mod, 0, 1, w_in[:, :n_main].astype(BF16))
    w_gate = jnp.pad(w_in[:, n_main:], ((0, 0), (0, LANE - 2 * GLA_GATE_RANK))).astype(BF16)
    gates = _modmm(tok, g_mix, mod, 0, 1, w_gate)
    wup = w["gla_w_gate_up"]
    wup_pad = jnp.zeros((2, LANE, GLA_DK), F32)
    wup_pad = wup_pad.at[0, :GLA_GATE_RANK].set(wup[0])
    wup_pad = wup_pad.at[1, GLA_GATE_RANK:2 * GLA_GATE_RANK].set(wup[1])
    o_fwd, o_bwd = _gla_scan(proj, gates, wup_pad.astype(BF16), w["gla_b_gate"].reshape(2, 1, GLA_DK), _gla_tri())
    return _gla_finish(o_fwd, o_bwd, proj, w["gla_norm_g"], n_tiles=n_tiles), w["gla_w_out"]


def _layer(i, tok, mod, w, ffn_w, g_final=None):
    n_tiles = NT_X if i == DEPTH - 1 else NT_ALL
    mixed, w_out = _mixer(i, tok, mod, w["g_norm_mix"], w, n_tiles)
    tok = _resmm(mixed, w_out.astype(BF16), tok, mod, 2, n_tiles=n_tiles)
    return _ffn(tok, w["g_norm_ffn"], mod, *ffn_w, i, n_tiles=n_tiles, g_final=g_final)


def kernel(x, c, ctx, c_ctx, w_ada, b_ada, g_norm_mix, g_norm_ffn, w_ffn_in, w_ffn_out, g_final, fno_w_out, gmlp_w_in, gmlp_ln_g, gmlp_ln_b, gmlp_w_s, gmlp_b_s, gmlp_w_out, diff_w_in, diff_lambda, diff_subln_g, diff_w_out, gla_w_in, gla_w_gate_up, gla_b_gate, gla_norm_g, gla_w_out):
    tok = jnp.concatenate([x.reshape(T_X, D_MODEL), ctx.reshape(T_C, D_MODEL)], axis=0)
    c_rows = jnp.concatenate([c, c_ctx[None, :], jnp.zeros((MOD_ROWS - BATCH - 1, D_MODEL), F32)], axis=0)
    mod_all = _ada_table(c_rows, w_ada, b_ada).reshape(DEPTH, MOD_ROWS * N_ADA, 1, D_MODEL)
    mixer_weights = (
        dict(fno_w_out=fno_w_out),
        dict(gmlp_w_in=gmlp_w_in, gmlp_ln_g=gmlp_ln_g, gmlp_ln_b=gmlp_ln_b, gmlp_w_s=gmlp_w_s,
             gmlp_b_s=gmlp_b_s, gmlp_w_out=gmlp_w_out),
        dict(diff_w_in=diff_w_in, diff_lambda=diff_lambda, diff_subln_g=diff_subln_g, diff_w_out=diff_w_out),
        dict(gla_w_in=gla_w_in, gla_w_gate_up=gla_w_gate_up, gla_b_gate=gla_b_gate, gla_norm_g=gla_norm_g,
             gla_w_out=gla_w_out),
    )
    ffn_w = _ffn_weights(w_ffn_in, w_ffn_out)
    for i in range(DEPTH):
        kind, j = i % N_MIXERS, i // N_MIXERS
        w = {name: arr[j] for name, arr in mixer_weights[kind].items()}
        w.update(g_norm_mix=g_norm_mix[i], g_norm_ffn=g_norm_ffn[i])
        tok = _layer(i, tok, mod_all[i], w, ffn_w, g_final=g_final if i == DEPTH - 1 else None)
    return tok.reshape(BATCH, SEQ, D_MODEL)
```

```python
import functools
import math

import numpy as np
import jax
import jax.numpy as jnp
from jax import lax
from jax.experimental import pallas as pl
from jax.experimental.pallas import tpu as pltpu

D_MODEL = 2048
BATCH = 2
SEQ = 4096
DEPTH = 4
GRID_W = 64
CTX_LEN = 256
N_MIXERS = 4
NORM_EPS = 1e-6
D_FF = 4 * D_MODEL
N_ADA = 6
FNO_GROUPS = 4
GMLP_CHUNK = 128
GMLP_HEADS = 16
GMLP_WIDTH = D_MODEL
DIFF_HEADS = 8
DIFF_HEAD_DIM = D_MODEL // DIFF_HEADS // 2
ROPE_THETA = 10000.0
GLA_HEADS = 4
GLA_DK = D_MODEL // 2
GLA_DV = D_MODEL
GLA_DK_HEAD = GLA_DK // GLA_HEADS
GLA_DV_HEAD = GLA_DV // GLA_HEADS
GLA_GATE_RANK = 16
GLA_TAU = 16.0
GLA_CHUNK = 64
LN_EPS = 1e-5

T_X = BATCH * SEQ
T_C = BATCH * CTX_LEN
T_ALL = T_X + T_C
TM = 512
NT_X = T_X // TM
NT_ALL = T_ALL // TM
TILES_PER_BATCH = SEQ // TM
MOD_ROWS = 16
CTX_GROUP = BATCH
LANE = 128
MM_TN = 2048
FFN_TF = 1024

BF16 = jnp.bfloat16
F32 = jnp.float32
VMEM_LIMIT = 56 * 1024 * 1024


def _params(semantics):
    return pltpu.CompilerParams(dimension_semantics=semantics, vmem_limit_bytes=VMEM_LIMIT)


def _group_of_tile(i):
    return jnp.minimum(i // TILES_PER_BATCH, CTX_GROUP)


def _mod_spec(slot):
    return pl.BlockSpec((1, 1, D_MODEL), lambda i, *_: (_group_of_tile(i) * N_ADA + slot, 0, 0))


def _modulated(x, g, shift, scale):
    ms = jnp.mean(x * x, axis=-1, keepdims=True)
    y = x * lax.rsqrt(ms + NORM_EPS) * g
    return y * (1.0 + scale) + shift


def _ada_kernel(c_ref, w_ref, b_ref, o_ref):
    c = c_ref[...]
    s = (c * jax.nn.sigmoid(c)).astype(BF16)
    w = w_ref[0].astype(BF16)
    o_ref[0] = jnp.dot(s, w, preferred_element_type=F32) + b_ref[0]


def _ada_table(c8, w_ada, b_ada):
    tn = 1024
    n_out = N_ADA * D_MODEL
    return pl.pallas_call(
        _ada_kernel,
        grid=(DEPTH, n_out // tn),
        in_specs=[
            pl.BlockSpec((MOD_ROWS, D_MODEL), lambda l, j: (0, 0)),
            pl.BlockSpec((1, D_MODEL, tn), lambda l, j: (l, 0, j)),
            pl.BlockSpec((1, 1, tn), lambda l, j: (l, 0, j)),
        ],
        out_specs=pl.BlockSpec((1, MOD_ROWS, tn), lambda l, j: (l, 0, j)),
        out_shape=jax.ShapeDtypeStruct((DEPTH, MOD_ROWS, n_out), F32),
        compiler_params=_params(("arbitrary", "arbitrary")),
        name="ada_table",
    )(c8, w_ada, b_ada.reshape(DEPTH, 1, n_out))


def _gelu_exact(a):
    return 0.5 * a * (1.0 + lax.erf(a * (1.0 / math.sqrt(2.0))))


def _modmm_kernel(x_ref, g_ref, sh_ref, sc_ref, w_ref, *rest, epilogue, tn):
    if epilogue == "rope":
        cs_ref, sn_ref, o_ref, h_ref = rest
    else:
        o_ref, h_ref = rest
    i = pl.program_id(0)
    j = pl.program_id(1)

    def finish(acc):
        if epilogue == "gelu":
            o_ref[...] = _gelu_exact(acc).astype(o_ref.dtype)
        elif epilogue == "rope":
            acc = acc * jnp.where(j < D_MODEL // tn, ATTN_Q_SCALE, 1.0)
            rotate = jnp.logical_and(j < (2 * D_MODEL) // tn, i < NT_X)

            @pl.when(rotate)
            def _():
                cs = cs_ref[...]
                sn = sn_ref[...]
                for blk in range(tn // LANE):
                    a = acc[:, blk * LANE:(blk + 1) * LANE]
                    swapped = pltpu.roll(a, LANE // 2, 1)
                    o_ref[:, blk * LANE:(blk + 1) * LANE] = (a * cs + swapped * sn).astype(o_ref.dtype)

            @pl.when(jnp.logical_not(rotate))
            def _():
                o_ref[...] = acc.astype(o_ref.dtype)
        else:
            o_ref[...] = acc.astype(o_ref.dtype)

    @pl.when(j == 0)
    def _():
        h = _modulated(x_ref[...], g_ref[...], sh_ref[0], sc_ref[0]).astype(BF16)
        h_ref[...] = h
        finish(jnp.dot(h, w_ref[...], preferred_element_type=F32))

    @pl.when(j > 0)
    def _():
        finish(jnp.dot(h_ref[...], w_ref[...], preferred_element_type=F32))


def _modmm(x, g, mod, slot_shift, slot_scale, w, *, epilogue="plain", rope=None, out_dtype=BF16):
    k, n_out = w.shape
    tn = min(MM_TN, n_out)
    in_specs = [
        pl.BlockSpec((TM, k), lambda i, j: (i, 0)),
        pl.BlockSpec((1, k), lambda i, j: (0, 0)),
        _mod_spec(slot_shift),
        _mod_spec(slot_scale),
        pl.BlockSpec((k, tn), lambda i, j: (0, j)),
    ]
    args = [x, g.reshape(1, k), mod, mod, w]
    if epilogue == "rope":
        pos = lambda i, j: (jnp.where(i < NT_X, i % TILES_PER_BATCH, 0), 0)
        in_specs += [pl.BlockSpec((TM, LANE), pos), pl.BlockSpec((TM, LANE), pos)]
        args += list(rope)
    return pl.pallas_call(
        functools.partial(_modmm_kernel, epilogue=epilogue, tn=tn),
        grid=(NT_ALL, n_out // tn),
        in_specs=in_specs,
        out_specs=pl.BlockSpec((TM, tn), lambda i, j: (i, j)),
        out_shape=jax.ShapeDtypeStruct((T_ALL, n_out), out_dtype),
        scratch_shapes=[pltpu.VMEM((TM, k), BF16)],
        compiler_params=_params(("arbitrary", "arbitrary")),
        name="modmm_" + epilogue,
    )(*args)


def _resmm_kernel(*refs, split):
    if split:
        ax_ref, ac_ref, w_ref, res_ref, gate_ref, o_ref = refs
        a = jnp.where(pl.program_id(0) < NT_X, ax_ref[...], ac_ref[...])
    else:
        a_ref, w_ref, res_ref, gate_ref, o_ref = refs
        a = a_ref[...]
    acc = jnp.dot(a, w_ref[...], preferred_element_type=F32)
    o_ref[...] = res_ref[...] + gate_ref[0] * acc


def _resmm(a, w, res, mod, slot_gate, *, n_tiles):
    k, n_out = w.shape
    split = isinstance(a, tuple)
    if split and n_tiles == NT_X:
        a, split = a[0], False
    if split:
        assert a[1].shape[0] == TM
        a_specs = [pl.BlockSpec((TM, k), lambda i: (jnp.minimum(i, NT_X - 1), 0)),
                   pl.BlockSpec((TM, k), lambda i: (0, 0))]
        a_args = list(a)
    else:
        a_specs = [pl.BlockSpec((TM, k), lambda i: (i, 0))]
        a_args = [a]
    return pl.pallas_call(
        functools.partial(_resmm_kernel, split=split),
        grid=(n_tiles,),
        in_specs=a_specs + [
            pl.BlockSpec((k, n_out), lambda i: (0, 0)),
            pl.BlockSpec((TM, n_out), lambda i: (i, 0)),
            _mod_spec(slot_gate),
        ],
        out_specs=pl.BlockSpec((TM, n_out), lambda i: (i, 0)),
        out_shape=jax.ShapeDtypeStruct((n_tiles * TM, n_out), F32),
        compiler_params=_params(("arbitrary",)),
        name="resmm",
    )(*a_args, w, res, mod)


def _ffn_kernel(x_ref, g_ref, sh_ref, sc_ref, gate_ref, w1_ref, w2_ref, *rest, final_norm):
    if final_norm:
        gf_ref, o_ref, h_ref, acc_ref = rest
    else:
        o_ref, h_ref, acc_ref = rest
    f = pl.program_id(1)

    def hidden_step(h):
        a = jnp.dot(h, w1_ref[0], preferred_element_type=F32)
        a = jnp.square(jnp.maximum(a, 0.0)).astype(BF16)
        return jnp.dot(a, w2_ref[0], preferred_element_type=F32)

    @pl.when(f == 0)
    def _():
        h = _modulated(x_ref[...], g_ref[...], sh_ref[0], sc_ref[0]).astype(BF16)
        h_ref[...] = h
        acc_ref[...] = hidden_step(h)

    @pl.when(f > 0)
    def _():
        acc_ref[...] += hidden_step(h_ref[...])

    @pl.when(f == pl.num_programs(1) - 1)
    def _():
        y = x_ref[...] + gate_ref[0] * acc_ref[...]
        if final_norm:
            ms = jnp.mean(y * y, axis=-1, keepdims=True)
            y = y * lax.rsqrt(ms + NORM_EPS) * gf_ref[...]
        o_ref[...] = y


def _ffn_weights(w_ffn_in, w_ffn_out):
    return w_ffn_in.astype(BF16), w_ffn_out.astype(BF16)


def _ffn(x, g, mod, w1, w2, layer, *, n_tiles, g_final=None):
    final_norm = g_final is not None
    tf = FFN_TF
    in_specs = [
        pl.BlockSpec((TM, D_MODEL), lambda i, f: (i, 0)),
        pl.BlockSpec((1, D_MODEL), lambda i, f: (0, 0)),
        _mod_spec(3),
        _mod_spec(4),
        _mod_spec(5),
        pl.BlockSpec((1, D_MODEL, tf), lambda i, f: (layer, 0, f)),
        pl.BlockSpec((1, tf, D_MODEL), lambda i, f: (layer, f, 0)),
    ]
    args = [x, g.reshape(1, D_MODEL), mod, mod, mod, w1, w2]
    if final_norm:
        in_specs.append(pl.BlockSpec((1, D_MODEL), lambda i, f: (0, 0)))
        args.append(g_final.reshape(1, D_MODEL))
    return pl.pallas_call(
        functools.partial(_ffn_kernel, final_norm=final_norm),
        grid=(n_tiles, D_FF // tf),
        in_specs=in_specs,
        out_specs=pl.BlockSpec((TM, D_MODEL), lambda i, f: (i, 0)),
        out_shape=jax.ShapeDtypeStruct((n_tiles * TM, D_MODEL), F32),
        scratch_shapes=[pltpu.VMEM((TM, D_MODEL), BF16), pltpu.VMEM((TM, D_MODEL), F32)],
        compiler_params=_params(("arbitrary", "arbitrary")),
        name="ffn_final" if final_norm else "ffn",
    )(*args)


def _dft_tables(n):
    idx = np.arange(n, dtype=np.int64)
    ang = 2.0 * np.pi * ((idx[:, None] * idx[None, :]) % n).astype(np.float64) / n
    return np.cos(ang) / math.sqrt(n), np.sin(ang) / math.sqrt(n)


def _channel_dft_matrix():
    c, s = _dft_tables(D_MODEL // FNO_GROUPS)
    return jnp.asarray(np.concatenate([c, s], axis=1), F32).astype(BF16)


def _chdft_kernel(x_ref, g_ref, sh_ref, sc_ref, w_ref, o_ref):
    h = _modulated(x_ref[...], g_ref[...], sh_ref[0], sc_ref[0]).astype(BF16)
    dg = D_MODEL // FNO_GROUPS
    w = w_ref[...]
    for grp in range(FNO_GROUPS):
        cols = slice(grp * dg, (grp + 1) * dg)
        r = jnp.dot(h[:, cols], w, preferred_element_type=F32)
        o_ref[:, cols] = r[:, :dg].astype(o_ref.dtype)
        o_ref[:, D_MODEL + grp * dg:D_MODEL + (grp + 1) * dg] = r[:, dg:].astype(o_ref.dtype)


def _chdft(x, g, mod, w):
    dg = D_MODEL // FNO_GROUPS
    return pl.pallas_call(
        _chdft_kernel,
        grid=(NT_ALL,),
        in_specs=[
            pl.BlockSpec((TM, D_MODEL), lambda i: (i, 0)),
            pl.BlockSpec((1, D_MODEL), lambda i: (0, 0)),
            _mod_spec(0),
            _mod_spec(1),
            pl.BlockSpec((dg, 2 * dg), lambda i: (0, 0)),
        ],
        out_specs=pl.BlockSpec((TM, 2 * D_MODEL), lambda i: (i, 0)),
        out_shape=jax.ShapeDtypeStruct((T_ALL, 2 * D_MODEL), BF16),
        compiler_params=_params(("arbitrary",)),
        name="chdft",
    )(x, g.reshape(1, D_MODEL), mod, mod, w)


def _seq_dft_matrices(n):
    if n <= 512:
        c, s = _dft_tables(n)
        return jnp.asarray(c, F32).astype(BF16), jnp.asarray(-s, F32).astype(BF16)
    r = 64
    q = n // r
    cols = np.arange(n, dtype=np.int64)
    ang_hi = 2.0 * np.pi * (((r * np.arange(q))[:, None] * cols[None, :]) % n) / n
    ang_lo = 2.0 * np.pi * ((np.arange(r)[:, None] * cols[None, :]) % n) / n
    ch = jnp.asarray(np.cos(ang_hi) / math.sqrt(n), F32)[:, None, :]
    sh = jnp.asarray(np.sin(ang_hi) / math.sqrt(n), F32)[:, None, :]
    cl = jnp.asarray(np.cos(ang_lo), F32)[None, :, :]
    sl = jnp.asarray(np.sin(ang_lo), F32)[None, :, :]
    c = (ch * cl - sh * sl).astype(BF16).reshape(n, n)
    s_neg = (-(sh * cl + ch * sl)).astype(BF16).reshape(n, n)
    return c, s_neg


def _seqdft_kernel(c_ref, s_ref, a_ref, b_ref, o_ref, acc_ref):
    kk = pl.program_id(3)

    @pl.when(kk == 0)
    def _():
        acc_ref[...] = jnp.zeros_like(acc_ref)

    acc_ref[...] += (jnp.dot(c_ref[...], a_ref[...], preferred_element_type=F32)
                     + jnp.dot(s_ref[...], b_ref[...], preferred_element_type=F32))

    @pl.when(kk == pl.num_programs(3) - 1)
    def _():
        o_ref[...] = acc_ref[...].astype(o_ref.dtype)


def _seqdft(ab, cmat, smat, *, n, row0):
    tmm = min(n, 2048)
    tk = min(n, 512)
    tn = 1024
    nk = n // tk
    col_b = D_MODEL // tn
    return pl.pallas_call(
        _seqdft_kernel,
        grid=(BATCH, n // tmm, D_MODEL // tn, nk),
        in_specs=[
            pl.BlockSpec((tmm, tk), lambda b, m, j, k: (m, k)),
            pl.BlockSpec((tmm, tk), lambda b, m, j, k: (m, k)),
            pl.BlockSpec((tk, tn), lambda b, m, j, k: (row0 // tk + b * nk + k, j)),
            pl.BlockSpec((tk, tn), lambda b, m, j, k: (row0 // tk + b * nk + k, col_b + j)),
        ],
        out_specs=pl.BlockSpec((tmm, tn), lambda b, m, j, k: (b * (n // tmm) + m, j)),
        out_shape=jax.ShapeDtypeStruct((BATCH * n, D_MODEL), BF16),
        scratch_shapes=[pltpu.VMEM((tmm, tn), F32)],
        compiler_params=_params(("arbitrary",) * 4),
        name="seqdft",
    )(cmat, smat, ab, ab)


SGU_ROWS = 256


def _sgu_kernel(u_ref, v_ref, lg_ref, lb_ref, ws_ref, bs_ref, o_ref):
    v = v_ref[...].astype(F32)
    mu = jnp.mean(v, axis=-1, keepdims=True)
    vc = v - mu
    var = jnp.mean(vc * vc, axis=-1, keepdims=True)
    vn = (vc * lax.rsqrt(var + LN_EPS) * lg_ref[...] + lb_ref[...]).astype(BF16)
    we = GMLP_WIDTH // GMLP_HEADS
    for c in range(SGU_ROWS // GMLP_CHUNK):
        r0 = c * GMLP_CHUNK
        for h in range(GMLP_HEADS):
            s = jnp.dot(ws_ref[h], vn[r0:r0 + GMLP_CHUNK, h * we:(h + 1) * we], preferred_element_type=F32)
            s = s + bs_ref[:, h:h + 1]
            u = u_ref[r0:r0 + GMLP_CHUNK, h * we:(h + 1) * we].astype(F32)
            o_ref[r0:r0 + GMLP_CHUNK, h * we:(h + 1) * we] = (u * s).astype(o_ref.dtype)


def _sgu(z, ln_g, ln_b, w_s, b_s_t):
    return pl.pallas_call(
        _sgu_kernel,
        grid=(T_ALL // SGU_ROWS,),
        in_specs=[
            pl.BlockSpec((SGU_ROWS, GMLP_WIDTH), lambda i: (i, 0)),
            pl.BlockSpec((SGU_ROWS, GMLP_WIDTH), lambda i: (i, 1)),
            pl.BlockSpec((1, GMLP_WIDTH), lambda i: (0, 0)),
            pl.BlockSpec((1, GMLP_WIDTH), lambda i: (0, 0)),
            pl.BlockSpec((GMLP_HEADS, GMLP_CHUNK, GMLP_CHUNK), lambda i: (0, 0, 0)),
            pl.BlockSpec((GMLP_CHUNK, GMLP_HEADS), lambda i: (0, 0)),
        ],
        out_specs=pl.BlockSpec((SGU_ROWS, GMLP_WIDTH), lambda i: (i, 0)),
        out_shape=jax.ShapeDtypeStruct((T_ALL, GMLP_WIDTH), BF16),
        compiler_params=_params(("arbitrary",)),
        name="sgu",
    )(z, z, ln_g.reshape(1, -1), ln_b.reshape(1, -1), w_s, b_s_t)


def _dot_nt(a, b):
    return lax.dot_general(a, b, (((1,), (1,)), ((), ())), preferred_element_type=F32)


ATTN_KEY_CHUNK = 256
ATTN_Q_SCALE = (DIFF_HEAD_DIM ** -0.5) * math.log2(math.e)


def _attn_kernel(lam_ref, g_ref, q_ref, *rest, latent, lam_init):
    if latent:
        kx_ref, vx_ref, kc_ref, vc_ref, o_ref, s_ref = rest
        chunks = [(kx_ref, vx_ref, r) for r in range(0, SEQ, ATTN_KEY_CHUNK)]
    else:
        kc_ref, vc_ref, o_ref, s_ref = rest
        chunks = []
    chunks += [(kc_ref, vc_ref, r) for r in range(0, CTX_LEN, ATTN_KEY_CHUNK)]
    hd = DIFF_HEAD_DIM
    kc = ATTN_KEY_CHUNK
    lv = lam_ref[...]
    lam = (jnp.exp(jnp.sum(lv[0:1] * lv[1:2], axis=-1, keepdims=True))
           - jnp.exp(jnp.sum(lv[2:3] * lv[3:4], axis=-1, keepdims=True)) + lam_init)
    q = q_ref[...]
    comps = (0, 1)
    qs = [q[:, c * hd:(c + 1) * hd] for c in comps]
    m = [None, None]
    for j, (k_ref, _, r0) in enumerate(chunks):
        for c in comps:
            s = _dot_nt(qs[c], k_ref[r0:r0 + kc, c * hd:(c + 1) * hd])
            s_ref[c, :, j * kc:(j + 1) * kc] = s
            sm = jnp.maximum(s[:, :LANE], s[:, LANE:])
            m[c] = sm if m[c] is None else jnp.maximum(m[c], sm)
    m = [jnp.max(mc, axis=-1, keepdims=True) for mc in m]
    l = [None, None]
    acc = [None, None]

    def exp_chunk(j, c, row_max):
        e = jnp.exp2(s_ref[c, :, j * kc:(j + 1) * kc] - row_max)
        ep = e[:, :LANE] + e[:, LANE:]
        l[c] = ep if l[c] is None else l[c] + ep
        return e.astype(BF16)

    ahead = 4
    n_chunks = len(chunks)
    ready = {(j, c): exp_chunk(j, c, m[c]) for j in range(min(ahead, n_chunks)) for c in comps}
    for j, (_, v_ref, r0) in enumerate(chunks):
        for c in comps:
            pv = jnp.dot(ready.pop((j, c)), v_ref[r0:r0 + kc, :], preferred_element_type=F32)
            acc[c] = pv if acc[c] is None else acc[c] + pv
            if j + ahead < n_chunks:
                zero = jnp.minimum(jnp.abs(pv[-1:, :1]), 0.0)
                ready[j + ahead, c] = exp_chunk(j + ahead, c, m[c] + zero)
    outs = [acc[c] * (1.0 / jnp.sum(l[c], axis=-1, keepdims=True)) for c in comps]
    o = outs[0] - lam * outs[1]
    ms = jnp.mean(o * o, axis=-1, keepdims=True)
    o = o * lax.rsqrt(ms + NORM_EPS) * g_ref[...] * (1.0 - lam_init)
    o_ref[...] = o.astype(o_ref.dtype)


def _diff_attention(qkv, lam_vecs, subln_g, lam_init, *, latent, tq=256):
    hw = 2 * DIFF_HEAD_DIM
    k_col = D_MODEL // hw
    v_col = 2 * D_MODEL // hw
    n_q = SEQ if latent else CTX_LEN
    q_row0 = 0 if latent else T_X
    nq = n_q // tq
    in_specs = [
        pl.BlockSpec((4, DIFF_HEAD_DIM), lambda b, h, i: (0, 0)),
        pl.BlockSpec((1, hw), lambda b, h, i: (0, 0)),
        pl.BlockSpec((tq, hw), lambda b, h, i: (q_row0 // tq + b * nq + i, h)),
    ]
    args = [lam_vecs, subln_g.reshape(1, hw), qkv]
    if latent:
        in_specs += [
            pl.BlockSpec((SEQ, hw), lambda b, h, i: (b, k_col + h)),
            pl.BlockSpec((SEQ, hw), lambda b, h, i: (b, v_col + h)),
        ]
        args += [qkv, qkv]
    in_specs += [
        pl.BlockSpec((CTX_LEN, hw), lambda b, h, i: (T_X // CTX_LEN + b, k_col + h)),
        pl.BlockSpec((CTX_LEN, hw), lambda b, h, i: (T_X // CTX_LEN + b, v_col + h)),
    ]
    args += [qkv, qkv]
    return pl.pallas_call(
        functools.partial(_attn_kernel, latent=latent, lam_init=lam_init),
        grid=(BATCH, DIFF_HEADS, nq),
        in_specs=in_specs,
        out_specs=pl.BlockSpec((tq, hw), lambda b, h, i: (b * nq + i, h)),
        out_shape=jax.ShapeDtypeStruct((BATCH * n_q, D_MODEL), BF16),
        scratch_shapes=[pltpu.VMEM((2, tq, (SEQ if latent else 0) + CTX_LEN), F32)],
        compiler_params=_params(("arbitrary",) * 3),
        name="diffattn_latent" if latent else "diffattn_ctx",
    )(*args)


GLA_CTX_CHUNKS = CTX_LEN // GLA_CHUNK
GLA_X_CHUNKS = SEQ // GLA_CHUNK
GLA_STEPS = GLA_CTX_CHUNKS + GLA_X_CHUNKS


def _split3_bf16(a):
    hi = a.astype(BF16)
    r = a - hi.astype(F32)
    mid = r.astype(BF16)
    lo = (r - mid.astype(F32)).astype(BF16)
    return hi, mid, lo


def _gla_scan_kernel(qf_ref, kf_ref, vf_ref, gf_ref, qb_ref, kb_ref, vb_ref, gb_ref, wup_ref, bg_ref, tri_ref,
                     of_ref, ob_ref, st_ref):
    @pl.when(pl.program_id(1) == 0)
    def _():
        st_ref[...] = jnp.zeros_like(st_ref)

    dk, dv = GLA_DK_HEAD, GLA_DV_HEAD
    dirs = (0, 1)
    chains = [(d, h) for h in range(GLA_HEADS) for d in dirs]
    q_refs, k_refs, v_refs, g_refs, o_refs = (qf_ref, qb_ref), (kf_ref, kb_ref), (vf_ref, vb_ref), (gf_ref, gb_ref), (of_ref, ob_ref)
    tri = [tri_ref[d] for d in dirs]
    mask = [t.astype(F32) > 0.0 for t in tri]
    z = [jnp.dot(g_refs[d][...], wup_ref[d], preferred_element_type=F32) + bg_ref[d] for d in dirs]
    g = [(jnp.minimum(zd, 0.0) - jnp.log(1.0 + jnp.exp(-jnp.abs(zd)))) * (1.0 / GLA_TAU) for zd in z]
    parts = [_split3_bf16(gd) for gd in g]
    bcum = [sum(jnp.dot(tri[d], part, preferred_element_type=F32) for part in parts[d]) for d in dirs]
    blast = [jnp.sum(gd, axis=0, keepdims=True) for gd in g]
    q = [q_refs[d][...].astype(F32) * (dk ** -0.5) for d in dirs]
    k = [k_refs[d][...].astype(F32) for d in dirs]
    q_dec = [(q[d] * jnp.exp(bcum[d])).astype(BF16) for d in dirs]
    k_inv = [(k[d] * jnp.exp(-bcum[d])).astype(BF16) for d in dirs]
    k_tail = [(k[d] * jnp.exp(blast[d] - bcum[d])).astype(BF16) for d in dirs]
    decay = [jnp.exp(blast[d]) for d in dirs]
    ks = lambda h: slice(h * dk, (h + 1) * dk)
    vs = lambda h: slice(h * dv, (h + 1) * dv)
    st = {c: st_ref[c[0], c[1]] for c in chains}
    att = {(d, h): _dot_nt(q_dec[d][:, ks(h)], k_inv[d][:, ks(h)]) for d, h in chains}
    inter = {(d, h): _dot_nt(q_dec[d][:, ks(h)], st[d, h].astype(BF16)) for d, h in chains}
    att = {(d, h): jnp.where(mask[d], att[d, h], 0.0).astype(BF16) for d, h in chains}
    v = {(d, h): v_refs[d][:, vs(h)] for d, h in chains}
    upd = {c: lax.dot_general(v[c], k_tail[c[0]][:, ks(c[1])], (((0,), (0,)), ((), ())),
                              preferred_element_type=F32) for c in chains}
    out = {c: inter[c] + jnp.dot(att[c], v[c], preferred_element_type=F32) for c in chains}
    for d, h in chains:
        o_refs[d][:, vs(h)] = out[d, h].astype(o_refs[d].dtype)
        st_ref[d, h] = decay[d][:, ks(h)] * st[d, h] + upd[d, h]


def _gla_scan(proj, gates, wup_pad, b_gate, tri):
    def row_block(d):
        def index(b, s):
            c_idx = s if d == 0 else GLA_CTX_CHUNKS - 1 - s
            x_idx = s - GLA_CTX_CHUNKS if d == 0 else GLA_STEPS - 1 - s
            return jnp.where(s < GLA_CTX_CHUNKS,
                             T_X // GLA_CHUNK + b * GLA_CTX_CHUNKS + c_idx,
                             b * GLA_X_CHUNKS + x_idx)
        return index

    def chunk_specs(d):
        rb = row_block(d)
        return [
            pl.BlockSpec((GLA_CHUNK, GLA_DK), lambda b, s: (rb(b, s), 0)),
            pl.BlockSpec((GLA_CHUNK, GLA_DK), lambda b, s: (rb(b, s), 1)),
            pl.BlockSpec((GLA_CHUNK, GLA_DV), lambda b, s: (rb(b, s), 2 * GLA_DK // GLA_DV)),
            pl.BlockSpec((GLA_CHUNK, LANE), lambda b, s: (rb(b, s), 0)),
        ]

    whole = lambda shape: pl.BlockSpec(shape, lambda b, s: (0,) * len(shape))
    out_sds = jax.ShapeDtypeStruct((T_ALL, GLA_DV), BF16)
    return pl.pallas_call(
        _gla_scan_kernel,
        grid=(BATCH, GLA_STEPS),
        in_specs=chunk_specs(0) + chunk_specs(1) + [
            whole((2, LANE, GLA_DK)), whole((2, 1, GLA_DK)), whole((2, GLA_CHUNK, GLA_CHUNK))],
        out_specs=[pl.BlockSpec((GLA_CHUNK, GLA_DV), lambda b, s: (row_block(0)(b, s), 0)),
                   pl.BlockSpec((GLA_CHUNK, GLA_DV), lambda b, s: (row_block(1)(b, s), 0))],
        out_shape=[out_sds, out_sds],
        scratch_shapes=[pltpu.VMEM((2, GLA_HEADS, GLA_DV_HEAD, GLA_DK_HEAD), F32)],
        compiler_params=_params(("arbitrary",) * 2),
        name="gla_scan",
    )(proj, proj, proj, gates, proj, proj, proj, gates, wup_pad, b_gate, tri)


def _gla_finish_kernel(of_ref, ob_ref, r_ref, g_ref, y_ref):
    o = of_ref[...].astype(F32) + ob_ref[...].astype(F32)
    r = r_ref[...].astype(F32)
    gate = r * jax.nn.sigmoid(r)
    dv = GLA_DV_HEAD
    for h in range(GLA_HEADS):
        oh = o[:, h * dv:(h + 1) * dv]
        ms = jnp.mean(oh * oh, axis=-1, keepdims=True)
        y = oh * lax.rsqrt(ms + NORM_EPS) * g_ref[...]
        y_ref[:, h * dv:(h + 1) * dv] = (y * gate[:, h * dv:(h + 1) * dv]).astype(y_ref.dtype)


def _gla_finish(o_fwd, o_bwd, proj, norm_g, *, n_tiles):
    r_col = (2 * GLA_DK + GLA_DV) // GLA_DV
    return pl.pallas_call(
        _gla_finish_kernel,
        grid=(n_tiles,),
        in_specs=[
            pl.BlockSpec((TM, GLA_DV), lambda i: (i, 0)),
            pl.BlockSpec((TM, GLA_DV), lambda i: (i, 0)),
            pl.BlockSpec((TM, GLA_DV), lambda i: (i, r_col)),
            pl.BlockSpec((1, GLA_DV_HEAD), lambda i: (0, 0)),
        ],
        out_specs=pl.BlockSpec((TM, GLA_DV), lambda i: (i, 0)),
        out_shape=jax.ShapeDtypeStruct((n_tiles * TM, GLA_DV), BF16),
        compiler_params=_params(("arbitrary",)),
        name="gla_finish",
    )(o_fwd, o_bwd, proj, norm_g.reshape(1, -1))


def _rope_tables():
    hd = DIFF_HEAD_DIM
    pos = np.arange(SEQ)
    row = (pos // GRID_W).astype(np.float32)
    col = (pos % GRID_W).astype(np.float32)
    n_freq = hd // 4
    inv = (ROPE_THETA ** (-np.arange(n_freq, dtype=np.float32) / n_freq)).astype(np.float32)
    ang = np.concatenate([row[:, None] * inv, col[:, None] * inv], axis=-1).astype(np.float64)
    cos, sin = np.cos(ang), np.sin(ang)
    cs = np.concatenate([cos, cos], axis=-1).astype(np.float32)
    sn = np.concatenate([-sin, sin], axis=-1).astype(np.float32)
    return jnp.asarray(cs), jnp.asarray(sn)


def _gla_tri():
    lower = np.tril(np.ones((GLA_CHUNK, GLA_CHUNK), np.float32))
    return jnp.asarray(np.stack([lower, lower.T])).astype(BF16)


def _mixer(i, tok, mod, g_mix, w, n_tiles):
    kind = i % N_MIXERS
    if kind == 0:
        ab = _chdft(tok, g_mix, mod, _channel_dft_matrix())
        zx = _seqdft(ab, *_seq_dft_matrices(SEQ), n=SEQ, row0=0)
        zc = _seqdft(ab, *_seq_dft_matrices(CTX_LEN), n=CTX_LEN, row0=T_X)
        return (zx, zc), w["fno_w_out"]
    if kind == 1:
        z = _modmm(tok, g_mix, mod, 0, 1, w["gmlp_w_in"].astype(BF16), epilogue="gelu")
        mixed = _sgu(z, w["gmlp_ln_g"], w["gmlp_ln_b"], w["gmlp_w_s"].astype(BF16), w["gmlp_b_s"].T)
        return mixed, w["gmlp_w_out"]
    if kind == 2:
        lam_init = 0.8 - 0.6 * math.exp(-0.3 * i)
        qkv = _modmm(tok, g_mix, mod, 0, 1, w["diff_w_in"].astype(BF16), epilogue="rope", rope=_rope_tables())
        ox = _diff_attention(qkv, w["diff_lambda"], w["diff_subln_g"], lam_init, latent=True)
        oc = _diff_attention(qkv, w["diff_lambda"], w["diff_subln_g"], lam_init, latent=False)
        return (ox, oc), w["diff_w_out"]
    n_main = 2 * GLA_DK + 2 * GLA_DV
    w_in = w["gla_w_in"]
    proj = _modmm(tok, g_mix, mod, 0, 1, w_in[:, :n_main].astype(BF16))
    w_gate = jnp.pad(w_in[:, n_main:], ((0, 0), (0, LANE - 2 * GLA_GATE_RANK))).astype(BF16)
    gates = _modmm(tok, g_mix, mod, 0, 1, w_gate)
    wup = w["gla_w_gate_up"]
    wup_pad = jnp.zeros((2, LANE, GLA_DK), F32)
    wup_pad = wup_pad.at[0, :GLA_GATE_RANK].set(wup[0])
    wup_pad = wup_pad.at[1, GLA_GATE_RANK:2 * GLA_GATE_RANK].set(wup[1])
    o_fwd, o_bwd = _gla_scan(proj, gates, wup_pad.astype(BF16), w["gla_b_gate"].reshape(2, 1, GLA_DK), _gla_tri())
    return _gla_finish(o_fwd, o_bwd, proj, w["gla_norm_g"], n_tiles=n_tiles), w["gla_w_out"]


def _layer(i, tok, mod, w, ffn_w, g_final=None):
    n_tiles = NT_X if i == DEPTH - 1 else NT_ALL
    mixed, w_out = _mixer(i, tok, mod, w["g_norm_mix"], w, n_tiles)
    tok = _resmm(mixed, w_out.astype(BF16), tok, mod, 2, n_tiles=n_tiles)
    return _ffn(tok, w["g_norm_ffn"], mod, *ffn_w, i, n_tiles=n_tiles, g_final=g_final)


def kernel(x, c, ctx, c_ctx, w_ada, b_ada, g_norm_mix, g_norm_ffn, w_ffn_in, w_ffn_out, g_final, fno_w_out, gmlp_w_in, gmlp_ln_g, gmlp_ln_b, gmlp_w_s, gmlp_b_s, gmlp_w_out, diff_w_in, diff_lambda, diff_subln_g, diff_w_out, gla_w_in, gla_w_gate_up, gla_b_gate, gla_norm_g, gla_w_out):
    tok = jnp.concatenate([x.reshape(T_X, D_MODEL), ctx.reshape(T_C, D_MODEL)], axis=0)
    c_rows = jnp.concatenate([c, c_ctx[None, :], jnp.zeros((MOD_ROWS - BATCH - 1, D_MODEL), F32)], axis=0)
    mod_all = _ada_table(c_rows, w_ada, b_ada).reshape(DEPTH, MOD_ROWS * N_ADA, 1, D_MODEL)
    mixer_weights = (
        dict(fno_w_out=fno_w_out),
        dict(gmlp_w_in=gmlp_w_in, gmlp_ln_g=gmlp_ln_g, gmlp_ln_b=gmlp_ln_b, gmlp_w_s=gmlp_w_s,
             gmlp_b_s=gmlp_b_s, gmlp_w_out=gmlp_w_out),
        dict(diff_w_in=diff_w_in, diff_lambda=diff_lambda, diff_subln_g=diff_subln_g, diff_w_out=diff_w_out),
        dict(gla_w_in=gla_w_in, gla_w_gate_up=gla_w_gate_up, gla_b_gate=gla_b_gate, gla_norm_g=gla_norm_g,
             gla_w_out=gla_w_out),
    )
    ffn_w = _ffn_weights(w_ffn_in, w_ffn_out)
    for i in range(DEPTH):
        kind, j = i % N_MIXERS, i // N_MIXERS
        w = {name: arr[j] for name, arr in mixer_weights[kind].items()}
        w.update(g_norm_mix=g_norm_mix[i], g_norm_ffn=g_norm_ffn[i])
        tok = _layer(i, tok, mod_all[i], w, ffn_w, g_final=g_final if i == DEPTH - 1 else None)
    return tok.reshape(BATCH, SEQ, D_MODEL)
```

```python
import functools
import math

import numpy as np
import jax
import jax.numpy as jnp
from jax import lax
from jax.experimental import pallas as pl
from jax.experimental.pallas import tpu as pltpu

D_MODEL = 2048
BATCH = 2
SEQ = 4096
DEPTH = 4
GRID_W = 64
CTX_LEN = 256
N_MIXERS = 4
NORM_EPS = 1e-6
D_FF = 4 * D_MODEL
N_ADA = 6
FNO_GROUPS = 4
GMLP_CHUNK = 128
GMLP_HEADS = 16
GMLP_WIDTH = D_MODEL
DIFF_HEADS = 8
DIFF_HEAD_DIM = D_MODEL // DIFF_HEADS // 2
ROPE_THETA = 10000.0
GLA_HEADS = 4
GLA_DK = D_MODEL // 2
GLA_DV = D_MODEL
GLA_DK_HEAD = GLA_DK // GLA_HEADS
GLA_DV_HEAD = GLA_DV // GLA_HEADS
GLA_GATE_RANK = 16
GLA_TAU = 16.0
GLA_CHUNK = 64
LN_EPS = 1e-5

T_X = BATCH * SEQ
T_C = BATCH * CTX_LEN
T_ALL = T_X + T_C
TM = 512
NT_X = T_X // TM
NT_ALL = T_ALL // TM
TILES_PER_BATCH = SEQ // TM
MOD_ROWS = 16
CTX_GROUP = BATCH
LANE = 128
MM_TN = 2048
FFN_TF = 1024

BF16 = jnp.bfloat16
F32 = jnp.float32
VMEM_LIMIT = 56 * 1024 * 1024


def _params(semantics):
    return pltpu.CompilerParams(dimension_semantics=semantics, vmem_limit_bytes=VMEM_LIMIT)


def _group_of_tile(i):
    return jnp.minimum(i // TILES_PER_BATCH, CTX_GROUP)


def _mod_spec(slot):
    return pl.BlockSpec((1, 1, D_MODEL), lambda i, *_: (_group_of_tile(i) * N_ADA + slot, 0, 0))


def _modulated(x, g, shift, scale):
    ms = jnp.mean(x * x, axis=-1, keepdims=True)
    y = x * lax.rsqrt(ms + NORM_EPS) * g
    return y * (1.0 + scale) + shift


def _ada_kernel(c_ref, w_ref, b_ref, o_ref):
    c = c_ref[...]
    s = (c * jax.nn.sigmoid(c)).astype(BF16)
    w = w_ref[0].astype(BF16)
    o_ref[0] = jnp.dot(s, w, preferred_element_type=F32) + b_ref[0]


def _ada_table(c8, w_ada, b_ada):
    tn = 1024
    n_out = N_ADA * D_MODEL
    return pl.pallas_call(
        _ada_kernel,
        grid=(DEPTH, n_out // tn),
        in_specs=[
            pl.BlockSpec((MOD_ROWS, D_MODEL), lambda l, j: (0, 0)),
            pl.BlockSpec((1, D_MODEL, tn), lambda l, j: (l, 0, j)),
            pl.BlockSpec((1, 1, tn), lambda l, j: (l, 0, j)),
        ],
        out_specs=pl.BlockSpec((1, MOD_ROWS, tn), lambda l, j: (l, 0, j)),
        out_shape=jax.ShapeDtypeStruct((DEPTH, MOD_ROWS, n_out), F32),
        compiler_params=_params(("arbitrary", "arbitrary")),
        name="ada_table",
    )(c8, w_ada, b_ada.reshape(DEPTH, 1, n_out))


def _gelu_exact(a):
    return 0.5 * a * (1.0 + lax.erf(a * (1.0 / math.sqrt(2.0))))


def _modmm_kernel(x_ref, g_ref, sh_ref, sc_ref, w_ref, *rest, epilogue, tn):
    if epilogue == "rope":
        cs_ref, sn_ref, o_ref, h_ref = rest
    else:
        o_ref, h_ref = rest
    i = pl.program_id(0)
    j = pl.program_id(1)

    def finish(acc):
        if epilogue == "gelu":
            o_ref[...] = _gelu_exact(acc).astype(o_ref.dtype)
        elif epilogue == "rope":
            acc = acc * jnp.where(j < D_MODEL // tn, ATTN_Q_SCALE, 1.0)
            rotate = jnp.logical_and(j < (2 * D_MODEL) // tn, i < NT_X)

            @pl.when(rotate)
            def _():
                cs = cs_ref[...]
                sn = sn_ref[...]
                for blk in range(tn // LANE):
                    a = acc[:, blk * LANE:(blk + 1) * LANE]
                    swapped = pltpu.roll(a, LANE // 2, 1)
                    o_ref[:, blk * LANE:(blk + 1) * LANE] = (a * cs + swapped * sn).astype(o_ref.dtype)

            @pl.when(jnp.logical_not(rotate))
            def _():
                o_ref[...] = acc.astype(o_ref.dtype)
        else:
            o_ref[...] = acc.astype(o_ref.dtype)

    @pl.when(j == 0)
    def _():
        h = _modulated(x_ref[...], g_ref[...], sh_ref[0], sc_ref[0]).astype(BF16)
        h_ref[...] = h
        finish(jnp.dot(h, w_ref[...], preferred_element_type=F32))

    @pl.when(j > 0)
    def _():
        finish(jnp.dot(h_ref[...], w_ref[...], preferred_element_type=F32))


def _modmm(x, g, mod, slot_shift, slot_scale, w, *, epilogue="plain", rope=None, out_dtype=BF16):
    k, n_out = w.shape
    tn = min(MM_TN, n_out)
    in_specs = [
        pl.BlockSpec((TM, k), lambda i, j: (i, 0)),
        pl.BlockSpec((1, k), lambda i, j: (0, 0)),
        _mod_spec(slot_shift),
        _mod_spec(slot_scale),
        pl.BlockSpec((k, tn), lambda i, j: (0, j)),
    ]
    args = [x, g.reshape(1, k), mod, mod, w]
    if epilogue == "rope":
        pos = lambda i, j: (jnp.where(i < NT_X, i % TILES_PER_BATCH, 0), 0)
        in_specs += [pl.BlockSpec((TM, LANE), pos), pl.BlockSpec((TM, LANE), pos)]
        args += list(rope)
    return pl.pallas_call(
        functools.partial(_modmm_kernel, epilogue=epilogue, tn=tn),
        grid=(NT_ALL, n_out // tn),
        in_specs=in_specs,
        out_specs=pl.BlockSpec((TM, tn), lambda i, j: (i, j)),
        out_shape=jax.ShapeDtypeStruct((T_ALL, n_out), out_dtype),
        scratch_shapes=[pltpu.VMEM((TM, k), BF16)],
        compiler_params=_params(("arbitrary", "arbitrary")),
        name="modmm_" + epilogue,
    )(*args)


def _stream_operand(arr, rows, width, n_tiles):
    if isinstance(arr, tuple) and n_tiles == NT_X:
        arr = arr[0]
    if isinstance(arr, tuple):
        assert arr[1].shape == (rows, width)
        return ([pl.BlockSpec((rows, width), lambda i, *_: (jnp.minimum(i, NT_X - 1), 0)),
                 pl.BlockSpec((rows, width), lambda i, *_: (0, 0))], list(arr))
    return [pl.BlockSpec((rows, width), lambda i, *_: (i, 0))], [arr]


def _stream_tile(refs):
    if len(refs) == 2:
        return jnp.where(pl.program_id(0) < NT_X, refs[0][...], refs[1][...])
    return refs[0][...]


def _gla_gated(of_ref, ob_ref, r_ref, g_ref):
    o = of_ref[...].astype(F32) + ob_ref[...].astype(F32)
    r = r_ref[...].astype(F32)
    gate = r * jax.nn.sigmoid(r)
    dv = GLA_DV_HEAD
    heads = []
    for h in range(GLA_HEADS):
        oh = o[:, h * dv:(h + 1) * dv]
        ms = jnp.mean(oh * oh, axis=-1, keepdims=True)
        heads.append((oh * lax.rsqrt(ms + NORM_EPS) * g_ref[...] * gate[:, h * dv:(h + 1) * dv]).astype(BF16))
    return jnp.concatenate(heads, axis=-1)


def _resmm_kernel(*refs, n_a, n_res, gla):
    a_refs, refs = refs[:n_a], refs[n_a:]
    w_ref, refs = refs[0], refs[1:]
    res_refs, (gate_ref, o_ref) = refs[:n_res], refs[n_res:]
    a = _gla_gated(*a_refs) if gla else _stream_tile(a_refs)
    acc = jnp.dot(a, w_ref[...], preferred_element_type=F32)
    o_ref[...] = _stream_tile(res_refs) + gate_ref[0] * acc


def _resmm(a, w, res, mod, slot_gate, *, n_tiles, gla=False):
    k, n_out = w.shape
    if gla:
        o_fwd, o_bwd, proj, norm_g = a
        r_col = (2 * GLA_DK + GLA_DV) // GLA_DV
        a_specs = [pl.BlockSpec((TM, GLA_DV), lambda i: (i, 0)),
                   pl.BlockSpec((TM, GLA_DV), lambda i: (i, 0)),
                   pl.BlockSpec((TM, GLA_DV), lambda i: (i, r_col)),
                   pl.BlockSpec((1, GLA_DV_HEAD), lambda i: (0, 0))]
        a_args = [o_fwd, o_bwd, proj, norm_g.reshape(1, -1)]
    else:
        a_specs, a_args = _stream_operand(a, TM, k, n_tiles)
    res_specs, res_args = _stream_operand(res, TM, n_out, n_tiles)
    return pl.pallas_call(
        functools.partial(_resmm_kernel, n_a=len(a_args), n_res=len(res_args), gla=gla),
        grid=(n_tiles,),
        in_specs=a_specs + [pl.BlockSpec((k, n_out), lambda i: (0, 0))] + res_specs + [_mod_spec(slot_gate)],
        out_specs=pl.BlockSpec((TM, n_out), lambda i: (i, 0)),
        out_shape=jax.ShapeDtypeStruct((n_tiles * TM, n_out), F32),
        compiler_params=_params(("arbitrary",)),
        name="resmm_gla" if gla else "resmm",
    )(*a_args, w, *res_args, mod)


def _ffn_kernel(x_ref, g_ref, sh_ref, sc_ref, gate_ref, w1_ref, w2_ref, *rest, final_norm):
    if final_norm:
        gf_ref, o_ref, h_ref, acc_ref = rest
    else:
        o_ref, h_ref, acc_ref = rest
    f = pl.program_id(1)

    def hidden_step(h):
        a = jnp.dot(h, w1_ref[0], preferred_element_type=F32)
        a = jnp.square(jnp.maximum(a, 0.0)).astype(BF16)
        return jnp.dot(a, w2_ref[0], preferred_element_type=F32)

    @pl.when(f == 0)
    def _():
        h = _modulated(x_ref[...], g_ref[...], sh_ref[0], sc_ref[0]).astype(BF16)
        h_ref[...] = h
        acc_ref[...] = hidden_step(h)

    @pl.when(f > 0)
    def _():
        acc_ref[...] += hidden_step(h_ref[...])

    @pl.when(f == pl.num_programs(1) - 1)
    def _():
        y = x_ref[...] + gate_ref[0] * acc_ref[...]
        if final_norm:
            ms = jnp.mean(y * y, axis=-1, keepdims=True)
            y = y * lax.rsqrt(ms + NORM_EPS) * gf_ref[...]
        o_ref[...] = y


def _ffn_weights(w_ffn_in, w_ffn_out):
    return w_ffn_in.astype(BF16), w_ffn_out.astype(BF16)


def _ffn(x, g, mod, w1, w2, layer, *, n_tiles, g_final=None):
    final_norm = g_final is not None
    tf = FFN_TF
    in_specs = [
        pl.BlockSpec((TM, D_MODEL), lambda i, f: (i, 0)),
        pl.BlockSpec((1, D_MODEL), lambda i, f: (0, 0)),
        _mod_spec(3),
        _mod_spec(4),
        _mod_spec(5),
        pl.BlockSpec((1, D_MODEL, tf), lambda i, f: (layer, 0, f)),
        pl.BlockSpec((1, tf, D_MODEL), lambda i, f: (layer, f, 0)),
    ]
    args = [x, g.reshape(1, D_MODEL), mod, mod, mod, w1, w2]
    if final_norm:
        in_specs.append(pl.BlockSpec((1, D_MODEL), lambda i, f: (0, 0)))
        args.append(g_final.reshape(1, D_MODEL))
    return pl.pallas_call(
        functools.partial(_ffn_kernel, final_norm=final_norm),
        grid=(n_tiles, D_FF // tf),
        in_specs=in_specs,
        out_specs=pl.BlockSpec((TM, D_MODEL), lambda i, f: (i, 0)),
        out_shape=jax.ShapeDtypeStruct((n_tiles * TM, D_MODEL), F32),
        scratch_shapes=[pltpu.VMEM((TM, D_MODEL), BF16), pltpu.VMEM((TM, D_MODEL), F32)],
        compiler_params=_params(("arbitrary", "arbitrary")),
        name="ffn_final" if final_norm else "ffn",
    )(*args)


def _dft_tables(n):
    idx = np.arange(n, dtype=np.int64)
    ang = 2.0 * np.pi * ((idx[:, None] * idx[None, :]) % n).astype(np.float64) / n
    return np.cos(ang) / math.sqrt(n), np.sin(ang) / math.sqrt(n)


def _channel_dft_matrix():
    c, s = _dft_tables(D_MODEL // FNO_GROUPS)
    return jnp.asarray(np.concatenate([c, s], axis=1), F32).astype(BF16)


def _chdft_kernel(*refs, n_x):
    x_refs, (g_ref, sh_ref, sc_ref, w_ref, oe_ref, oo_ref) = refs[:n_x], refs[n_x:]
    dg = D_MODEL // FNO_GROUPS
    w = w_ref[...]
    x_pairs = _stream_tile(x_refs)
    for parity, o_ref in enumerate((oe_ref, oo_ref)):
        x = x_pairs[:, parity * D_MODEL:(parity + 1) * D_MODEL]
        h = _modulated(x, g_ref[...], sh_ref[0], sc_ref[0]).astype(BF16)
        for grp in range(FNO_GROUPS):
            cols = slice(grp * dg, (grp + 1) * dg)
            r = jnp.dot(h[:, cols], w, preferred_element_type=F32)
            o_ref[:, cols] = r[:, :dg].astype(o_ref.dtype)
            o_ref[:, D_MODEL + grp * dg:D_MODEL + (grp + 1) * dg] = r[:, dg:].astype(o_ref.dtype)


def _chdft(x, g, mod, w):
    dg = D_MODEL // FNO_GROUPS
    half = jax.ShapeDtypeStruct((T_ALL // 2, 2 * D_MODEL), BF16)
    pair_view = lambda t: t.reshape(t.shape[0] // 2, 2 * D_MODEL)
    x_pairs = tuple(pair_view(t) for t in x) if isinstance(x, tuple) else pair_view(x)
    x_specs, x_args = _stream_operand(x_pairs, TM // 2, 2 * D_MODEL, NT_ALL)
    return pl.pallas_call(
        functools.partial(_chdft_kernel, n_x=len(x_args)),
        grid=(NT_ALL,),
        in_specs=x_specs + [
            pl.BlockSpec((1, D_MODEL), lambda i: (0, 0)),
            _mod_spec(0),
            _mod_spec(1),
            pl.BlockSpec((dg, 2 * dg), lambda i: (0, 0)),
        ],
        out_specs=[pl.BlockSpec((TM // 2, 2 * D_MODEL), lambda i: (i, 0))] * 2,
        out_shape=[half, half],
        compiler_params=_params(("arbitrary",)),
        name="chdft",
    )(*x_args, g.reshape(1, D_MODEL), mod, mod, w)


def _seq_dft_matrices(n, parity):
    half = n // 2
    kk = 2 * np.arange(half, dtype=np.int64) + parity
    if n <= 512:
        ang = 2.0 * np.pi * ((np.arange(half, dtype=np.int64)[:, None] * kk[None, :]) % n) / n
        return (jnp.asarray(np.cos(ang) / math.sqrt(n), F32).astype(BF16),
                jnp.asarray(-np.sin(ang) / math.sqrt(n), F32).astype(BF16))
    r = 64
    q = half // r
    ang_hi = 2.0 * np.pi * (((r * np.arange(q, dtype=np.int64))[:, None] * kk[None, :]) % n) / n
    ang_lo = 2.0 * np.pi * ((np.arange(r, dtype=np.int64)[:, None] * kk[None, :]) % n) / n
    ch = jnp.asarray(np.cos(ang_hi) / math.sqrt(n), F32)[:, None, :]
    sh = jnp.asarray(np.sin(ang_hi) / math.sqrt(n), F32)[:, None, :]
    cl = jnp.asarray(np.cos(ang_lo), F32)[None, :, :]
    sl = jnp.asarray(np.sin(ang_lo), F32)[None, :, :]
    c = (ch * cl - sh * sl).astype(BF16).reshape(half, half)
    s_neg = (-(sh * cl + ch * sl)).astype(BF16).reshape(half, half)
    return c, s_neg


def _seqdft_kernel(ce_ref, se_ref, co_ref, so_ref, ae_ref, be_ref, ao_ref, bo_ref, o_ref, acc_ref):
    kk = pl.program_id(3)

    @pl.when(kk == 0)
    def _():
        acc_ref[...] = jnp.zeros_like(acc_ref)

    acc_ref[0] += (jnp.dot(ce_ref[...], ae_ref[...], preferred_element_type=F32)
                   + jnp.dot(se_ref[...], be_ref[...], preferred_element_type=F32))
    acc_ref[1] += (jnp.dot(co_ref[...], ao_ref[...], preferred_element_type=F32)
                   + jnp.dot(so_ref[...], bo_ref[...], preferred_element_type=F32))

    @pl.when(kk == pl.num_programs(3) - 1)
    def _():
        o_ref[0] = (acc_ref[0] + acc_ref[1]).astype(o_ref.dtype)
        o_ref[1] = (acc_ref[0] - acc_ref[1]).astype(o_ref.dtype)


def _seqdft(ab_even, ab_odd, *, n, row0):
    half = n // 2
    tmm = min(half, 1024)
    tk = min(half, 512)
    tn = 1024
    nk = half // tk
    col_b = D_MODEL // tn
    mat = pl.BlockSpec((tmm, tk), lambda b, m, j, k: (m, k))
    a_spec = pl.BlockSpec((tk, tn), lambda b, m, j, k: (row0 // 2 // tk + b * nk + k, j))
    b_spec = pl.BlockSpec((tk, tn), lambda b, m, j, k: (row0 // 2 // tk + b * nk + k, col_b + j))
    out = pl.pallas_call(
        _seqdft_kernel,
        grid=(BATCH, half // tmm, D_MODEL // tn, nk),
        in_specs=[mat, mat, mat, mat, a_spec, b_spec, a_spec, b_spec],
        out_specs=pl.BlockSpec((2, tmm, tn), lambda b, m, j, k: (b, m, j)),
        out_shape=jax.ShapeDtypeStruct((BATCH * 2, half, D_MODEL), BF16),
        scratch_shapes=[pltpu.VMEM((2, tmm, tn), F32)],
        compiler_params=_params(("arbitrary",) * 4),
        name="seqdft",
    )(*_seq_dft_matrices(n, 0), *_seq_dft_matrices(n, 1), ab_even, ab_even, ab_odd, ab_odd)
    return out.reshape(BATCH * n, D_MODEL)


SGU_ROWS = 256


def _sgu_kernel(u_ref, v_ref, lg_ref, lb_ref, ws_ref, bs_ref, o_ref):
    v = v_ref[...].astype(F32)
    mu = jnp.mean(v, axis=-1, keepdims=True)
    vc = v - mu
    var = jnp.mean(vc * vc, axis=-1, keepdims=True)
    vn = (vc * lax.rsqrt(var + LN_EPS) * lg_ref[...] + lb_ref[...]).astype(BF16)
    we = GMLP_WIDTH // GMLP_HEADS
    for c in range(SGU_ROWS // GMLP_CHUNK):
        r0 = c * GMLP_CHUNK
        for h in range(GMLP_HEADS):
            s = jnp.dot(ws_ref[h], vn[r0:r0 + GMLP_CHUNK, h * we:(h + 1) * we], preferred_element_type=F32)
            s = s + bs_ref[:, h:h + 1]
            u = u_ref[r0:r0 + GMLP_CHUNK, h * we:(h + 1) * we].astype(F32)
            o_ref[r0:r0 + GMLP_CHUNK, h * we:(h + 1) * we] = (u * s).astype(o_ref.dtype)


def _sgu(z, ln_g, ln_b, w_s, b_s_t):
    return pl.pallas_call(
        _sgu_kernel,
        grid=(T_ALL // SGU_ROWS,),
        in_specs=[
            pl.BlockSpec((SGU_ROWS, GMLP_WIDTH), lambda i: (i, 0)),
            pl.BlockSpec((SGU_ROWS, GMLP_WIDTH), lambda i: (i, 1)),
            pl.BlockSpec((1, GMLP_WIDTH), lambda i: (0, 0)),
            pl.BlockSpec((1, GMLP_WIDTH), lambda i: (0, 0)),
            pl.BlockSpec((GMLP_HEADS, GMLP_CHUNK, GMLP_CHUNK), lambda i: (0, 0, 0)),
            pl.BlockSpec((GMLP_CHUNK, GMLP_HEADS), lambda i: (0, 0)),
        ],
        out_specs=pl.BlockSpec((SGU_ROWS, GMLP_WIDTH), lambda i: (i, 0)),
        out_shape=jax.ShapeDtypeStruct((T_ALL, GMLP_WIDTH), BF16),
        compiler_params=_params(("arbitrary",)),
        name="sgu",
    )(z, z, ln_g.reshape(1, -1), ln_b.reshape(1, -1), w_s, b_s_t)


def _dot_nt(a, b):
    return lax.dot_general(a, b, (((1,), (1,)), ((), ())), preferred_element_type=F32)


ATTN_KEY_CHUNK = 256
ATTN_TQ = 512
ATTN_Q_SCALE = (DIFF_HEAD_DIM ** -0.5) * math.log2(math.e)


def _attn_kernel(lam_ref, g_ref, q_ref, *rest, latent, lam_init):
    if latent:
        kx_ref, vx_ref, kc_ref, vc_ref, o_ref, s_ref = rest
        chunks = [(kx_ref, vx_ref, r) for r in range(0, SEQ, ATTN_KEY_CHUNK)]
    else:
        kc_ref, vc_ref, o_ref, s_ref = rest
        chunks = []
    chunks += [(kc_ref, vc_ref, r) for r in range(0, CTX_LEN, ATTN_KEY_CHUNK)]
    hd = DIFF_HEAD_DIM
    kc = ATTN_KEY_CHUNK
    lv = lam_ref[...]
    lam = (jnp.exp(jnp.sum(lv[0:1] * lv[1:2], axis=-1, keepdims=True))
           - jnp.exp(jnp.sum(lv[2:3] * lv[3:4], axis=-1, keepdims=True)) + lam_init)
    q = q_ref[...]
    comps = (0, 1)
    qs = [q[:, c * hd:(c + 1) * hd] for c in comps]
    m = [None, None]
    for j, (k_ref, _, r0) in enumerate(chunks):
        for c in comps:
            s = _dot_nt(qs[c], k_ref[r0:r0 + kc, c * hd:(c + 1) * hd])
            s_ref[c, :, j * kc:(j + 1) * kc] = s
            sm = jnp.maximum(s[:, :LANE], s[:, LANE:])
            m[c] = sm if m[c] is None else jnp.maximum(m[c], sm)
    m = [jnp.max(mc, axis=-1, keepdims=True) for mc in m]
    l = [None, None]
    acc = [None, None]

    def exp_chunk(j, c, row_max):
        e = jnp.exp2(s_ref[c, :, j * kc:(j + 1) * kc] - row_max)
        ep = e[:, :LANE] + e[:, LANE:]
        l[c] = ep if l[c] is None else l[c] + ep
        return e.astype(BF16)

    ahead = 4
    n_chunks = len(chunks)
    ready = {(j, c): exp_chunk(j, c, m[c]) for j in range(min(ahead, n_chunks)) for c in comps}
    for j, (_, v_ref, r0) in enumerate(chunks):
        for c in comps:
            pv = jnp.dot(ready.pop((j, c)), v_ref[r0:r0 + kc, :], preferred_element_type=F32)
            acc[c] = pv if acc[c] is None else acc[c] + pv
            if j + ahead < n_chunks:
                zero = jnp.minimum(jnp.abs(pv[-1:, :1]), 0.0)
                ready[j + ahead, c] = exp_chunk(j + ahead, c, m[c] + zero)
    outs = [acc[c] * (1.0 / jnp.sum(l[c], axis=-1, keepdims=True)) for c in comps]
    o = outs[0] - lam * outs[1]
    ms = jnp.mean(o * o, axis=-1, keepdims=True)
    o = o * lax.rsqrt(ms + NORM_EPS) * g_ref[...] * (1.0 - lam_init)
    o_ref[...] = o.astype(o_ref.dtype)


def _diff_attention(qkv, lam_vecs, subln_g, lam_init, *, latent):
    tq = ATTN_TQ if latent else CTX_LEN
    hw = 2 * DIFF_HEAD_DIM
    k_col = D_MODEL // hw
    v_col = 2 * D_MODEL // hw
    n_q = SEQ if latent else CTX_LEN
    q_row0 = 0 if latent else T_X
    nq = n_q // tq
    in_specs = [
        pl.BlockSpec((4, DIFF_HEAD_DIM), lambda b, h, i: (0, 0)),
        pl.BlockSpec((1, hw), lambda b, h, i: (0, 0)),
        pl.BlockSpec((tq, hw), lambda b, h, i: (q_row0 // tq + b * nq + i, h)),
    ]
    args = [lam_vecs, subln_g.reshape(1, hw), qkv]
    if latent:
        in_specs += [
            pl.BlockSpec((SEQ, hw), lambda b, h, i: (b, k_col + h)),
            pl.BlockSpec((SEQ, hw), lambda b, h, i: (b, v_col + h)),
        ]
        args += [qkv, qkv]
    in_specs += [
        pl.BlockSpec((CTX_LEN, hw), lambda b, h, i: (T_X // CTX_LEN + b, k_col + h)),
        pl.BlockSpec((CTX_LEN, hw), lambda b, h, i: (T_X // CTX_LEN + b, v_col + h)),
    ]
    args += [qkv, qkv]
    return pl.pallas_call(
        functools.partial(_attn_kernel, latent=latent, lam_init=lam_init),
        grid=(BATCH, DIFF_HEADS, nq),
        in_specs=in_specs,
        out_specs=pl.BlockSpec((tq, hw), lambda b, h, i: (b * nq + i, h)),
        out_shape=jax.ShapeDtypeStruct((BATCH * n_q, D_MODEL), BF16),
        scratch_shapes=[pltpu.VMEM((2, tq, (SEQ if latent else 0) + CTX_LEN), F32)],
        compiler_params=_params(("arbitrary",) * 3),
        name="diffattn_latent" if latent else "diffattn_ctx",
    )(*args)


GLA_CTX_CHUNKS = CTX_LEN // GLA_CHUNK
GLA_X_CHUNKS = SEQ // GLA_CHUNK
GLA_STEPS = GLA_CTX_CHUNKS + GLA_X_CHUNKS


def _split3_bf16(a):
    hi = a.astype(BF16)
    r = a - hi.astype(F32)
    mid = r.astype(BF16)
    lo = (r - mid.astype(F32)).astype(BF16)
    return hi, mid, lo


def _gla_scan_kernel(qf_ref, kf_ref, vf_ref, gf_ref, qb_ref, kb_ref, vb_ref, gb_ref, wup_ref, bg_ref, tri_ref,
                     of_ref, ob_ref, st_ref):
    @pl.when(pl.program_id(1) == 0)
    def _():
        st_ref[...] = jnp.zeros_like(st_ref)

    dk, dv = GLA_DK_HEAD, GLA_DV_HEAD
    dirs = (0, 1)
    chains = [(d, h) for h in range(GLA_HEADS) for d in dirs]
    q_refs, k_refs, v_refs, g_refs, o_refs = (qf_ref, qb_ref), (kf_ref, kb_ref), (vf_ref, vb_ref), (gf_ref, gb_ref), (of_ref, ob_ref)
    tri = [tri_ref[d] for d in dirs]
    mask = [t.astype(F32) > 0.0 for t in tri]
    z = [jnp.dot(g_refs[d][...], wup_ref[d], preferred_element_type=F32) + bg_ref[d] for d in dirs]
    g = [(jnp.minimum(zd, 0.0) - jnp.log(1.0 + jnp.exp(-jnp.abs(zd)))) * (1.0 / GLA_TAU) for zd in z]
    parts = [_split3_bf16(gd) for gd in g]
    bcum = [sum(jnp.dot(tri[d], part, preferred_element_type=F32) for part in parts[d]) for d in dirs]
    blast = [jnp.sum(gd, axis=0, keepdims=True) for gd in g]
    q = [q_refs[d][...].astype(F32) * (dk ** -0.5) for d in dirs]
    k = [k_refs[d][...].astype(F32) for d in dirs]
    q_dec = [(q[d] * jnp.exp(bcum[d])).astype(BF16) for d in dirs]
    k_inv = [(k[d] * jnp.exp(-bcum[d])).astype(BF16) for d in dirs]
    k_tail = [(k[d] * jnp.exp(blast[d] - bcum[d])).astype(BF16) for d in dirs]
    decay = [jnp.exp(blast[d]) for d in dirs]
    ks = lambda h: slice(h * dk, (h + 1) * dk)
    vs = lambda h: slice(h * dv, (h + 1) * dv)
    st = {c: st_ref[c[0], c[1]] for c in chains}
    att = {(d, h): _dot_nt(q_dec[d][:, ks(h)], k_inv[d][:, ks(h)]) for d, h in chains}
    inter = {(d, h): _dot_nt(q_dec[d][:, ks(h)], st[d, h].astype(BF16)) for d, h in chains}
    att = {(d, h): jnp.where(mask[d], att[d, h], 0.0).astype(BF16) for d, h in chains}
    v = {(d, h): v_refs[d][:, vs(h)] for d, h in chains}
    upd = {c: lax.dot_general(v[c], k_tail[c[0]][:, ks(c[1])], (((0,), (0,)), ((), ())),
                              preferred_element_type=F32) for c in chains}
    out = {c: inter[c] + jnp.dot(att[c], v[c], preferred_element_type=F32) for c in chains}
    for d, h in chains:
        o_refs[d][:, vs(h)] = out[d, h].astype(o_refs[d].dtype)
        st_ref[d, h] = decay[d][:, ks(h)] * st[d, h] + upd[d, h]


def _gla_scan(proj, gates, wup_pad, b_gate, tri):
    def row_block(d):
        def index(b, s):
            c_idx = s if d == 0 else GLA_CTX_CHUNKS - 1 - s
            x_idx = s - GLA_CTX_CHUNKS if d == 0 else GLA_STEPS - 1 - s
            return jnp.where(s < GLA_CTX_CHUNKS,
                             T_X // GLA_CHUNK + b * GLA_CTX_CHUNKS + c_idx,
                             b * GLA_X_CHUNKS + x_idx)
        return index

    def chunk_specs(d):
        rb = row_block(d)
        return [
            pl.BlockSpec((GLA_CHUNK, GLA_DK), lambda b, s: (rb(b, s), 0)),
            pl.BlockSpec((GLA_CHUNK, GLA_DK), lambda b, s: (rb(b, s), 1)),
            pl.BlockSpec((GLA_CHUNK, GLA_DV), lambda b, s: (rb(b, s), 2 * GLA_DK // GLA_DV)),
            pl.BlockSpec((GLA_CHUNK, LANE), lambda b, s: (rb(b, s), 0)),
        ]

    whole = lambda shape: pl.BlockSpec(shape, lambda b, s: (0,) * len(shape))
    out_sds = jax.ShapeDtypeStruct((T_ALL, GLA_DV), BF16)
    return pl.pallas_call(
        _gla_scan_kernel,
        grid=(BATCH, GLA_STEPS),
        in_specs=chunk_specs(0) + chunk_specs(1) + [
            whole((2, LANE, GLA_DK)), whole((2, 1, GLA_DK)), whole((2, GLA_CHUNK, GLA_CHUNK))],
        out_specs=[pl.BlockSpec((GLA_CHUNK, GLA_DV), lambda b, s: (row_block(0)(b, s), 0)),
                   pl.BlockSpec((GLA_CHUNK, GLA_DV), lambda b, s: (row_block(1)(b, s), 0))],
        out_shape=[out_sds, out_sds],
        scratch_shapes=[pltpu.VMEM((2, GLA_HEADS, GLA_DV_HEAD, GLA_DK_HEAD), F32)],
        compiler_params=_params(("arbitrary",) * 2),
        name="gla_scan",
    )(proj, proj, proj, gates, proj, proj, proj, gates, wup_pad, b_gate, tri)


def _rope_tables():
    hd = DIFF_HEAD_DIM
    pos = np.arange(SEQ)
    row = (pos // GRID_W).astype(np.float32)
    col = (pos % GRID_W).astype(np.float32)
    n_freq = hd // 4
    inv = (ROPE_THETA ** (-np.arange(n_freq, dtype=np.float32) / n_freq)).astype(np.float32)
    ang = np.concatenate([row[:, None] * inv, col[:, None] * inv], axis=-1).astype(np.float64)
    cos, sin = np.cos(ang), np.sin(ang)
    cs = np.concatenate([cos, cos], axis=-1).astype(np.float32)
    sn = np.concatenate([-sin, sin], axis=-1).astype(np.float32)
    return jnp.asarray(cs), jnp.asarray(sn)


def _gla_tri():
    lower = np.tril(np.ones((GLA_CHUNK, GLA_CHUNK), np.float32))
    return jnp.asarray(np.stack([lower, lower.T])).astype(BF16)


def _mixer(i, tok, mod, g_mix, w):
    kind = i % N_MIXERS
    if kind == 0:
        ab_even, ab_odd = _chdft(tok, g_mix, mod, _channel_dft_matrix())
        zx = _seqdft(ab_even, ab_odd, n=SEQ, row0=0)
        zc = _seqdft(ab_even, ab_odd, n=CTX_LEN, row0=T_X)
        return (zx, zc), w["fno_w_out"]
    if kind == 1:
        z = _modmm(tok, g_mix, mod, 0, 1, w["gmlp_w_in"].astype(BF16), epilogue="gelu")
        mixed = _sgu(z, w["gmlp_ln_g"], w["gmlp_ln_b"], w["gmlp_w_s"].astype(BF16), w["gmlp_b_s"].T)
        return mixed, w["gmlp_w_out"]
    if kind == 2:
        lam_init = 0.8 - 0.6 * math.exp(-0.3 * i)
        qkv = _modmm(tok, g_mix, mod, 0, 1, w["diff_w_in"].astype(BF16), epilogue="rope", rope=_rope_tables())
        ox = _diff_attention(qkv, w["diff_lambda"], w["diff_subln_g"], lam_init, latent=True)
        oc = _diff_attention(qkv, w["diff_lambda"], w["diff_subln_g"], lam_init, latent=False)
        return (ox, oc), w["diff_w_out"]
    n_main = 2 * GLA_DK + 2 * GLA_DV
    w_in = w["gla_w_in"]
    proj = _modmm(tok, g_mix, mod, 0, 1, w_in[:, :n_main].astype(BF16))
    w_gate = jnp.pad(w_in[:, n_main:], ((0, 0), (0, LANE - 2 * GLA_GATE_RANK))).astype(BF16)
    gates = _modmm(tok, g_mix, mod, 0, 1, w_gate)
    wup = w["gla_w_gate_up"]
    wup_pad = jnp.zeros((2, LANE, GLA_DK), F32)
    wup_pad = wup_pad.at[0, :GLA_GATE_RANK].set(wup[0])
    wup_pad = wup_pad.at[1, GLA_GATE_RANK:2 * GLA_GATE_RANK].set(wup[1])
    o_fwd, o_bwd = _gla_scan(proj, gates, wup_pad.astype(BF16), w["gla_b_gate"].reshape(2, 1, GLA_DK), _gla_tri())
    return (o_fwd, o_bwd, proj, w["gla_norm_g"]), w["gla_w_out"]


def _layer(i, tok, mod, w, ffn_w, g_final=None):
    n_tiles = NT_X if i == DEPTH - 1 else NT_ALL
    mixed, w_out = _mixer(i, tok, mod, w["g_norm_mix"], w)
    tok = _resmm(mixed, w_out.astype(BF16), tok, mod, 2, n_tiles=n_tiles, gla=i % N_MIXERS == 3)
    return _ffn(tok, w["g_norm_ffn"], mod, *ffn_w, i, n_tiles=n_tiles, g_final=g_final)


def kernel(x, c, ctx, c_ctx, w_ada, b_ada, g_norm_mix, g_norm_ffn, w_ffn_in, w_ffn_out, g_final, fno_w_out, gmlp_w_in, gmlp_ln_g, gmlp_ln_b, gmlp_w_s, gmlp_b_s, gmlp_w_out, diff_w_in, diff_lambda, diff_subln_g, diff_w_out, gla_w_in, gla_w_gate_up, gla_b_gate, gla_norm_g, gla_w_out):
    tok = (x.reshape(T_X, D_MODEL), ctx.reshape(T_C, D_MODEL))
    c_rows = jnp.concatenate([c, c_ctx[None, :], jnp.zeros((MOD_ROWS - BATCH - 1, D_MODEL), F32)], axis=0)
    mod_all = _ada_table(c_rows, w_ada, b_ada).reshape(DEPTH, MOD_ROWS * N_ADA, 1, D_MODEL)
    mixer_weights = (
        dict(fno_w_out=fno_w_out),
        dict(gmlp_w_in=gmlp_w_in, gmlp_ln_g=gmlp_ln_g, gmlp_ln_b=gmlp_ln_b, gmlp_w_s=gmlp_w_s,
             gmlp_b_s=gmlp_b_s, gmlp_w_out=gmlp_w_out),
        dict(diff_w_in=diff_w_in, diff_lambda=diff_lambda, diff_subln_g=diff_subln_g, diff_w_out=diff_w_out),
        dict(gla_w_in=gla_w_in, gla_w_gate_up=gla_w_gate_up, gla_b_gate=gla_b_gate, gla_norm_g=gla_norm_g,
             gla_w_out=gla_w_out),
    )
    ffn_w = _ffn_weights(w_ffn_in, w_ffn_out)
    for i in range(DEPTH):
        kind, j = i % N_MIXERS, i // N_MIXERS
        w = {name: arr[j] for name, arr in mixer_weights[kind].items()}
        w.update(g_norm_mix=g_norm_mix[i], g_norm_ffn=g_norm_ffn[i])
        tok = _layer(i, tok, mod_all[i], w, ffn_w, g_final=g_final if i == DEPTH - 1 else None)
    return tok.reshape(BATCH, SEQ, D_MODEL)
```

```python
import functools
import math

import numpy as np
import jax
import jax.numpy as jnp
from jax import lax
from jax.experimental import pallas as pl
from jax.experimental.pallas import tpu as pltpu

D_MODEL = 2048
BATCH = 2
SEQ = 4096
DEPTH = 4
GRID_W = 64
CTX_LEN = 256
N_MIXERS = 4
NORM_EPS = 1e-6
D_FF = 4 * D_MODEL
N_ADA = 6
FNO_GROUPS = 4
GMLP_CHUNK = 128
GMLP_HEADS = 16
GMLP_WIDTH = D_MODEL
DIFF_HEADS = 8
DIFF_HEAD_DIM = D_MODEL // DIFF_HEADS // 2
ROPE_THETA = 10000.0
GLA_HEADS = 4
GLA_DK = D_MODEL // 2
GLA_DV = D_MODEL
GLA_DK_HEAD = GLA_DK // GLA_HEADS
GLA_DV_HEAD = GLA_DV // GLA_HEADS
GLA_GATE_RANK = 16
GLA_TAU = 16.0
GLA_CHUNK = 64
LN_EPS = 1e-5

T_X = BATCH * SEQ
T_C = BATCH * CTX_LEN
T_ALL = T_X + T_C
TM = 512
NT_X = T_X // TM
NT_ALL = T_ALL // TM
TILES_PER_BATCH = SEQ // TM
MOD_ROWS = 16
CTX_GROUP = BATCH
LANE = 128
MM_TN = 2048
FFN_TF = 512
FFN_TM = 1024

BF16 = jnp.bfloat16
F32 = jnp.float32
VMEM_LIMIT = 56 * 1024 * 1024


def _params(semantics):
    return pltpu.CompilerParams(dimension_semantics=semantics, vmem_limit_bytes=VMEM_LIMIT)


def _mod_spec(slot, tile_rows=TM, first_tile=0):
    def index(i, *_):
        group = jnp.minimum((i + first_tile) // (SEQ // tile_rows), CTX_GROUP)
        return group * N_ADA + slot, 0, 0
    return pl.BlockSpec((1, 1, D_MODEL), index)


def _modulated(x, g, shift, scale):
    ms = jnp.mean(x * x, axis=-1, keepdims=True)
    y = x * lax.rsqrt(ms + NORM_EPS) * g
    return y * (1.0 + scale) + shift


def _ada_kernel(c_ref, w_ref, b_ref, o_ref):
    c = c_ref[...]
    s = (c * jax.nn.sigmoid(c)).astype(BF16)
    w = w_ref[0].astype(BF16)
    o_ref[0] = jnp.dot(s, w, preferred_element_type=F32) + b_ref[0]


def _ada_table(c8, w_ada, b_ada):
    tn = 1024
    n_out = N_ADA * D_MODEL
    return pl.pallas_call(
        _ada_kernel,
        grid=(DEPTH, n_out // tn),
        in_specs=[
            pl.BlockSpec((MOD_ROWS, D_MODEL), lambda l, j: (0, 0)),
            pl.BlockSpec((1, D_MODEL, tn), lambda l, j: (l, 0, j)),
            pl.BlockSpec((1, 1, tn), lambda l, j: (l, 0, j)),
        ],
        out_specs=pl.BlockSpec((1, MOD_ROWS, tn), lambda l, j: (l, 0, j)),
        out_shape=jax.ShapeDtypeStruct((DEPTH, MOD_ROWS, n_out), F32),
        compiler_params=_params(("arbitrary", "arbitrary")),
        name="ada_table",
    )(c8, w_ada, b_ada.reshape(DEPTH, 1, n_out))


def _gelu_exact(a):
    return 0.5 * a * (1.0 + lax.erf(a * (1.0 / math.sqrt(2.0))))


def _modmm_kernel(*refs, n_x, epilogue, tn):
    x_refs, (g_ref, sh_ref, sc_ref, w_ref, *rest) = refs[:n_x], refs[n_x:]
    if epilogue == "rope":
        cs_ref, sn_ref, o_ref, h_ref = rest
    else:
        o_ref, h_ref = rest
    i = pl.program_id(0)
    j = pl.program_id(1)

    def finish(acc):
        if epilogue == "gelu":
            o_ref[...] = _gelu_exact(acc).astype(o_ref.dtype)
        elif epilogue == "rope":
            acc = acc * jnp.where(j < D_MODEL // tn, ATTN_Q_SCALE, 1.0)
            rotate = jnp.logical_and(j < (2 * D_MODEL) // tn, i < NT_X)

            @pl.when(rotate)
            def _():
                cs = cs_ref[...]
                sn = sn_ref[...]
                for blk in range(tn // LANE):
                    a = acc[:, blk * LANE:(blk + 1) * LANE]
                    swapped = pltpu.roll(a, LANE // 2, 1)
                    o_ref[:, blk * LANE:(blk + 1) * LANE] = (a * cs + swapped * sn).astype(o_ref.dtype)

            @pl.when(jnp.logical_not(rotate))
            def _():
                o_ref[...] = acc.astype(o_ref.dtype)
        else:
            o_ref[...] = acc.astype(o_ref.dtype)

    @pl.when(j == 0)
    def _():
        h = _modulated(_stream_tile(x_refs), g_ref[...], sh_ref[0], sc_ref[0]).astype(BF16)
        h_ref[...] = h
        finish(jnp.dot(h, w_ref[...], preferred_element_type=F32))

    @pl.when(j > 0)
    def _():
        finish(jnp.dot(h_ref[...], w_ref[...], preferred_element_type=F32))


def _modmm(x, g, mod, slot_shift, slot_scale, w, *, n_out=None, epilogue="plain", rope=None, out_dtype=BF16):
    k = w.shape[0]
    n_out = w.shape[1] if n_out is None else n_out
    tn = min(MM_TN, n_out)
    x_specs, x_args = _stream_operand(x, TM, k, NT_ALL)
    in_specs = x_specs + [
        pl.BlockSpec((1, k), lambda i, j: (0, 0)),
        _mod_spec(slot_shift),
        _mod_spec(slot_scale),
        pl.BlockSpec((k, tn), lambda i, j: (0, j)),
    ]
    args = x_args + [g.reshape(1, k), mod, mod, w]
    if epilogue == "rope":
        pos = lambda i, j: (jnp.where(i < NT_X, i % TILES_PER_BATCH, 0), 0)
        in_specs += [pl.BlockSpec((TM, LANE), pos), pl.BlockSpec((TM, LANE), pos)]
        args += list(rope)
    return pl.pallas_call(
        functools.partial(_modmm_kernel, n_x=len(x_args), epilogue=epilogue, tn=tn),
        grid=(NT_ALL, n_out // tn),
        in_specs=in_specs,
        out_specs=pl.BlockSpec((TM, tn), lambda i, j: (i, j)),
        out_shape=jax.ShapeDtypeStruct((T_ALL, n_out), out_dtype),
        scratch_shapes=[pltpu.VMEM((TM, k), BF16)],
        compiler_params=_params(("arbitrary", "arbitrary")),
        name="modmm_" + epilogue,
    )(*args)


def _stream_operand(arr, rows, width, n_tiles):
    if isinstance(arr, tuple) and n_tiles == NT_X:
        arr = arr[0]
    if isinstance(arr, tuple):
        assert arr[1].shape == (rows, width)
        return ([pl.BlockSpec((rows, width), lambda i, *_: (jnp.minimum(i, NT_X - 1), 0)),
                 pl.BlockSpec((rows, width), lambda i, *_: (0, 0))], list(arr))
    return [pl.BlockSpec((rows, width), lambda i, *_: (i, 0))], [arr]


def _stream_tile(refs):
    if len(refs) == 2:
        return jnp.where(pl.program_id(0) < NT_X, refs[0][...], refs[1][...])
    return refs[0][...]


def _gla_gated(of_ref, ob_ref, r_ref, g_ref):
    o = of_ref[...].astype(F32) + ob_ref[...].astype(F32)
    r = r_ref[...].astype(F32)
    gate = r * jax.nn.sigmoid(r)
    dv = GLA_DV_HEAD
    heads = []
    for h in range(GLA_HEADS):
        oh = o[:, h * dv:(h + 1) * dv]
        ms = jnp.mean(oh * oh, axis=-1, keepdims=True)
        heads.append((oh * lax.rsqrt(ms + NORM_EPS) * g_ref[...] * gate[:, h * dv:(h + 1) * dv]).astype(BF16))
    return jnp.concatenate(heads, axis=-1)


def _resmm_kernel(*refs, n_a, n_res, gla):
    a_refs, refs = refs[:n_a], refs[n_a:]
    w_ref, refs = refs[0], refs[1:]
    res_refs, (gate_ref, o_ref) = refs[:n_res], refs[n_res:]
    a = _gla_gated(*a_refs) if gla else _stream_tile(a_refs)
    acc = jnp.dot(a, w_ref[...], preferred_element_type=F32)
    o_ref[...] = _stream_tile(res_refs) + gate_ref[0] * acc


def _resmm(a, w, res, mod, slot_gate, *, n_tiles, gla=False):
    k, n_out = w.shape
    if gla:
        o_fwd, o_bwd, proj, norm_g = a
        r_col = (2 * GLA_DK + GLA_DV) // GLA_DV
        a_specs = [pl.BlockSpec((TM, GLA_DV), lambda i: (i, 0)),
                   pl.BlockSpec((TM, GLA_DV), lambda i: (i, 0)),
                   pl.BlockSpec((TM, GLA_DV), lambda i: (i, r_col)),
                   pl.BlockSpec((1, GLA_DV_HEAD), lambda i: (0, 0))]
        a_args = [o_fwd, o_bwd, proj, norm_g.reshape(1, -1)]
    else:
        a_specs, a_args = _stream_operand(a, TM, k, n_tiles)
    res_specs, res_args = _stream_operand(res, TM, n_out, n_tiles)
    return pl.pallas_call(
        functools.partial(_resmm_kernel, n_a=len(a_args), n_res=len(res_args), gla=gla),
        grid=(n_tiles,),
        in_specs=a_specs + [pl.BlockSpec((k, n_out), lambda i: (0, 0))] + res_specs + [_mod_spec(slot_gate)],
        out_specs=pl.BlockSpec((TM, n_out), lambda i: (i, 0)),
        out_shape=jax.ShapeDtypeStruct((n_tiles * TM, n_out), F32),
        compiler_params=_params(("arbitrary",)),
        name="resmm_gla" if gla else "resmm",
    )(*a_args, w, *res_args, mod)


def _ffn_kernel(x_ref, g_ref, sh_ref, sc_ref, gate_ref, w1_ref, w2_ref, *rest, final_norm):
    if final_norm:
        gf_ref, o_ref, h_ref = rest
    else:
        o_ref, h_ref = rest
    f = pl.program_id(1)

    def hidden_step(h):
        a = jnp.dot(h, w1_ref[0].astype(BF16), preferred_element_type=F32)
        a = jnp.square(jnp.maximum(a, 0.0)).astype(BF16)
        return jnp.dot(a, w2_ref[0].astype(BF16), preferred_element_type=F32)

    @pl.when(f == 0)
    def _():
        h = _modulated(x_ref[...], g_ref[...], sh_ref[0], sc_ref[0]).astype(BF16)
        h_ref[...] = h
        o_ref[...] = hidden_step(h)

    @pl.when(f > 0)
    def _():
        o_ref[...] += hidden_step(h_ref[...])

    @pl.when(f == pl.num_programs(1) - 1)
    def _():
        y = x_ref[...] + gate_ref[0] * o_ref[...]
        if final_norm:
            ms = jnp.mean(y * y, axis=-1, keepdims=True)
            y = y * lax.rsqrt(ms + NORM_EPS) * gf_ref[...]
        o_ref[...] = y


def _ffn_part(x, g, mod, w1, w2, layer, *, tile_rows, first_tile, n_tiles, g_final=None):
    final_norm = g_final is not None
    tf = FFN_TF
    single = pl.Buffered(1)
    in_specs = [
        pl.BlockSpec((tile_rows, D_MODEL), lambda i, f: (i + first_tile, 0)),
        pl.BlockSpec((1, D_MODEL), lambda i, f: (0, 0)),
        _mod_spec(3, tile_rows, first_tile),
        _mod_spec(4, tile_rows, first_tile),
        _mod_spec(5, tile_rows, first_tile),
        pl.BlockSpec((1, D_MODEL, tf), lambda i, f: (layer, 0, f)),
        pl.BlockSpec((1, tf, D_MODEL), lambda i, f: (layer, f, 0)),
    ]
    args = [x, g.reshape(1, D_MODEL), mod, mod, mod, w1, w2]
    if final_norm:
        in_specs.append(pl.BlockSpec((1, D_MODEL), lambda i, f: (0, 0)))
        args.append(g_final.reshape(1, D_MODEL))
    return pl.pallas_call(
        functools.partial(_ffn_kernel, final_norm=final_norm),
        grid=(n_tiles, D_FF // tf),
        in_specs=in_specs,
        out_specs=pl.BlockSpec((tile_rows, D_MODEL), lambda i, f: (i, 0), pipeline_mode=single),
        out_shape=jax.ShapeDtypeStruct((n_tiles * tile_rows, D_MODEL), F32),
        scratch_shapes=[pltpu.VMEM((tile_rows, D_MODEL), BF16)],
        compiler_params=_params(("arbitrary", "arbitrary")),
        name="ffn_final" if final_norm else "ffn",
    )(*args)


def _ffn(x, g, mod, w1, w2, layer, *, with_context, g_final=None):
    latent = _ffn_part(x, g, mod, w1, w2, layer, tile_rows=FFN_TM, first_tile=0, n_tiles=T_X // FFN_TM,
                       g_final=g_final)
    if not with_context:
        return latent
    context = _ffn_part(x, g, mod, w1, w2, layer, tile_rows=T_C, first_tile=T_X // T_C, n_tiles=1)
    return latent, context


def _dft_tables(n):
    idx = np.arange(n, dtype=np.int64)
    ang = 2.0 * np.pi * ((idx[:, None] * idx[None, :]) % n).astype(np.float64) / n
    return np.cos(ang) / math.sqrt(n), np.sin(ang) / math.sqrt(n)


def _channel_dft_matrix():
    c, s = _dft_tables(D_MODEL // FNO_GROUPS)
    return jnp.asarray(np.concatenate([c, s], axis=1), F32).astype(BF16)


def _chdft_kernel(*refs, n_x):
    x_refs, (g_ref, sh_ref, sc_ref, w_ref, sel_ref, oe_ref, oo_ref) = refs[:n_x], refs[n_x:]
    dg = D_MODEL // FNO_GROUPS
    w = w_ref[...]
    h_all = _modulated(_stream_tile(x_refs), g_ref[...], sh_ref[0], sc_ref[0]).astype(BF16)
    for parity, o_ref in enumerate((oe_ref, oo_ref)):
        h = jnp.dot(sel_ref[parity], h_all, preferred_element_type=F32).astype(BF16)
        for grp in range(FNO_GROUPS):
            cols = slice(grp * dg, (grp + 1) * dg)
            r = jnp.dot(h[:, cols], w, preferred_element_type=F32)
            o_ref[:, cols] = r[:, :dg].astype(o_ref.dtype)
            o_ref[:, D_MODEL + grp * dg:D_MODEL + (grp + 1) * dg] = r[:, dg:].astype(o_ref.dtype)


def _chdft(x, g, mod, w):
    dg = D_MODEL // FNO_GROUPS
    half = jax.ShapeDtypeStruct((T_ALL // 2, 2 * D_MODEL), BF16)
    x_specs, x_args = _stream_operand(x, TM, D_MODEL, NT_ALL)
    rows = np.arange(TM // 2)
    sel = np.zeros((2, TM // 2, TM), np.float32)
    sel[0, rows, 2 * rows] = 1.0
    sel[1, rows, 2 * rows + 1] = 1.0
    return pl.pallas_call(
        functools.partial(_chdft_kernel, n_x=len(x_args)),
        grid=(NT_ALL,),
        in_specs=x_specs + [
            pl.BlockSpec((1, D_MODEL), lambda i: (0, 0)),
            _mod_spec(0),
            _mod_spec(1),
            pl.BlockSpec((dg, 2 * dg), lambda i: (0, 0)),
            pl.BlockSpec((2, TM // 2, TM), lambda i: (0, 0, 0)),
        ],
        out_specs=[pl.BlockSpec((TM // 2, 2 * D_MODEL), lambda i: (i, 0))] * 2,
        out_shape=[half, half],
        compiler_params=_params(("arbitrary",)),
        name="chdft",
    )(*x_args, g.reshape(1, D_MODEL), mod, mod, w, jnp.asarray(sel).astype(BF16))


def _seq_dft_matrices(n, parity):
    half = n // 2
    kk = 2 * np.arange(half, dtype=np.int64) + parity
    if n <= 512:
        ang = 2.0 * np.pi * ((np.arange(half, dtype=np.int64)[:, None] * kk[None, :]) % n) / n
        return (jnp.asarray(np.cos(ang) / math.sqrt(n), F32).astype(BF16),
                jnp.asarray(-np.sin(ang) / math.sqrt(n), F32).astype(BF16))
    r = 64
    q = half // r
    ang_hi = 2.0 * np.pi * (((r * np.arange(q, dtype=np.int64))[:, None] * kk[None, :]) % n) / n
    ang_lo = 2.0 * np.pi * ((np.arange(r, dtype=np.int64)[:, None] * kk[None, :]) % n) / n
    ch = jnp.asarray(np.cos(ang_hi) / math.sqrt(n), F32)[:, None, :]
    sh = jnp.asarray(np.sin(ang_hi) / math.sqrt(n), F32)[:, None, :]
    cl = jnp.asarray(np.cos(ang_lo), F32)[None, :, :]
    sl = jnp.asarray(np.sin(ang_lo), F32)[None, :, :]
    c = (ch * cl - sh * sl).astype(BF16).reshape(half, half)
    s_neg = (-(sh * cl + ch * sl)).astype(BF16).reshape(half, half)
    return c, s_neg


def _seqdft_kernel(ce_ref, se_ref, co_ref, so_ref, ae_ref, be_ref, ao_ref, bo_ref, o_ref, acc_ref):
    kk = pl.program_id(3)

    @pl.when(kk == 0)
    def _():
        acc_ref[...] = jnp.zeros_like(acc_ref)

    acc_ref[0] += (jnp.dot(ce_ref[...], ae_ref[...], preferred_element_type=F32)
                   + jnp.dot(se_ref[...], be_ref[...], preferred_element_type=F32))
    acc_ref[1] += (jnp.dot(co_ref[...], ao_ref[...], preferred_element_type=F32)
                   + jnp.dot(so_ref[...], bo_ref[...], preferred_element_type=F32))

    @pl.when(kk == pl.num_programs(3) - 1)
    def _():
        o_ref[0] = (acc_ref[0] + acc_ref[1]).astype(o_ref.dtype)
        o_ref[1] = (acc_ref[0] - acc_ref[1]).astype(o_ref.dtype)


def _seqdft(ab_even, ab_odd, *, n, row0):
    half = n // 2
    tmm = min(half, 1024)
    tk = min(half, 512)
    tn = 1024
    nk = half // tk
    col_b = D_MODEL // tn
    mat = pl.BlockSpec((tmm, tk), lambda b, m, j, k: (m, k))
    a_spec = pl.BlockSpec((tk, tn), lambda b, m, j, k: (row0 // 2 // tk + b * nk + k, j))
    b_spec = pl.BlockSpec((tk, tn), lambda b, m, j, k: (row0 // 2 // tk + b * nk + k, col_b + j))
    out = pl.pallas_call(
        _seqdft_kernel,
        grid=(BATCH, half // tmm, D_MODEL // tn, nk),
        in_specs=[mat, mat, mat, mat, a_spec, b_spec, a_spec, b_spec],
        out_specs=pl.BlockSpec((2, tmm, tn), lambda b, m, j, k: (b, m, j)),
        out_shape=jax.ShapeDtypeStruct((BATCH * 2, half, D_MODEL), BF16),
        scratch_shapes=[pltpu.VMEM((2, tmm, tn), F32)],
        compiler_params=_params(("arbitrary",) * 4),
        name="seqdft",
    )(*_seq_dft_matrices(n, 0), *_seq_dft_matrices(n, 1), ab_even, ab_even, ab_odd, ab_odd)
    return out.reshape(BATCH * n, D_MODEL)


SGU_ROWS = 256


def _sgu_kernel(u_ref, v_ref, lg_ref, lb_ref, ws_ref, bs_ref, o_ref):
    v = v_ref[...].astype(F32)
    mu = jnp.mean(v, axis=-1, keepdims=True)
    vc = v - mu
    var = jnp.mean(vc * vc, axis=-1, keepdims=True)
    vn = (vc * lax.rsqrt(var + LN_EPS) * lg_ref[...] + lb_ref[...]).astype(BF16)
    we = GMLP_WIDTH // GMLP_HEADS
    for c in range(SGU_ROWS // GMLP_CHUNK):
        r0 = c * GMLP_CHUNK
        for h in range(GMLP_HEADS):
            s = jnp.dot(ws_ref[h], vn[r0:r0 + GMLP_CHUNK, h * we:(h + 1) * we], preferred_element_type=F32)
            s = s + bs_ref[:, h:h + 1]
            u = u_ref[r0:r0 + GMLP_CHUNK, h * we:(h + 1) * we].astype(F32)
            o_ref[r0:r0 + GMLP_CHUNK, h * we:(h + 1) * we] = (u * s).astype(o_ref.dtype)


def _sgu(z, ln_g, ln_b, w_s, b_s_t):
    return pl.pallas_call(
        _sgu_kernel,
        grid=(T_ALL // SGU_ROWS,),
        in_specs=[
            pl.BlockSpec((SGU_ROWS, GMLP_WIDTH), lambda i: (i, 0)),
            pl.BlockSpec((SGU_ROWS, GMLP_WIDTH), lambda i: (i, 1)),
            pl.BlockSpec((1, GMLP_WIDTH), lambda i: (0, 0)),
            pl.BlockSpec((1, GMLP_WIDTH), lambda i: (0, 0)),
            pl.BlockSpec((GMLP_HEADS, GMLP_CHUNK, GMLP_CHUNK), lambda i: (0, 0, 0)),
            pl.BlockSpec((GMLP_CHUNK, GMLP_HEADS), lambda i: (0, 0)),
        ],
        out_specs=pl.BlockSpec((SGU_ROWS, GMLP_WIDTH), lambda i: (i, 0)),
        out_shape=jax.ShapeDtypeStruct((T_ALL, GMLP_WIDTH), BF16),
        compiler_params=_params(("arbitrary",)),
        name="sgu",
    )(z, z, ln_g.reshape(1, -1), ln_b.reshape(1, -1), w_s, b_s_t)


def _dot_nt(a, b):
    return lax.dot_general(a, b, (((1,), (1,)), ((), ())), preferred_element_type=F32)


ATTN_KEY_CHUNK = 256
ATTN_TQ = 512
ATTN_Q_SCALE = (DIFF_HEAD_DIM ** -0.5) * math.log2(math.e)


def _attn_kernel(lam_ref, g_ref, q_ref, *rest, latent, lam_init):
    if latent:
        kx_ref, vx_ref, kc_ref, vc_ref, o_ref, s_ref = rest
        chunks = [(kx_ref, vx_ref, r) for r in range(0, SEQ, ATTN_KEY_CHUNK)]
    else:
        kc_ref, vc_ref, o_ref, s_ref = rest
        chunks = []
    chunks += [(kc_ref, vc_ref, r) for r in range(0, CTX_LEN, ATTN_KEY_CHUNK)]
    hd = DIFF_HEAD_DIM
    kc = ATTN_KEY_CHUNK
    lv = lam_ref[...]
    lam = (jnp.exp(jnp.sum(lv[0:1] * lv[1:2], axis=-1, keepdims=True))
           - jnp.exp(jnp.sum(lv[2:3] * lv[3:4], axis=-1, keepdims=True)) + lam_init)
    q = q_ref[...]
    comps = (0, 1)
    qs = [q[:, c * hd:(c + 1) * hd] for c in comps]
    m = [None, None]
    for j, (k_ref, _, r0) in enumerate(chunks):
        for c in comps:
            s = _dot_nt(qs[c], k_ref[r0:r0 + kc, c * hd:(c + 1) * hd])
            s_ref[c, :, j * kc:(j + 1) * kc] = s
            sm = jnp.maximum(s[:, :LANE], s[:, LANE:])
            m[c] = sm if m[c] is None else jnp.maximum(m[c], sm)
    m = [jnp.max(mc, axis=-1, keepdims=True) for mc in m]
    l = [None, None]
    acc = [None, None]

    def exp_chunk(j, c, row_max):
        e = jnp.exp2(s_ref[c, :, j * kc:(j + 1) * kc] - row_max)
        ep = e[:, :LANE] + e[:, LANE:]
        l[c] = ep if l[c] is None else l[c] + ep
        return e.astype(BF16)

    ahead = 4
    n_chunks = len(chunks)
    ready = {(j, c): exp_chunk(j, c, m[c]) for j in range(min(ahead, n_chunks)) for c in comps}
    for j, (_, v_ref, r0) in enumerate(chunks):
        for c in comps:
            pv = jnp.dot(ready.pop((j, c)), v_ref[r0:r0 + kc, :], preferred_element_type=F32)
            acc[c] = pv if acc[c] is None else acc[c] + pv
            if j + ahead < n_chunks:
                zero = jnp.minimum(jnp.abs(pv[-1:, :1]), 0.0)
                ready[j + ahead, c] = exp_chunk(j + ahead, c, m[c] + zero)
    outs = [acc[c] * (1.0 / jnp.sum(l[c], axis=-1, keepdims=True)) for c in comps]
    o = outs[0] - lam * outs[1]
    ms = jnp.mean(o * o, axis=-1, keepdims=True)
    o = o * lax.rsqrt(ms + NORM_EPS) * g_ref[...] * (1.0 - lam_init)
    o_ref[...] = o.astype(o_ref.dtype)


def _diff_attention(qkv, lam_vecs, subln_g, lam_init, *, latent):
    tq = ATTN_TQ if latent else CTX_LEN
    hw = 2 * DIFF_HEAD_DIM
    k_col = D_MODEL // hw
    v_col = 2 * D_MODEL // hw
    n_q = SEQ if latent else CTX_LEN
    q_row0 = 0 if latent else T_X
    nq = n_q // tq
    in_specs = [
        pl.BlockSpec((4, DIFF_HEAD_DIM), lambda b, h, i: (0, 0)),
        pl.BlockSpec((1, hw), lambda b, h, i: (0, 0)),
        pl.BlockSpec((tq, hw), lambda b, h, i: (q_row0 // tq + b * nq + i, h)),
    ]
    args = [lam_vecs, subln_g.reshape(1, hw), qkv]
    if latent:
        in_specs += [
            pl.BlockSpec((SEQ, hw), lambda b, h, i: (b, k_col + h)),
            pl.BlockSpec((SEQ, hw), lambda b, h, i: (b, v_col + h)),
        ]
        args += [qkv, qkv]
    in_specs += [
        pl.BlockSpec((CTX_LEN, hw), lambda b, h, i: (T_X // CTX_LEN + b, k_col + h)),
        pl.BlockSpec((CTX_LEN, hw), lambda b, h, i: (T_X // CTX_LEN + b, v_col + h)),
    ]
    args += [qkv, qkv]
    return pl.pallas_call(
        functools.partial(_attn_kernel, latent=latent, lam_init=lam_init),
        grid=(BATCH, DIFF_HEADS, nq),
        in_specs=in_specs,
        out_specs=pl.BlockSpec((tq, hw), lambda b, h, i: (b * nq + i, h)),
        out_shape=jax.ShapeDtypeStruct((BATCH * n_q, D_MODEL), BF16),
        scratch_shapes=[pltpu.VMEM((2, tq, (SEQ if latent else 0) + CTX_LEN), F32)],
        compiler_params=_params(("arbitrary",) * 3),
        name="diffattn_latent" if latent else "diffattn_ctx",
    )(*args)


GLA_CTX_CHUNKS = CTX_LEN // GLA_CHUNK
GLA_X_CHUNKS = SEQ // GLA_CHUNK
GLA_STEPS = GLA_CTX_CHUNKS + GLA_X_CHUNKS


def _split3_bf16(a):
    hi = a.astype(BF16)
    r = a - hi.astype(F32)
    mid = r.astype(BF16)
    lo = (r - mid.astype(F32)).astype(BF16)
    return hi, mid, lo


def _gla_scan_kernel(qf_ref, kf_ref, vf_ref, gf_ref, qb_ref, kb_ref, vb_ref, gb_ref, wup_ref, bg_ref, tri_ref,
                     of_ref, ob_ref, st_ref):
    @pl.when(pl.program_id(1) == 0)
    def _():
        st_ref[...] = jnp.zeros_like(st_ref)

    dk, dv = GLA_DK_HEAD, GLA_DV_HEAD
    dirs = (0, 1)
    chains = [(d, h) for h in range(GLA_HEADS) for d in dirs]
    q_refs, k_refs, v_refs, g_refs, o_refs = (qf_ref, qb_ref), (kf_ref, kb_ref), (vf_ref, vb_ref), (gf_ref, gb_ref), (of_ref, ob_ref)
    tri = [tri_ref[d] for d in dirs]
    mask = [t.astype(F32) > 0.0 for t in tri]
    z = [jnp.dot(g_refs[d][...], wup_ref[d], preferred_element_type=F32) + bg_ref[d] for d in dirs]
    g = [(jnp.minimum(zd, 0.0) - jnp.log(1.0 + jnp.exp(-jnp.abs(zd)))) * (1.0 / GLA_TAU) for zd in z]
    parts = [_split3_bf16(gd) for gd in g]
    bcum = [sum(jnp.dot(tri[d], part, preferred_element_type=F32) for part in parts[d]) for d in dirs]
    blast = [jnp.sum(gd, axis=0, keepdims=True) for gd in g]
    q = [q_refs[d][...].astype(F32) * (dk ** -0.5) for d in dirs]
    k = [k_refs[d][...].astype(F32) for d in dirs]
    q_dec = [(q[d] * jnp.exp(bcum[d])).astype(BF16) for d in dirs]
    k_inv = [(k[d] * jnp.exp(-bcum[d])).astype(BF16) for d in dirs]
    k_tail = [(k[d] * jnp.exp(blast[d] - bcum[d])).astype(BF16) for d in dirs]
    decay = [jnp.exp(blast[d]) for d in dirs]
    ks = lambda h: slice(h * dk, (h + 1) * dk)
    vs = lambda h: slice(h * dv, (h + 1) * dv)
    st = {c: st_ref[c[0], c[1]] for c in chains}
    att = {(d, h): _dot_nt(q_dec[d][:, ks(h)], k_inv[d][:, ks(h)]) for d, h in chains}
    inter = {(d, h): _dot_nt(q_dec[d][:, ks(h)], st[d, h].astype(BF16)) for d, h in chains}
    att = {(d, h): jnp.where(mask[d], att[d, h], 0.0).astype(BF16) for d, h in chains}
    v = {(d, h): v_refs[d][:, vs(h)] for d, h in chains}
    upd = {c: lax.dot_general(v[c], k_tail[c[0]][:, ks(c[1])], (((0,), (0,)), ((), ())),
                              preferred_element_type=F32) for c in chains}
    out = {c: inter[c] + jnp.dot(att[c], v[c], preferred_element_type=F32) for c in chains}
    for d, h in chains:
        o_refs[d][:, vs(h)] = out[d, h].astype(o_refs[d].dtype)
        st_ref[d, h] = decay[d][:, ks(h)] * st[d, h] + upd[d, h]


def _gla_scan(proj, gates, wup_pad, b_gate, tri):
    def row_block(d):
        def index(b, s):
            c_idx = s if d == 0 else GLA_CTX_CHUNKS - 1 - s
            x_idx = s - GLA_CTX_CHUNKS if d == 0 else GLA_STEPS - 1 - s
            return jnp.where(s < GLA_CTX_CHUNKS,
                             T_X // GLA_CHUNK + b * GLA_CTX_CHUNKS + c_idx,
                             b * GLA_X_CHUNKS + x_idx)
        return index

    def chunk_specs(d):
        rb = row_block(d)
        return [
            pl.BlockSpec((GLA_CHUNK, GLA_DK), lambda b, s: (rb(b, s), 0)),
            pl.BlockSpec((GLA_CHUNK, GLA_DK), lambda b, s: (rb(b, s), 1)),
            pl.BlockSpec((GLA_CHUNK, GLA_DV), lambda b, s: (rb(b, s), 2 * GLA_DK // GLA_DV)),
            pl.BlockSpec((GLA_CHUNK, LANE), lambda b, s: (rb(b, s), 0)),
        ]

    whole = lambda shape: pl.BlockSpec(shape, lambda b, s: (0,) * len(shape))
    out_sds = jax.ShapeDtypeStruct((T_ALL, GLA_DV), BF16)
    return pl.pallas_call(
        _gla_scan_kernel,
        grid=(BATCH, GLA_STEPS),
        in_specs=chunk_specs(0) + chunk_specs(1) + [
            whole((2, LANE, GLA_DK)), whole((2, 1, GLA_DK)), whole((2, GLA_CHUNK, GLA_CHUNK))],
        out_specs=[pl.BlockSpec((GLA_CHUNK, GLA_DV), lambda b, s: (row_block(0)(b, s), 0)),
                   pl.BlockSpec((GLA_CHUNK, GLA_DV), lambda b, s: (row_block(1)(b, s), 0))],
        out_shape=[out_sds, out_sds],
        scratch_shapes=[pltpu.VMEM((2, GLA_HEADS, GLA_DV_HEAD, GLA_DK_HEAD), F32)],
        compiler_params=_params(("arbitrary",) * 2),
        name="gla_scan",
    )(proj, proj, proj, gates, proj, proj, proj, gates, wup_pad, b_gate, tri)


def _rope_tables():
    hd = DIFF_HEAD_DIM
    pos = np.arange(SEQ)
    row = (pos // GRID_W).astype(np.float32)
    col = (pos % GRID_W).astype(np.float32)
    n_freq = hd // 4
    inv = (ROPE_THETA ** (-np.arange(n_freq, dtype=np.float32) / n_freq)).astype(np.float32)
    ang = np.concatenate([row[:, None] * inv, col[:, None] * inv], axis=-1).astype(np.float64)
    cos, sin = np.cos(ang), np.sin(ang)
    cs = np.concatenate([cos, cos], axis=-1).astype(np.float32)
    sn = np.concatenate([-sin, sin], axis=-1).astype(np.float32)
    return jnp.asarray(cs), jnp.asarray(sn)


def _gla_tri():
    lower = np.tril(np.ones((GLA_CHUNK, GLA_CHUNK), np.float32))
    return jnp.asarray(np.stack([lower, lower.T])).astype(BF16)


def _mixer(i, tok, mod, g_mix, w):
    kind = i % N_MIXERS
    if kind == 0:
        ab_even, ab_odd = _chdft(tok, g_mix, mod, _channel_dft_matrix())
        zx = _seqdft(ab_even, ab_odd, n=SEQ, row0=0)
        zc = _seqdft(ab_even, ab_odd, n=CTX_LEN, row0=T_X)
        return (zx, zc), w["fno_w_out"]
    if kind == 1:
        z = _modmm(tok, g_mix, mod, 0, 1, w["gmlp_w_in"].astype(BF16), epilogue="gelu")
        mixed = _sgu(z, w["gmlp_ln_g"], w["gmlp_ln_b"], w["gmlp_w_s"].astype(BF16), w["gmlp_b_s"].T)
        return mixed, w["gmlp_w_out"]
    if kind == 2:
        lam_init = 0.8 - 0.6 * math.exp(-0.3 * i)
        qkv = _modmm(tok, g_mix, mod, 0, 1, w["diff_w_in"].astype(BF16), epilogue="rope", rope=_rope_tables())
        ox = _diff_attention(qkv, w["diff_lambda"], w["diff_subln_g"], lam_init, latent=True)
        oc = _diff_attention(qkv, w["diff_lambda"], w["diff_subln_g"], lam_init, latent=False)
        return (ox, oc), w["diff_w_out"]
    n_main = 2 * GLA_DK + 2 * GLA_DV
    w_in = w["gla_w_in"]
    proj = _modmm(tok, g_mix, mod, 0, 1, w_in.astype(BF16), n_out=n_main)
    w_gate = jnp.pad(w_in[:, n_main:], ((0, 0), (0, LANE - 2 * GLA_GATE_RANK))).astype(BF16)
    gates = _modmm(tok, g_mix, mod, 0, 1, w_gate)
    wup = w["gla_w_gate_up"]
    wup_pad = jnp.zeros((2, LANE, GLA_DK), F32)
    wup_pad = wup_pad.at[0, :GLA_GATE_RANK].set(wup[0])
    wup_pad = wup_pad.at[1, GLA_GATE_RANK:2 * GLA_GATE_RANK].set(wup[1])
    o_fwd, o_bwd = _gla_scan(proj, gates, wup_pad.astype(BF16), w["gla_b_gate"].reshape(2, 1, GLA_DK), _gla_tri())
    return (o_fwd, o_bwd, proj, w["gla_norm_g"]), w["gla_w_out"]


def _layer(i, tok, mod, w, ffn_w, g_final=None):
    last = i == DEPTH - 1
    mixed, w_out = _mixer(i, tok, mod, w["g_norm_mix"], w)
    joint = _resmm(mixed, w_out.astype(BF16), tok, mod, 2, n_tiles=NT_X if last else NT_ALL,
                   gla=i % N_MIXERS == 3)
    return _ffn(joint, w["g_norm_ffn"], mod, *ffn_w, i, with_context=not last, g_final=g_final)


def kernel(x, c, ctx, c_ctx, w_ada, b_ada, g_norm_mix, g_norm_ffn, w_ffn_in, w_ffn_out, g_final, fno_w_out, gmlp_w_in, gmlp_ln_g, gmlp_ln_b, gmlp_w_s, gmlp_b_s, gmlp_w_out, diff_w_in, diff_lambda, diff_subln_g, diff_w_out, gla_w_in, gla_w_gate_up, gla_b_gate, gla_norm_g, gla_w_out):
    tok = (x.reshape(T_X, D_MODEL), ctx.reshape(T_C, D_MODEL))
    c_rows = jnp.concatenate([c, c_ctx[None, :], jnp.zeros((MOD_ROWS - BATCH - 1, D_MODEL), F32)], axis=0)
    mod_all = _ada_table(c_rows, w_ada, b_ada).reshape(DEPTH, MOD_ROWS * N_ADA, 1, D_MODEL)
    mixer_weights = (
        dict(fno_w_out=fno_w_out),
        dict(gmlp_w_in=gmlp_w_in, gmlp_ln_g=gmlp_ln_g, gmlp_ln_b=gmlp_ln_b, gmlp_w_s=gmlp_w_s,
             gmlp_b_s=gmlp_b_s, gmlp_w_out=gmlp_w_out),
        dict(diff_w_in=diff_w_in, diff_lambda=diff_lambda, diff_subln_g=diff_subln_g, diff_w_out=diff_w_out),
        dict(gla_w_in=gla_w_in, gla_w_gate_up=gla_w_gate_up, gla_b_gate=gla_b_gate, gla_norm_g=gla_norm_g,
             gla_w_out=gla_w_out),
    )
    ffn_w = (w_ffn_in, w_ffn_out)
    for i in range(DEPTH):
        kind, j = i % N_MIXERS, i // N_MIXERS
        w = {name: arr[j] for name, arr in mixer_weights[kind].items()}
        w.update(g_norm_mix=g_norm_mix[i], g_norm_ffn=g_norm_ffn[i])
        tok = _layer(i, tok, mod_all[i], w, ffn_w, g_final=g_final if i == DEPTH - 1 else None)
    return tok.reshape(BATCH, SEQ, D_MODEL)
```

```python
import functools
import math

import numpy as np
import jax
import jax.numpy as jnp
from jax import lax
from jax.experimental import pallas as pl
from jax.experimental.pallas import tpu as pltpu

D_MODEL = 2048
BATCH = 2
SEQ = 4096
DEPTH = 4
GRID_W = 64
CTX_LEN = 256
N_MIXERS = 4
NORM_EPS = 1e-6
D_FF = 4 * D_MODEL
N_ADA = 6
FNO_GROUPS = 4
GMLP_CHUNK = 128
GMLP_HEADS = 16
GMLP_WIDTH = D_MODEL
DIFF_HEADS = 8
DIFF_HEAD_DIM = D_MODEL // DIFF_HEADS // 2
ROPE_THETA = 10000.0
GLA_HEADS = 4
GLA_DK = D_MODEL // 2
GLA_DV = D_MODEL
GLA_DK_HEAD = GLA_DK // GLA_HEADS
GLA_DV_HEAD = GLA_DV // GLA_HEADS
GLA_GATE_RANK = 16
GLA_TAU = 16.0
GLA_CHUNK = 64
LN_EPS = 1e-5

T_X = BATCH * SEQ
T_C = BATCH * CTX_LEN
T_ALL = T_X + T_C
TM = 512
NT_X = T_X // TM
NT_ALL = T_ALL // TM
TILES_PER_BATCH = SEQ // TM
MOD_ROWS = 16
CTX_GROUP = BATCH
LANE = 128
MM_TN = 2048
FFN_TF = 1024
FFN_TF_FINAL = 512
FFN_TM = 1024

BF16 = jnp.bfloat16
F32 = jnp.float32
VMEM_LIMIT = 56 * 1024 * 1024


def _params(semantics):
    return pltpu.CompilerParams(dimension_semantics=semantics, vmem_limit_bytes=VMEM_LIMIT)


def _mod_spec(slot, tile_rows=TM, first_tile=0):
    def index(i, *_):
        group = jnp.minimum((i + first_tile) // (SEQ // tile_rows), CTX_GROUP)
        return group * N_ADA + slot, 0, 0
    return pl.BlockSpec((1, 1, D_MODEL), index)


def _modulated(x, g, shift, scale):
    ms = jnp.mean(x * x, axis=-1, keepdims=True)
    y = x * lax.rsqrt(ms + NORM_EPS) * g
    return y * (1.0 + scale) + shift


def _ada_kernel(c_ref, w_ref, b_ref, o_ref):
    c = c_ref[...]
    s = (c * jax.nn.sigmoid(c)).astype(BF16)
    w = w_ref[0].astype(BF16)
    o_ref[0] = jnp.dot(s, w, preferred_element_type=F32) + b_ref[0]


def _ada_table(c8, w_ada, b_ada):
    tn = 1024
    n_out = N_ADA * D_MODEL
    return pl.pallas_call(
        _ada_kernel,
        grid=(DEPTH, n_out // tn),
        in_specs=[
            pl.BlockSpec((MOD_ROWS, D_MODEL), lambda l, j: (0, 0)),
            pl.BlockSpec((1, D_MODEL, tn), lambda l, j: (l, 0, j)),
            pl.BlockSpec((1, 1, tn), lambda l, j: (l, 0, j)),
        ],
        out_specs=pl.BlockSpec((1, MOD_ROWS, tn), lambda l, j: (l, 0, j)),
        out_shape=jax.ShapeDtypeStruct((DEPTH, MOD_ROWS, n_out), F32),
        compiler_params=_params(("arbitrary", "arbitrary")),
        name="ada_table",
    )(c8, w_ada, b_ada.reshape(DEPTH, 1, n_out))


def _gelu_exact(a):
    return 0.5 * a * (1.0 + lax.erf(a * (1.0 / math.sqrt(2.0))))


def _modmm_kernel(*refs, n_x, epilogue, tn):
    x_refs, (g_ref, sh_ref, sc_ref, w_ref, *rest) = refs[:n_x], refs[n_x:]
    if epilogue == "rope":
        cs_ref, sn_ref, o_ref, h_ref = rest
    else:
        o_ref, h_ref = rest
    i = pl.program_id(0)
    j = pl.program_id(1)

    def finish(acc):
        if epilogue == "gelu":
            o_ref[...] = _gelu_exact(acc).astype(o_ref.dtype)
        elif epilogue == "rope":
            acc = acc * jnp.where(j < D_MODEL // tn, ATTN_Q_SCALE, 1.0)
            rotate = jnp.logical_and(j < (2 * D_MODEL) // tn, i < NT_X)

            @pl.when(rotate)
            def _():
                cs = cs_ref[...]
                sn = sn_ref[...]
                for blk in range(tn // LANE):
                    a = acc[:, blk * LANE:(blk + 1) * LANE]
                    swapped = pltpu.roll(a, LANE // 2, 1)
                    o_ref[:, blk * LANE:(blk + 1) * LANE] = (a * cs + swapped * sn).astype(o_ref.dtype)

            @pl.when(jnp.logical_not(rotate))
            def _():
                o_ref[...] = acc.astype(o_ref.dtype)
        else:
            o_ref[...] = acc.astype(o_ref.dtype)

    @pl.when(j == 0)
    def _():
        h = _modulated(_stream_tile(x_refs), g_ref[...], sh_ref[0], sc_ref[0]).astype(BF16)
        h_ref[...] = h
        finish(jnp.dot(h, w_ref[...], preferred_element_type=F32))

    @pl.when(j > 0)
    def _():
        finish(jnp.dot(h_ref[...], w_ref[...], preferred_element_type=F32))


def _modmm(x, g, mod, slot_shift, slot_scale, w, *, n_out=None, epilogue="plain", rope=None, out_dtype=BF16):
    k = w.shape[0]
    n_out = w.shape[1] if n_out is None else n_out
    tn = min(MM_TN, n_out)
    x_specs, x_args = _stream_operand(x, TM, k, NT_ALL)
    in_specs = x_specs + [
        pl.BlockSpec((1, k), lambda i, j: (0, 0)),
        _mod_spec(slot_shift),
        _mod_spec(slot_scale),
        pl.BlockSpec((k, tn), lambda i, j: (0, j)),
    ]
    args = x_args + [g.reshape(1, k), mod, mod, w]
    if epilogue == "rope":
        pos = lambda i, j: (jnp.where(i < NT_X, i % TILES_PER_BATCH, 0), 0)
        in_specs += [pl.BlockSpec((TM, LANE), pos), pl.BlockSpec((TM, LANE), pos)]
        args += list(rope)
    return pl.pallas_call(
        functools.partial(_modmm_kernel, n_x=len(x_args), epilogue=epilogue, tn=tn),
        grid=(NT_ALL, n_out // tn),
        in_specs=in_specs,
        out_specs=pl.BlockSpec((TM, tn), lambda i, j: (i, j)),
        out_shape=jax.ShapeDtypeStruct((T_ALL, n_out), out_dtype),
        scratch_shapes=[pltpu.VMEM((TM, k), BF16)],
        compiler_params=_params(("arbitrary", "arbitrary")),
        name="modmm_" + epilogue,
    )(*args)


def _stream_operand(arr, rows, width, n_tiles):
    if isinstance(arr, tuple) and n_tiles == NT_X:
        arr = arr[0]
    if isinstance(arr, tuple):
        assert arr[1].shape == (rows, width)
        return ([pl.BlockSpec((rows, width), lambda i, *_: (jnp.minimum(i, NT_X - 1), 0)),
                 pl.BlockSpec((rows, width), lambda i, *_: (0, 0))], list(arr))
    return [pl.BlockSpec((rows, width), lambda i, *_: (i, 0))], [arr]


def _stream_tile(refs):
    if len(refs) == 2:
        return jnp.where(pl.program_id(0) < NT_X, refs[0][...], refs[1][...])
    return refs[0][...]


def _gla_gated(of_ref, ob_ref, r_ref, g_ref):
    o = of_ref[...].astype(F32) + ob_ref[...].astype(F32)
    r = r_ref[...].astype(F32)
    gate = r * jax.nn.sigmoid(r)
    dv = GLA_DV_HEAD
    heads = []
    for h in range(GLA_HEADS):
        oh = o[:, h * dv:(h + 1) * dv]
        ms = jnp.mean(oh * oh, axis=-1, keepdims=True)
        heads.append((oh * lax.rsqrt(ms + NORM_EPS) * g_ref[...] * gate[:, h * dv:(h + 1) * dv]).astype(BF16))
    return jnp.concatenate(heads, axis=-1)


def _resmm_kernel(*refs, n_a, n_res, gla):
    a_refs, refs = refs[:n_a], refs[n_a:]
    w_ref, refs = refs[0], refs[1:]
    res_refs, (gate_ref, o_ref) = refs[:n_res], refs[n_res:]
    a = _gla_gated(*a_refs) if gla else _stream_tile(a_refs)
    acc = jnp.dot(a, w_ref[...], preferred_element_type=F32)
    o_ref[...] = _stream_tile(res_refs) + gate_ref[0] * acc


def _resmm(a, w, res, mod, slot_gate, *, n_tiles, gla=False):
    k, n_out = w.shape
    if gla:
        o_fwd, o_bwd, proj, norm_g = a
        r_col = (2 * GLA_DK + GLA_DV) // GLA_DV
        a_specs = [pl.BlockSpec((TM, GLA_DV), lambda i: (i, 0)),
                   pl.BlockSpec((TM, GLA_DV), lambda i: (i, 0)),
                   pl.BlockSpec((TM, GLA_DV), lambda i: (i, r_col)),
                   pl.BlockSpec((1, GLA_DV_HEAD), lambda i: (0, 0))]
        a_args = [o_fwd, o_bwd, proj, norm_g.reshape(1, -1)]
    else:
        a_specs, a_args = _stream_operand(a, TM, k, n_tiles)
    res_specs, res_args = _stream_operand(res, TM, n_out, n_tiles)
    return pl.pallas_call(
        functools.partial(_resmm_kernel, n_a=len(a_args), n_res=len(res_args), gla=gla),
        grid=(n_tiles,),
        in_specs=a_specs + [pl.BlockSpec((k, n_out), lambda i: (0, 0))] + res_specs + [_mod_spec(slot_gate)],
        out_specs=pl.BlockSpec((TM, n_out), lambda i: (i, 0)),
        out_shape=jax.ShapeDtypeStruct((n_tiles * TM, n_out), F32),
        compiler_params=_params(("arbitrary",)),
        name="resmm_gla" if gla else "resmm",
    )(*a_args, w, *res_args, mod)


def _ffn_kernel(x_ref, g_ref, sh_ref, sc_ref, gate_ref, w1_ref, w2_ref, *rest, final_norm, cast_next):
    rest = list(rest)
    gf_ref = rest.pop(0) if final_norm else None
    next_in = [rest.pop(0), rest.pop(0)] if cast_next else []
    o_ref = rest.pop(0)
    next_out = [rest.pop(0), rest.pop(0)] if cast_next else []
    (h_ref,) = rest
    f = pl.program_id(1)
    for src, dst in zip(next_in, next_out):
        dst[...] = src[0].astype(BF16)

    def hidden_step(h):
        a = jnp.dot(h, w1_ref[...], preferred_element_type=F32)
        a = jnp.square(jnp.maximum(a, 0.0)).astype(BF16)
        return jnp.dot(a, w2_ref[...], preferred_element_type=F32)

    @pl.when(f == 0)
    def _():
        h = _modulated(x_ref[...], g_ref[...], sh_ref[0], sc_ref[0]).astype(BF16)
        h_ref[...] = h
        o_ref[...] = hidden_step(h)

    @pl.when(f > 0)
    def _():
        o_ref[...] += hidden_step(h_ref[...])

    @pl.when(f == pl.num_programs(1) - 1)
    def _():
        y = x_ref[...] + gate_ref[0] * o_ref[...]
        if final_norm:
            ms = jnp.mean(y * y, axis=-1, keepdims=True)
            y = y * lax.rsqrt(ms + NORM_EPS) * gf_ref[...]
        o_ref[...] = y


def _ffn_part(x, g, mod, w1, w2, *, tile_rows, first_tile, n_tiles, g_final=None, cast_next=None):
    final_norm = g_final is not None
    tf = FFN_TF_FINAL if final_norm else FFN_TF
    nf = D_FF // tf
    single = pl.Buffered(1)
    in_specs = [
        pl.BlockSpec((tile_rows, D_MODEL), lambda i, f: (i + first_tile, 0)),
        pl.BlockSpec((1, D_MODEL), lambda i, f: (0, 0)),
        _mod_spec(3, tile_rows, first_tile),
        _mod_spec(4, tile_rows, first_tile),
        _mod_spec(5, tile_rows, first_tile),
        pl.BlockSpec((D_MODEL, tf), lambda i, f: (0, f)),
        pl.BlockSpec((tf, D_MODEL), lambda i, f: (f, 0)),
    ]
    args = [x, g.reshape(1, D_MODEL), mod, mod, mod, w1, w2]
    out_specs = [pl.BlockSpec((tile_rows, D_MODEL), lambda i, f: (i, 0), pipeline_mode=single)]
    out_shape = [jax.ShapeDtypeStruct((n_tiles * tile_rows, D_MODEL), F32)]
    if final_norm:
        in_specs.append(pl.BlockSpec((1, D_MODEL), lambda i, f: (0, 0)))
        args.append(g_final.reshape(1, D_MODEL))
    if cast_next is not None:
        w_in_all, w_out_all, layer = cast_next
        steps = n_tiles * nf
        for w_all in (w_in_all, w_out_all):
            rows, cols = w_all.shape[1:]
            slab = rows // steps
            assert slab * steps == rows and slab % 16 == 0
            in_specs.append(pl.BlockSpec((1, slab, cols), lambda i, f: (layer, i * nf + f, 0)))
            args.append(w_all)
            out_specs.append(pl.BlockSpec((slab, cols), lambda i, f: (i * nf + f, 0)))
            out_shape.append(jax.ShapeDtypeStruct((rows, cols), BF16))
    results = pl.pallas_call(
        functools.partial(_ffn_kernel, final_norm=final_norm, cast_next=cast_next is not None),
        grid=(n_tiles, nf),
        in_specs=in_specs,
        out_specs=out_specs,
        out_shape=out_shape,
        scratch_shapes=[pltpu.VMEM((tile_rows, D_MODEL), BF16)],
        compiler_params=_params(("arbitrary", "arbitrary")),
        name="ffn_final" if final_norm else "ffn",
    )(*args)
    return results[0] if cast_next is None else (results[0], (results[1], results[2]))


def _ffn(x, g, mod, w1, w2, *, with_context, g_final=None, cast_next=None):
    latent = _ffn_part(x, g, mod, w1, w2, tile_rows=FFN_TM, first_tile=0, n_tiles=T_X // FFN_TM,
                       g_final=g_final, cast_next=cast_next)
    latent, next_w = latent if cast_next is not None else (latent, None)
    context = None
    if with_context:
        context = _ffn_part(x, g, mod, w1, w2, tile_rows=T_C, first_tile=T_X // T_C, n_tiles=1)
    return latent, context, next_w


def _dft_tables(n):
    idx = np.arange(n, dtype=np.int64)
    ang = 2.0 * np.pi * ((idx[:, None] * idx[None, :]) % n).astype(np.float64) / n
    return np.cos(ang) / math.sqrt(n), np.sin(ang) / math.sqrt(n)


def _channel_dft_matrix():
    c, s = _dft_tables(D_MODEL // FNO_GROUPS)
    return jnp.asarray(np.concatenate([c, s], axis=1), F32).astype(BF16)


def _chdft_kernel(*refs, n_x):
    x_refs, (g_ref, sh_ref, sc_ref, w_ref, sel_ref, oe_ref, oo_ref) = refs[:n_x], refs[n_x:]
    dg = D_MODEL // FNO_GROUPS
    w = w_ref[...]
    h_all = _modulated(_stream_tile(x_refs), g_ref[...], sh_ref[0], sc_ref[0]).astype(BF16)
    for parity, o_ref in enumerate((oe_ref, oo_ref)):
        h = jnp.dot(sel_ref[parity], h_all, preferred_element_type=F32).astype(BF16)
        for grp in range(FNO_GROUPS):
            cols = slice(grp * dg, (grp + 1) * dg)
            r = jnp.dot(h[:, cols], w, preferred_element_type=F32)
            o_ref[:, cols] = r[:, :dg].astype(o_ref.dtype)
            o_ref[:, D_MODEL + grp * dg:D_MODEL + (grp + 1) * dg] = r[:, dg:].astype(o_ref.dtype)


def _chdft(x, g, mod, w):
    dg = D_MODEL // FNO_GROUPS
    half = jax.ShapeDtypeStruct((T_ALL // 2, 2 * D_MODEL), BF16)
    x_specs, x_args = _stream_operand(x, TM, D_MODEL, NT_ALL)
    rows = np.arange(TM // 2)
    sel = np.zeros((2, TM // 2, TM), np.float32)
    sel[0, rows, 2 * rows] = 1.0
    sel[1, rows, 2 * rows + 1] = 1.0
    return pl.pallas_call(
        functools.partial(_chdft_kernel, n_x=len(x_args)),
        grid=(NT_ALL,),
        in_specs=x_specs + [
            pl.BlockSpec((1, D_MODEL), lambda i: (0, 0)),
            _mod_spec(0),
            _mod_spec(1),
            pl.BlockSpec((dg, 2 * dg), lambda i: (0, 0)),
            pl.BlockSpec((2, TM // 2, TM), lambda i: (0, 0, 0)),
        ],
        out_specs=[pl.BlockSpec((TM // 2, 2 * D_MODEL), lambda i: (i, 0))] * 2,
        out_shape=[half, half],
        compiler_params=_params(("arbitrary",)),
        name="chdft",
    )(*x_args, g.reshape(1, D_MODEL), mod, mod, w, jnp.asarray(sel).astype(BF16))


def _seq_dft_matrices(n, parity):
    half = n // 2
    kk = 2 * np.arange(half, dtype=np.int64) + parity
    if n <= 512:
        ang = 2.0 * np.pi * ((np.arange(half, dtype=np.int64)[:, None] * kk[None, :]) % n) / n
        return (jnp.asarray(np.cos(ang) / math.sqrt(n), F32).astype(BF16),
                jnp.asarray(-np.sin(ang) / math.sqrt(n), F32).astype(BF16))
    r = 64
    q = half // r
    ang_hi = 2.0 * np.pi * (((r * np.arange(q, dtype=np.int64))[:, None] * kk[None, :]) % n) / n
    ang_lo = 2.0 * np.pi * ((np.arange(r, dtype=np.int64)[:, None] * kk[None, :]) % n) / n
    ch = jnp.asarray(np.cos(ang_hi) / math.sqrt(n), F32)[:, None, :]
    sh = jnp.asarray(np.sin(ang_hi) / math.sqrt(n), F32)[:, None, :]
    cl = jnp.asarray(np.cos(ang_lo), F32)[None, :, :]
    sl = jnp.asarray(np.sin(ang_lo), F32)[None, :, :]
    c = (ch * cl - sh * sl).astype(BF16).reshape(half, half)
    s_neg = (-(sh * cl + ch * sl)).astype(BF16).reshape(half, half)
    return c, s_neg


def _seqdft_kernel(ce_ref, se_ref, co_ref, so_ref, ae_ref, be_ref, ao_ref, bo_ref, o_ref, acc_ref):
    kk = pl.program_id(3)

    @pl.when(kk == 0)
    def _():
        acc_ref[...] = jnp.zeros_like(acc_ref)

    acc_ref[0] += (jnp.dot(ce_ref[...], ae_ref[...], preferred_element_type=F32)
                   + jnp.dot(se_ref[...], be_ref[...], preferred_element_type=F32))
    acc_ref[1] += (jnp.dot(co_ref[...], ao_ref[...], preferred_element_type=F32)
                   + jnp.dot(so_ref[...], bo_ref[...], preferred_element_type=F32))

    @pl.when(kk == pl.num_programs(3) - 1)
    def _():
        o_ref[0] = (acc_ref[0] + acc_ref[1]).astype(o_ref.dtype)
        o_ref[1] = (acc_ref[0] - acc_ref[1]).astype(o_ref.dtype)


def _seqdft(ab_even, ab_odd, *, n, row0):
    half = n // 2
    tmm = min(half, 1024)
    tk = min(half, 512)
    tn = 1024
    nk = half // tk
    col_b = D_MODEL // tn
    mat = pl.BlockSpec((tmm, tk), lambda b, m, j, k: (m, k))
    a_spec = pl.BlockSpec((tk, tn), lambda b, m, j, k: (row0 // 2 // tk + b * nk + k, j))
    b_spec = pl.BlockSpec((tk, tn), lambda b, m, j, k: (row0 // 2 // tk + b * nk + k, col_b + j))
    out = pl.pallas_call(
        _seqdft_kernel,
        grid=(BATCH, half // tmm, D_MODEL // tn, nk),
        in_specs=[mat, mat, mat, mat, a_spec, b_spec, a_spec, b_spec],
        out_specs=pl.BlockSpec((2, tmm, tn), lambda b, m, j, k: (b, m, j)),
        out_shape=jax.ShapeDtypeStruct((BATCH * 2, half, D_MODEL), BF16),
        scratch_shapes=[pltpu.VMEM((2, tmm, tn), F32)],
        compiler_params=_params(("arbitrary",) * 4),
        name="seqdft",
    )(*_seq_dft_matrices(n, 0), *_seq_dft_matrices(n, 1), ab_even, ab_even, ab_odd, ab_odd)
    return out.reshape(BATCH * n, D_MODEL)


SGU_ROWS = 256


def _sgu_kernel(u_ref, v_ref, lg_ref, lb_ref, ws_ref, bs_ref, o_ref):
    v = v_ref[...].astype(F32)
    mu = jnp.mean(v, axis=-1, keepdims=True)
    vc = v - mu
    var = jnp.mean(vc * vc, axis=-1, keepdims=True)
    vn = (vc * lax.rsqrt(var + LN_EPS) * lg_ref[...] + lb_ref[...]).astype(BF16)
    we = GMLP_WIDTH // GMLP_HEADS
    for c in range(SGU_ROWS // GMLP_CHUNK):
        r0 = c * GMLP_CHUNK
        for h in range(GMLP_HEADS):
            s = jnp.dot(ws_ref[h], vn[r0:r0 + GMLP_CHUNK, h * we:(h + 1) * we], preferred_element_type=F32)
            s = s + bs_ref[:, h:h + 1]
            u = u_ref[r0:r0 + GMLP_CHUNK, h * we:(h + 1) * we].astype(F32)
            o_ref[r0:r0 + GMLP_CHUNK, h * we:(h + 1) * we] = (u * s).astype(o_ref.dtype)


def _sgu(z, ln_g, ln_b, w_s, b_s_t):
    return pl.pallas_call(
        _sgu_kernel,
        grid=(T_ALL // SGU_ROWS,),
        in_specs=[
            pl.BlockSpec((SGU_ROWS, GMLP_WIDTH), lambda i: (i, 0)),
            pl.BlockSpec((SGU_ROWS, GMLP_WIDTH), lambda i: (i, 1)),
            pl.BlockSpec((1, GMLP_WIDTH), lambda i: (0, 0)),
            pl.BlockSpec((1, GMLP_WIDTH), lambda i: (0, 0)),
            pl.BlockSpec((GMLP_HEADS, GMLP_CHUNK, GMLP_CHUNK), lambda i: (0, 0, 0)),
            pl.BlockSpec((GMLP_CHUNK, GMLP_HEADS), lambda i: (0, 0)),
        ],
        out_specs=pl.BlockSpec((SGU_ROWS, GMLP_WIDTH), lambda i: (i, 0)),
        out_shape=jax.ShapeDtypeStruct((T_ALL, GMLP_WIDTH), BF16),
        compiler_params=_params(("arbitrary",)),
        name="sgu",
    )(z, z, ln_g.reshape(1, -1), ln_b.reshape(1, -1), w_s, b_s_t)


def _dot_nt(a, b):
    return lax.dot_general(a, b, (((1,), (1,)), ((), ())), preferred_element_type=F32)


ATTN_KEY_CHUNK = 256
ATTN_TQ = 512
ATTN_Q_SCALE = (DIFF_HEAD_DIM ** -0.5) * math.log2(math.e)


def _attn_kernel(lam_ref, g_ref, q_ref, *rest, latent, lam_init):
    if latent:
        kx_ref, vx_ref, kc_ref, vc_ref, o_ref, s_ref = rest
        chunks = [(kx_ref, vx_ref, r) for r in range(0, SEQ, ATTN_KEY_CHUNK)]
    else:
        kc_ref, vc_ref, o_ref, s_ref = rest
        chunks = []
    chunks += [(kc_ref, vc_ref, r) for r in range(0, CTX_LEN, ATTN_KEY_CHUNK)]
    hd = DIFF_HEAD_DIM
    kc = ATTN_KEY_CHUNK
    lv = lam_ref[...]
    lam = (jnp.exp(jnp.sum(lv[0:1] * lv[1:2], axis=-1, keepdims=True))
           - jnp.exp(jnp.sum(lv[2:3] * lv[3:4], axis=-1, keepdims=True)) + lam_init)
    q = q_ref[...]
    comps = (0, 1)
    qs = [q[:, c * hd:(c + 1) * hd] for c in comps]
    m = [None, None]
    for j, (k_ref, _, r0) in enumerate(chunks):
        for c in comps:
            s = _dot_nt(qs[c], k_ref[r0:r0 + kc, c * hd:(c + 1) * hd])
            s_ref[c, :, j * kc:(j + 1) * kc] = s
            sm = jnp.maximum(s[:, :LANE], s[:, LANE:])
            m[c] = sm if m[c] is None else jnp.maximum(m[c], sm)
    m = [jnp.max(mc, axis=-1, keepdims=True) for mc in m]
    l = [None, None]
    acc = [None, None]

    def exp_chunk(j, c, row_max):
        e = jnp.exp2(s_ref[c, :, j * kc:(j + 1) * kc] - row_max)
        ep = e[:, :LANE] + e[:, LANE:]
        l[c] = ep if l[c] is None else l[c] + ep
        return e.astype(BF16)

    ahead = 4
    n_chunks = len(chunks)
    ready = {(j, c): exp_chunk(j, c, m[c]) for j in range(min(ahead, n_chunks)) for c in comps}
    for j, (_, v_ref, r0) in enumerate(chunks):
        for c in comps:
            pv = jnp.dot(ready.pop((j, c)), v_ref[r0:r0 + kc, :], preferred_element_type=F32)
            acc[c] = pv if acc[c] is None else acc[c] + pv
            if j + ahead < n_chunks:
                zero = jnp.minimum(jnp.abs(pv[-1:, :1]), 0.0)
                ready[j + ahead, c] = exp_chunk(j + ahead, c, m[c] + zero)
    outs = [acc[c] * (1.0 / jnp.sum(l[c], axis=-1, keepdims=True)) for c in comps]
    o = outs[0] - lam * outs[1]
    ms = jnp.mean(o * o, axis=-1, keepdims=True)
    o = o * lax.rsqrt(ms + NORM_EPS) * g_ref[...] * (1.0 - lam_init)
    o_ref[...] = o.astype(o_ref.dtype)


def _diff_attention(qkv, lam_vecs, subln_g, lam_init, *, latent):
    tq = ATTN_TQ if latent else CTX_LEN
    hw = 2 * DIFF_HEAD_DIM
    k_col = D_MODEL // hw
    v_col = 2 * D_MODEL // hw
    n_q = SEQ if latent else CTX_LEN
    q_row0 = 0 if latent else T_X
    nq = n_q // tq
    in_specs = [
        pl.BlockSpec((4, DIFF_HEAD_DIM), lambda b, h, i: (0, 0)),
        pl.BlockSpec((1, hw), lambda b, h, i: (0, 0)),
        pl.BlockSpec((tq, hw), lambda b, h, i: (q_row0 // tq + b * nq + i, h)),
    ]
    args = [lam_vecs, subln_g.reshape(1, hw), qkv]
    if latent:
        in_specs += [
            pl.BlockSpec((SEQ, hw), lambda b, h, i: (b, k_col + h)),
            pl.BlockSpec((SEQ, hw), lambda b, h, i: (b, v_col + h)),
        ]
        args += [qkv, qkv]
    in_specs += [
        pl.BlockSpec((CTX_LEN, hw), lambda b, h, i: (T_X // CTX_LEN + b, k_col + h)),
        pl.BlockSpec((CTX_LEN, hw), lambda b, h, i: (T_X // CTX_LEN + b, v_col + h)),
    ]
    args += [qkv, qkv]
    return pl.pallas_call(
        functools.partial(_attn_kernel, latent=latent, lam_init=lam_init),
        grid=(BATCH, DIFF_HEADS, nq),
        in_specs=in_specs,
        out_specs=pl.BlockSpec((tq, hw), lambda b, h, i: (b * nq + i, h)),
        out_shape=jax.ShapeDtypeStruct((BATCH * n_q, D_MODEL), BF16),
        scratch_shapes=[pltpu.VMEM((2, tq, (SEQ if latent else 0) + CTX_LEN), F32)],
        compiler_params=_params(("arbitrary",) * 3),
        name="diffattn_latent" if latent else "diffattn_ctx",
    )(*args)


GLA_CTX_CHUNKS = CTX_LEN // GLA_CHUNK
GLA_X_CHUNKS = SEQ // GLA_CHUNK
GLA_STEPS = GLA_CTX_CHUNKS + GLA_X_CHUNKS


def _split3_bf16(a):
    hi = a.astype(BF16)
    r = a - hi.astype(F32)
    mid = r.astype(BF16)
    lo = (r - mid.astype(F32)).astype(BF16)
    return hi, mid, lo


def _gla_scan_kernel(qf_ref, kf_ref, vf_ref, gf_ref, qb_ref, kb_ref, vb_ref, gb_ref, wup_ref, bg_ref, tri_ref,
                     of_ref, ob_ref, st_ref):
    @pl.when(pl.program_id(1) == 0)
    def _():
        st_ref[...] = jnp.zeros_like(st_ref)

    dk, dv = GLA_DK_HEAD, GLA_DV_HEAD
    dirs = (0, 1)
    chains = [(d, h) for h in range(GLA_HEADS) for d in dirs]
    q_refs, k_refs, v_refs, g_refs, o_refs = (qf_ref, qb_ref), (kf_ref, kb_ref), (vf_ref, vb_ref), (gf_ref, gb_ref), (of_ref, ob_ref)
    tri = [tri_ref[d] for d in dirs]
    mask = [t.astype(F32) > 0.0 for t in tri]
    z = [jnp.dot(g_refs[d][...], wup_ref[d], preferred_element_type=F32) + bg_ref[d] for d in dirs]
    g = [(jnp.minimum(zd, 0.0) - jnp.log(1.0 + jnp.exp(-jnp.abs(zd)))) * (1.0 / GLA_TAU) for zd in z]
    parts = [_split3_bf16(gd) for gd in g]
    bcum = [sum(jnp.dot(tri[d], part, preferred_element_type=F32) for part in parts[d]) for d in dirs]
    blast = [jnp.sum(gd, axis=0, keepdims=True) for gd in g]
    q = [q_refs[d][...].astype(F32) * (dk ** -0.5) for d in dirs]
    k = [k_refs[d][...].astype(F32) for d in dirs]
    q_dec = [(q[d] * jnp.exp(bcum[d])).astype(BF16) for d in dirs]
    k_inv = [(k[d] * jnp.exp(-bcum[d])).astype(BF16) for d in dirs]
    k_tail = [(k[d] * jnp.exp(blast[d] - bcum[d])).astype(BF16) for d in dirs]
    decay = [jnp.exp(blast[d]) for d in dirs]
    ks = lambda h: slice(h * dk, (h + 1) * dk)
    vs = lambda h: slice(h * dv, (h + 1) * dv)
    st = {c: st_ref[c[0], c[1]] for c in chains}
    att = {(d, h): _dot_nt(q_dec[d][:, ks(h)], k_inv[d][:, ks(h)]) for d, h in chains}
    inter = {(d, h): _dot_nt(q_dec[d][:, ks(h)], st[d, h].astype(BF16)) for d, h in chains}
    att = {(d, h): jnp.where(mask[d], att[d, h], 0.0).astype(BF16) for d, h in chains}
    v = {(d, h): v_refs[d][:, vs(h)] for d, h in chains}
    upd = {c: lax.dot_general(v[c], k_tail[c[0]][:, ks(c[1])], (((0,), (0,)), ((), ())),
                              preferred_element_type=F32) for c in chains}
    out = {c: inter[c] + jnp.dot(att[c], v[c], preferred_element_type=F32) for c in chains}
    for d, h in chains:
        o_refs[d][:, vs(h)] = out[d, h].astype(o_refs[d].dtype)
        st_ref[d, h] = decay[d][:, ks(h)] * st[d, h] + upd[d, h]


def _gla_scan(proj, gates, wup_pad, b_gate, tri):
    def row_block(d):
        def index(b, s):
            c_idx = s if d == 0 else GLA_CTX_CHUNKS - 1 - s
            x_idx = s - GLA_CTX_CHUNKS if d == 0 else GLA_STEPS - 1 - s
            return jnp.where(s < GLA_CTX_CHUNKS,
                             T_X // GLA_CHUNK + b * GLA_CTX_CHUNKS + c_idx,
                             b * GLA_X_CHUNKS + x_idx)
        return index

    def chunk_specs(d):
        rb = row_block(d)
        return [
            pl.BlockSpec((GLA_CHUNK, GLA_DK), lambda b, s: (rb(b, s), 0)),
            pl.BlockSpec((GLA_CHUNK, GLA_DK), lambda b, s: (rb(b, s), 1)),
            pl.BlockSpec((GLA_CHUNK, GLA_DV), lambda b, s: (rb(b, s), 2 * GLA_DK // GLA_DV)),
            pl.BlockSpec((GLA_CHUNK, LANE), lambda b, s: (rb(b, s), 0)),
        ]

    whole = lambda shape: pl.BlockSpec(shape, lambda b, s: (0,) * len(shape))
    out_sds = jax.ShapeDtypeStruct((T_ALL, GLA_DV), BF16)
    return pl.pallas_call(
        _gla_scan_kernel,
        grid=(BATCH, GLA_STEPS),
        in_specs=chunk_specs(0) + chunk_specs(1) + [
            whole((2, LANE, GLA_DK)), whole((2, 1, GLA_DK)), whole((2, GLA_CHUNK, GLA_CHUNK))],
        out_specs=[pl.BlockSpec((GLA_CHUNK, GLA_DV), lambda b, s: (row_block(0)(b, s), 0)),
                   pl.BlockSpec((GLA_CHUNK, GLA_DV), lambda b, s: (row_block(1)(b, s), 0))],
        out_shape=[out_sds, out_sds],
        scratch_shapes=[pltpu.VMEM((2, GLA_HEADS, GLA_DV_HEAD, GLA_DK_HEAD), F32)],
        compiler_params=_params(("arbitrary",) * 2),
        name="gla_scan",
    )(proj, proj, proj, gates, proj, proj, proj, gates, wup_pad, b_gate, tri)


def _rope_tables():
    hd = DIFF_HEAD_DIM
    pos = np.arange(SEQ)
    row = (pos // GRID_W).astype(np.float32)
    col = (pos % GRID_W).astype(np.float32)
    n_freq = hd // 4
    inv = (ROPE_THETA ** (-np.arange(n_freq, dtype=np.float32) / n_freq)).astype(np.float32)
    ang = np.concatenate([row[:, None] * inv, col[:, None] * inv], axis=-1).astype(np.float64)
    cos, sin = np.cos(ang), np.sin(ang)
    cs = np.concatenate([cos, cos], axis=-1).astype(np.float32)
    sn = np.concatenate([-sin, sin], axis=-1).astype(np.float32)
    return jnp.asarray(cs), jnp.asarray(sn)


def _gla_tri():
    lower = np.tril(np.ones((GLA_CHUNK, GLA_CHUNK), np.float32))
    return jnp.asarray(np.stack([lower, lower.T])).astype(BF16)


def _mixer(i, tok, mod, g_mix, w):
    kind = i % N_MIXERS
    if kind == 0:
        ab_even, ab_odd = _chdft(tok, g_mix, mod, _channel_dft_matrix())
        zx = _seqdft(ab_even, ab_odd, n=SEQ, row0=0)
        zc = _seqdft(ab_even, ab_odd, n=CTX_LEN, row0=T_X)
        return (zx, zc), w["fno_w_out"]
    if kind == 1:
        z = _modmm(tok, g_mix, mod, 0, 1, w["gmlp_w_in"].astype(BF16), epilogue="gelu")
        mixed = _sgu(z, w["gmlp_ln_g"], w["gmlp_ln_b"], w["gmlp_w_s"].astype(BF16), w["gmlp_b_s"].T)
        return mixed, w["gmlp_w_out"]
    if kind == 2:
        lam_init = 0.8 - 0.6 * math.exp(-0.3 * i)
        qkv = _modmm(tok, g_mix, mod, 0, 1, w["diff_w_in"].astype(BF16), epilogue="rope", rope=_rope_tables())
        ox = _diff_attention(qkv, w["diff_lambda"], w["diff_subln_g"], lam_init, latent=True)
        oc = _diff_attention(qkv, w["diff_lambda"], w["diff_subln_g"], lam_init, latent=False)
        return (ox, oc), w["diff_w_out"]
    n_main = 2 * GLA_DK + 2 * GLA_DV
    w_in = w["gla_w_in"]
    proj = _modmm(tok, g_mix, mod, 0, 1, w_in.astype(BF16), n_out=n_main)
    w_gate = jnp.pad(w_in[:, n_main:], ((0, 0), (0, LANE - 2 * GLA_GATE_RANK))).astype(BF16)
    gates = _modmm(tok, g_mix, mod, 0, 1, w_gate)
    wup = w["gla_w_gate_up"]
    wup_pad = jnp.zeros((2, LANE, GLA_DK), F32)
    wup_pad = wup_pad.at[0, :GLA_GATE_RANK].set(wup[0])
    wup_pad = wup_pad.at[1, GLA_GATE_RANK:2 * GLA_GATE_RANK].set(wup[1])
    o_fwd, o_bwd = _gla_scan(proj, gates, wup_pad.astype(BF16), w["gla_b_gate"].reshape(2, 1, GLA_DK), _gla_tri())
    return (o_fwd, o_bwd, proj, w["gla_norm_g"]), w["gla_w_out"]


def _layer(i, tok, mod, w, ffn_w, cast_next=None, g_final=None):
    last = i == DEPTH - 1
    mixed, w_out = _mixer(i, tok, mod, w["g_norm_mix"], w)
    joint = _resmm(mixed, w_out.astype(BF16), tok, mod, 2, n_tiles=NT_X if last else NT_ALL,
                   gla=i % N_MIXERS == 3)
    return _ffn(joint, w["g_norm_ffn"], mod, *ffn_w, with_context=not last, g_final=g_final, cast_next=cast_next)


def kernel(x, c, ctx, c_ctx, w_ada, b_ada, g_norm_mix, g_norm_ffn, w_ffn_in, w_ffn_out, g_final, fno_w_out, gmlp_w_in, gmlp_ln_g, gmlp_ln_b, gmlp_w_s, gmlp_b_s, gmlp_w_out, diff_w_in, diff_lambda, diff_subln_g, diff_w_out, gla_w_in, gla_w_gate_up, gla_b_gate, gla_norm_g, gla_w_out):
    tok = (x.reshape(T_X, D_MODEL), ctx.reshape(T_C, D_MODEL))
    c_rows = jnp.concatenate([c, c_ctx[None, :], jnp.zeros((MOD_ROWS - BATCH - 1, D_MODEL), F32)], axis=0)
    mod_all = _ada_table(c_rows, w_ada, b_ada).reshape(DEPTH, MOD_ROWS * N_ADA, 1, D_MODEL)
    mixer_weights = (
        dict(fno_w_out=fno_w_out),
        dict(gmlp_w_in=gmlp_w_in, gmlp_ln_g=gmlp_ln_g, gmlp_ln_b=gmlp_ln_b, gmlp_w_s=gmlp_w_s,
             gmlp_b_s=gmlp_b_s, gmlp_w_out=gmlp_w_out),
        dict(diff_w_in=diff_w_in, diff_lambda=diff_lambda, diff_subln_g=diff_subln_g, diff_w_out=diff_w_out),
        dict(gla_w_in=gla_w_in, gla_w_gate_up=gla_w_gate_up, gla_b_gate=gla_b_gate, gla_norm_g=gla_norm_g,
             gla_w_out=gla_w_out),
    )
    ffn_w = (w_ffn_in[0].astype(BF16), w_ffn_out[0].astype(BF16))
    for i in range(DEPTH):
        kind, j = i % N_MIXERS, i // N_MIXERS
        last = i == DEPTH - 1
        w = {name: arr[j] for name, arr in mixer_weights[kind].items()}
        w.update(g_norm_mix=g_norm_mix[i], g_norm_ffn=g_norm_ffn[i])
        latent, context, ffn_w = _layer(i, tok, mod_all[i], w, ffn_w,
                                        cast_next=None if last else (w_ffn_in, w_ffn_out, i + 1),
                                        g_final=g_final if last else None)
        tok = (latent, context)
    return latent.reshape(BATCH, SEQ, D_MODEL)
```

```python
import functools
import math

import numpy as np
import jax
import jax.numpy as jnp
from jax import lax
from jax.experimental import pallas as pl
from jax.experimental.pallas import tpu as pltpu

D_MODEL = 2048
BATCH = 2
SEQ = 4096
DEPTH = 4
GRID_W = 64
CTX_LEN = 256
N_MIXERS = 4
NORM_EPS = 1e-6
D_FF = 4 * D_MODEL
N_ADA = 6
FNO_GROUPS = 4
GMLP_CHUNK = 128
GMLP_HEADS = 16
GMLP_WIDTH = D_MODEL
DIFF_HEADS = 8
DIFF_HEAD_DIM = D_MODEL // DIFF_HEADS // 2
ROPE_THETA = 10000.0
GLA_HEADS = 4
GLA_DK = D_MODEL // 2
GLA_DV = D_MODEL
GLA_DK_HEAD = GLA_DK // GLA_HEADS
GLA_DV_HEAD = GLA_DV // GLA_HEADS
GLA_GATE_RANK = 16
GLA_TAU = 16.0
GLA_CHUNK = 64
LN_EPS = 1e-5

T_X = BATCH * SEQ
T_C = BATCH * CTX_LEN
T_ALL = T_X + T_C
TM = 512
NT_X = T_X // TM
NT_ALL = T_ALL // TM
TILES_PER_BATCH = SEQ // TM
MOD_ROWS = 16
CTX_GROUP = BATCH
LANE = 128
MM_TN = 2048
FFN_TF = 1024
FFN_TF_FINAL = 512
FFN_TM = 1024

BF16 = jnp.bfloat16
F32 = jnp.float32
VMEM_LIMIT = 56 * 1024 * 1024
FFN_VMEM_LIMIT = 60 * 1024 * 1024


def _params(semantics, vmem_limit=VMEM_LIMIT):
    return pltpu.CompilerParams(dimension_semantics=semantics, vmem_limit_bytes=vmem_limit)


def _mod_spec(slot, tile_rows=TM, first_tile=0):
    def index(i, *_):
        group = jnp.minimum((i + first_tile) // (SEQ // tile_rows), CTX_GROUP)
        return group * N_ADA + slot, 0, 0
    return pl.BlockSpec((1, 1, D_MODEL), index)


def _modulated(x, g, shift, scale):
    ms = jnp.mean(x * x, axis=-1, keepdims=True)
    y = x * lax.rsqrt(ms + NORM_EPS) * g
    return y * (1.0 + scale) + shift


def _ada_kernel(c_ref, w_ref, b_ref, o_ref):
    c = c_ref[...]
    s = (c * jax.nn.sigmoid(c)).astype(BF16)
    w = w_ref[0].astype(BF16)
    o_ref[0] = jnp.dot(s, w, preferred_element_type=F32) + b_ref[0]


def _ada_table(c8, w_ada, b_ada):
    tn = 1024
    n_out = N_ADA * D_MODEL
    return pl.pallas_call(
        _ada_kernel,
        grid=(DEPTH, n_out // tn),
        in_specs=[
            pl.BlockSpec((MOD_ROWS, D_MODEL), lambda l, j: (0, 0)),
            pl.BlockSpec((1, D_MODEL, tn), lambda l, j: (l, 0, j)),
            pl.BlockSpec((1, 1, tn), lambda l, j: (l, 0, j)),
        ],
        out_specs=pl.BlockSpec((1, MOD_ROWS, tn), lambda l, j: (l, 0, j)),
        out_shape=jax.ShapeDtypeStruct((DEPTH, MOD_ROWS, n_out), F32),
        compiler_params=_params(("arbitrary", "arbitrary")),
        name="ada_table",
    )(c8, w_ada, b_ada.reshape(DEPTH, 1, n_out))


def _gelu_exact(a):
    return 0.5 * a * (1.0 + lax.erf(a * (1.0 / math.sqrt(2.0))))


def _modmm_kernel(*refs, n_x, epilogue, tn):
    x_refs, (g_ref, sh_ref, sc_ref, w_ref, *rest) = refs[:n_x], refs[n_x:]
    if epilogue == "rope":
        cs_ref, sn_ref, o_ref, h_ref = rest
    else:
        o_ref, h_ref = rest
    i = pl.program_id(0)
    j = pl.program_id(1)

    def finish(acc):
        if epilogue == "gelu":
            o_ref[...] = _gelu_exact(acc).astype(o_ref.dtype)
        elif epilogue == "rope":
            acc = acc * jnp.where(j < D_MODEL // tn, ATTN_Q_SCALE, 1.0)
            rotate = jnp.logical_and(j < (2 * D_MODEL) // tn, i < NT_X)

            @pl.when(rotate)
            def _():
                cs = cs_ref[...]
                sn = sn_ref[...]
                for blk in range(tn // LANE):
                    a = acc[:, blk * LANE:(blk + 1) * LANE]
                    swapped = pltpu.roll(a, LANE // 2, 1)
                    o_ref[:, blk * LANE:(blk + 1) * LANE] = (a * cs + swapped * sn).astype(o_ref.dtype)

            @pl.when(jnp.logical_not(rotate))
            def _():
                o_ref[...] = acc.astype(o_ref.dtype)
        else:
            o_ref[...] = acc.astype(o_ref.dtype)

    @pl.when(j == 0)
    def _():
        h = _modulated(_stream_tile(x_refs), g_ref[...], sh_ref[0], sc_ref[0]).astype(BF16)
        h_ref[...] = h
        finish(jnp.dot(h, w_ref[...], preferred_element_type=F32))

    @pl.when(j > 0)
    def _():
        finish(jnp.dot(h_ref[...], w_ref[...], preferred_element_type=F32))


def _modmm(x, g, mod, slot_shift, slot_scale, w, *, n_out=None, epilogue="plain", rope=None, out_dtype=BF16):
    k = w.shape[0]
    n_out = w.shape[1] if n_out is None else n_out
    tn = min(MM_TN, n_out)
    x_specs, x_args = _stream_operand(x, TM, k, NT_ALL)
    in_specs = x_specs + [
        pl.BlockSpec((1, k), lambda i, j: (0, 0)),
        _mod_spec(slot_shift),
        _mod_spec(slot_scale),
        pl.BlockSpec((k, tn), lambda i, j: (0, j)),
    ]
    args = x_args + [g.reshape(1, k), mod, mod, w]
    if epilogue == "rope":
        pos = lambda i, j: (jnp.where(i < NT_X, i % TILES_PER_BATCH, 0), 0)
        in_specs += [pl.BlockSpec((TM, LANE), pos), pl.BlockSpec((TM, LANE), pos)]
        args += list(rope)
    return pl.pallas_call(
        functools.partial(_modmm_kernel, n_x=len(x_args), epilogue=epilogue, tn=tn),
        grid=(NT_ALL, n_out // tn),
        in_specs=in_specs,
        out_specs=pl.BlockSpec((TM, tn), lambda i, j: (i, j)),
        out_shape=jax.ShapeDtypeStruct((T_ALL, n_out), out_dtype),
        scratch_shapes=[pltpu.VMEM((TM, k), BF16)],
        compiler_params=_params(("arbitrary", "arbitrary")),
        name="modmm_" + epilogue,
    )(*args)


def _stream_operand(arr, rows, width, n_tiles):
    if isinstance(arr, tuple) and n_tiles == NT_X:
        arr = arr[0]
    if isinstance(arr, tuple):
        assert arr[1].shape == (rows, width)
        return ([pl.BlockSpec((rows, width), lambda i, *_: (jnp.minimum(i, NT_X - 1), 0)),
                 pl.BlockSpec((rows, width), lambda i, *_: (0, 0))], list(arr))
    return [pl.BlockSpec((rows, width), lambda i, *_: (i, 0))], [arr]


def _stream_tile(refs):
    if len(refs) == 2:
        return jnp.where(pl.program_id(0) < NT_X, refs[0][...], refs[1][...])
    return refs[0][...]


def _gla_gated(of_ref, ob_ref, r_ref, g_ref):
    o = of_ref[...].astype(F32) + ob_ref[...].astype(F32)
    r = r_ref[...].astype(F32)
    gate = r * jax.nn.sigmoid(r)
    dv = GLA_DV_HEAD
    heads = []
    for h in range(GLA_HEADS):
        oh = o[:, h * dv:(h + 1) * dv]
        ms = jnp.mean(oh * oh, axis=-1, keepdims=True)
        heads.append((oh * lax.rsqrt(ms + NORM_EPS) * g_ref[...] * gate[:, h * dv:(h + 1) * dv]).astype(BF16))
    return jnp.concatenate(heads, axis=-1)


def _resmm_kernel(*refs, n_a, n_res, gla):
    a_refs, refs = refs[:n_a], refs[n_a:]
    w_ref, refs = refs[0], refs[1:]
    res_refs, (gate_ref, o_ref) = refs[:n_res], refs[n_res:]
    a = _gla_gated(*a_refs) if gla else _stream_tile(a_refs)
    acc = jnp.dot(a, w_ref[...], preferred_element_type=F32)
    o_ref[...] = _stream_tile(res_refs) + gate_ref[0] * acc


def _resmm(a, w, res, mod, slot_gate, *, n_tiles, gla=False):
    k, n_out = w.shape
    if gla:
        o_fwd, o_bwd, proj, norm_g = a
        r_col = (2 * GLA_DK + GLA_DV) // GLA_DV
        a_specs = [pl.BlockSpec((TM, GLA_DV), lambda i: (i, 0)),
                   pl.BlockSpec((TM, GLA_DV), lambda i: (i, 0)),
                   pl.BlockSpec((TM, GLA_DV), lambda i: (i, r_col)),
                   pl.BlockSpec((1, GLA_DV_HEAD), lambda i: (0, 0))]
        a_args = [o_fwd, o_bwd, proj, norm_g.reshape(1, -1)]
    else:
        a_specs, a_args = _stream_operand(a, TM, k, n_tiles)
    res_specs, res_args = _stream_operand(res, TM, n_out, n_tiles)
    return pl.pallas_call(
        functools.partial(_resmm_kernel, n_a=len(a_args), n_res=len(res_args), gla=gla),
        grid=(n_tiles,),
        in_specs=a_specs + [pl.BlockSpec((k, n_out), lambda i: (0, 0))] + res_specs + [_mod_spec(slot_gate)],
        out_specs=pl.BlockSpec((TM, n_out), lambda i: (i, 0)),
        out_shape=jax.ShapeDtypeStruct((n_tiles * TM, n_out), F32),
        compiler_params=_params(("arbitrary",)),
        name="resmm_gla" if gla else "resmm",
    )(*a_args, w, *res_args, mod)


def _ffn_kernel(x_ref, g_ref, sh_ref, sc_ref, gate_ref, w1_ref, w2_ref, *rest, final_norm, n_casts):
    rest = list(rest)
    gf_ref = rest.pop(0) if final_norm else None
    next_in = [rest.pop(0) for _ in range(n_casts)]
    o_ref = rest.pop(0)
    next_out = [rest.pop(0) for _ in range(n_casts)]
    (h_ref,) = rest
    f = pl.program_id(1)
    for src, dst in zip(next_in, next_out):
        dst[...] = src[0].astype(BF16)

    def hidden_step(h):
        a = jnp.dot(h, w1_ref[...], preferred_element_type=F32)
        a = jnp.square(jnp.maximum(a, 0.0)).astype(BF16)
        return jnp.dot(a, w2_ref[...], preferred_element_type=F32)

    @pl.when(f == 0)
    def _():
        h = _modulated(x_ref[...], g_ref[...], sh_ref[0], sc_ref[0]).astype(BF16)
        h_ref[...] = h
        o_ref[...] = hidden_step(h)

    @pl.when(f > 0)
    def _():
        o_ref[...] += hidden_step(h_ref[...])

    @pl.when(f == pl.num_programs(1) - 1)
    def _():
        y = x_ref[...] + gate_ref[0] * o_ref[...]
        if final_norm:
            ms = jnp.mean(y * y, axis=-1, keepdims=True)
            y = y * lax.rsqrt(ms + NORM_EPS) * gf_ref[...]
        o_ref[...] = y


def _ffn_part(x, g, mod, w1, w2, *, tile_rows, first_tile, n_tiles, g_final=None, cast_next=()):
    final_norm = g_final is not None
    tf = FFN_TF_FINAL if final_norm else FFN_TF
    nf = D_FF // tf
    single = pl.Buffered(1)
    in_specs = [
        pl.BlockSpec((tile_rows, D_MODEL), lambda i, f: (i + first_tile, 0)),
        pl.BlockSpec((1, D_MODEL), lambda i, f: (0, 0)),
        _mod_spec(3, tile_rows, first_tile),
        _mod_spec(4, tile_rows, first_tile),
        _mod_spec(5, tile_rows, first_tile),
        pl.BlockSpec((D_MODEL, tf), lambda i, f: (0, f)),
        pl.BlockSpec((tf, D_MODEL), lambda i, f: (f, 0)),
    ]
    args = [x, g.reshape(1, D_MODEL), mod, mod, mod, w1, w2]
    out_specs = [pl.BlockSpec((tile_rows, D_MODEL), lambda i, f: (i, 0), pipeline_mode=single)]
    out_shape = [jax.ShapeDtypeStruct((n_tiles * tile_rows, D_MODEL), F32)]
    if final_norm:
        in_specs.append(pl.BlockSpec((1, D_MODEL), lambda i, f: (0, 0)))
        args.append(g_final.reshape(1, D_MODEL))
    steps = n_tiles * nf
    for w_all, layer in cast_next:
        rows, cols = w_all.shape[1:]
        slab = rows // steps
        assert slab * steps == rows and slab % 16 == 0
        in_specs.append(pl.BlockSpec((1, slab, cols), lambda i, f, layer=layer: (layer, i * nf + f, 0)))
        args.append(w_all)
        out_specs.append(pl.BlockSpec((slab, cols), lambda i, f: (i * nf + f, 0)))
        out_shape.append(jax.ShapeDtypeStruct((rows, cols), BF16))
    results = pl.pallas_call(
        functools.partial(_ffn_kernel, final_norm=final_norm, n_casts=len(cast_next)),
        grid=(n_tiles, nf),
        in_specs=in_specs,
        out_specs=out_specs,
        out_shape=out_shape,
        scratch_shapes=[pltpu.VMEM((tile_rows, D_MODEL), BF16)],
        compiler_params=_params(("arbitrary", "arbitrary"), FFN_VMEM_LIMIT),
        name="ffn_final" if final_norm else "ffn",
    )(*args)
    return results[0], list(results[1:])


def _ffn(x, g, mod, w1, w2, *, with_context, g_final=None, cast_next=()):
    latent, casts = _ffn_part(x, g, mod, w1, w2, tile_rows=FFN_TM, first_tile=0, n_tiles=T_X // FFN_TM,
                              g_final=g_final, cast_next=cast_next)
    context = None
    if with_context:
        context, _ = _ffn_part(x, g, mod, w1, w2, tile_rows=T_C, first_tile=T_X // T_C, n_tiles=1)
    return latent, context, casts


def _dft_tables(n):
    idx = np.arange(n, dtype=np.int64)
    ang = 2.0 * np.pi * ((idx[:, None] * idx[None, :]) % n).astype(np.float64) / n
    return np.cos(ang) / math.sqrt(n), np.sin(ang) / math.sqrt(n)


def _channel_dft_matrix():
    c, s = _dft_tables(D_MODEL // FNO_GROUPS)
    return jnp.asarray(np.concatenate([c, s], axis=1), F32).astype(BF16)


def _chdft_kernel(*refs, n_x):
    x_refs, (g_ref, sh_ref, sc_ref, w_ref, sel_ref, oe_ref, oo_ref) = refs[:n_x], refs[n_x:]
    dg = D_MODEL // FNO_GROUPS
    w = w_ref[...]
    h_all = _modulated(_stream_tile(x_refs), g_ref[...], sh_ref[0], sc_ref[0]).astype(BF16)
    for parity, o_ref in enumerate((oe_ref, oo_ref)):
        h = jnp.dot(sel_ref[parity], h_all, preferred_element_type=F32).astype(BF16)
        for grp in range(FNO_GROUPS):
            cols = slice(grp * dg, (grp + 1) * dg)
            r = jnp.dot(h[:, cols], w, preferred_element_type=F32)
            o_ref[:, cols] = r[:, :dg].astype(o_ref.dtype)
            o_ref[:, D_MODEL + grp * dg:D_MODEL + (grp + 1) * dg] = r[:, dg:].astype(o_ref.dtype)


def _chdft(x, g, mod, w):
    dg = D_MODEL // FNO_GROUPS
    half = jax.ShapeDtypeStruct((T_ALL // 2, 2 * D_MODEL), BF16)
    x_specs, x_args = _stream_operand(x, TM, D_MODEL, NT_ALL)
    rows = np.arange(TM // 2)
    sel = np.zeros((2, TM // 2, TM), np.float32)
    sel[0, rows, 2 * rows] = 1.0
    sel[1, rows, 2 * rows + 1] = 1.0
    return pl.pallas_call(
        functools.partial(_chdft_kernel, n_x=len(x_args)),
        grid=(NT_ALL,),
        in_specs=x_specs + [
            pl.BlockSpec((1, D_MODEL), lambda i: (0, 0)),
            _mod_spec(0),
            _mod_spec(1),
            pl.BlockSpec((dg, 2 * dg), lambda i: (0, 0)),
            pl.BlockSpec((2, TM // 2, TM), lambda i: (0, 0, 0)),
        ],
        out_specs=[pl.BlockSpec((TM // 2, 2 * D_MODEL), lambda i: (i, 0))] * 2,
        out_shape=[half, half],
        compiler_params=_params(("arbitrary",)),
        name="chdft",
    )(*x_args, g.reshape(1, D_MODEL), mod, mod, w, jnp.asarray(sel).astype(BF16))


def _seq_dft_matrices(n, parity):
    half = n // 2
    kk = 2 * np.arange(half, dtype=np.int64) + parity
    if n <= 512:
        ang = 2.0 * np.pi * ((np.arange(half, dtype=np.int64)[:, None] * kk[None, :]) % n) / n
        return (jnp.asarray(np.cos(ang) / math.sqrt(n), F32).astype(BF16),
                jnp.asarray(-np.sin(ang) / math.sqrt(n), F32).astype(BF16))
    r = 64
    q = half // r
    ang_hi = 2.0 * np.pi * (((r * np.arange(q, dtype=np.int64))[:, None] * kk[None, :]) % n) / n
    ang_lo = 2.0 * np.pi * ((np.arange(r, dtype=np.int64)[:, None] * kk[None, :]) % n) / n
    ch = jnp.asarray(np.cos(ang_hi) / math.sqrt(n), F32)[:, None, :]
    sh = jnp.asarray(np.sin(ang_hi) / math.sqrt(n), F32)[:, None, :]
    cl = jnp.asarray(np.cos(ang_lo), F32)[None, :, :]
    sl = jnp.asarray(np.sin(ang_lo), F32)[None, :, :]
    c = (ch * cl - sh * sl).astype(BF16).reshape(half, half)
    s_neg = (-(sh * cl + ch * sl)).astype(BF16).reshape(half, half)
    return c, s_neg


def _seqdft_kernel(ce_ref, se_ref, co_ref, so_ref, ae_ref, be_ref, ao_ref, bo_ref, o_ref, acc_ref):
    kk = pl.program_id(3)

    @pl.when(kk == 0)
    def _():
        acc_ref[...] = jnp.zeros_like(acc_ref)

    acc_ref[0] += (jnp.dot(ce_ref[...], ae_ref[...], preferred_element_type=F32)
                   + jnp.dot(se_ref[...], be_ref[...], preferred_element_type=F32))
    acc_ref[1] += (jnp.dot(co_ref[...], ao_ref[...], preferred_element_type=F32)
                   + jnp.dot(so_ref[...], bo_ref[...], preferred_element_type=F32))

    @pl.when(kk == pl.num_programs(3) - 1)
    def _():
        o_ref[0] = (acc_ref[0] + acc_ref[1]).astype(o_ref.dtype)
        o_ref[1] = (acc_ref[0] - acc_ref[1]).astype(o_ref.dtype)


def _seqdft(ab_even, ab_odd, *, n, row0):
    half = n // 2
    tmm = min(half, 1024)
    tk = min(half, 512)
    tn = 1024
    nk = half // tk
    col_b = D_MODEL // tn
    mat = pl.BlockSpec((tmm, tk), lambda b, m, j, k: (m, k))
    a_spec = pl.BlockSpec((tk, tn), lambda b, m, j, k: (row0 // 2 // tk + b * nk + k, j))
    b_spec = pl.BlockSpec((tk, tn), lambda b, m, j, k: (row0 // 2 // tk + b * nk + k, col_b + j))
    out = pl.pallas_call(
        _seqdft_kernel,
        grid=(BATCH, half // tmm, D_MODEL // tn, nk),
        in_specs=[mat, mat, mat, mat, a_spec, b_spec, a_spec, b_spec],
        out_specs=pl.BlockSpec((2, tmm, tn), lambda b, m, j, k: (b, m, j)),
        out_shape=jax.ShapeDtypeStruct((BATCH * 2, half, D_MODEL), BF16),
        scratch_shapes=[pltpu.VMEM((2, tmm, tn), F32)],
        compiler_params=_params(("arbitrary",) * 4),
        name="seqdft",
    )(*_seq_dft_matrices(n, 0), *_seq_dft_matrices(n, 1), ab_even, ab_even, ab_odd, ab_odd)
    return out.reshape(BATCH * n, D_MODEL)


SGU_ROWS = 256


def _sgu_kernel(u_ref, v_ref, lg_ref, lb_ref, ws_ref, bs_ref, o_ref):
    v = v_ref[...].astype(F32)
    mu = jnp.mean(v, axis=-1, keepdims=True)
    vc = v - mu
    var = jnp.mean(vc * vc, axis=-1, keepdims=True)
    vn = (vc * lax.rsqrt(var + LN_EPS) * lg_ref[...] + lb_ref[...]).astype(BF16)
    we = GMLP_WIDTH // GMLP_HEADS
    for c in range(SGU_ROWS // GMLP_CHUNK):
        r0 = c * GMLP_CHUNK
        for h in range(GMLP_HEADS):
            s = jnp.dot(ws_ref[h], vn[r0:r0 + GMLP_CHUNK, h * we:(h + 1) * we], preferred_element_type=F32)
            s = s + bs_ref[:, h:h + 1]
            u = u_ref[r0:r0 + GMLP_CHUNK, h * we:(h + 1) * we].astype(F32)
            o_ref[r0:r0 + GMLP_CHUNK, h * we:(h + 1) * we] = (u * s).astype(o_ref.dtype)


def _sgu(z, ln_g, ln_b, w_s, b_s_t):
    return pl.pallas_call(
        _sgu_kernel,
        grid=(T_ALL // SGU_ROWS,),
        in_specs=[
            pl.BlockSpec((SGU_ROWS, GMLP_WIDTH), lambda i: (i, 0)),
            pl.BlockSpec((SGU_ROWS, GMLP_WIDTH), lambda i: (i, 1)),
            pl.BlockSpec((1, GMLP_WIDTH), lambda i: (0, 0)),
            pl.BlockSpec((1, GMLP_WIDTH), lambda i: (0, 0)),
            pl.BlockSpec((GMLP_HEADS, GMLP_CHUNK, GMLP_CHUNK), lambda i: (0, 0, 0)),
            pl.BlockSpec((GMLP_CHUNK, GMLP_HEADS), lambda i: (0, 0)),
        ],
        out_specs=pl.BlockSpec((SGU_ROWS, GMLP_WIDTH), lambda i: (i, 0)),
        out_shape=jax.ShapeDtypeStruct((T_ALL, GMLP_WIDTH), BF16),
        compiler_params=_params(("arbitrary",)),
        name="sgu",
    )(z, z, ln_g.reshape(1, -1), ln_b.reshape(1, -1), w_s, b_s_t)


def _dot_nt(a, b):
    return lax.dot_general(a, b, (((1,), (1,)), ((), ())), preferred_element_type=F32)


ATTN_KEY_CHUNK = 256
ATTN_TQ = 512
ATTN_Q_SCALE = (DIFF_HEAD_DIM ** -0.5) * math.log2(math.e)


def _attn_kernel(lam_ref, g_ref, q_ref, *rest, latent, lam_init):
    if latent:
        kx_ref, vx_ref, kc_ref, vc_ref, o_ref, s_ref = rest
        chunks = [(kx_ref, vx_ref, r) for r in range(0, SEQ, ATTN_KEY_CHUNK)]
    else:
        kc_ref, vc_ref, o_ref, s_ref = rest
        chunks = []
    chunks += [(kc_ref, vc_ref, r) for r in range(0, CTX_LEN, ATTN_KEY_CHUNK)]
    hd = DIFF_HEAD_DIM
    kc = ATTN_KEY_CHUNK
    lv = lam_ref[...]
    lam = (jnp.exp(jnp.sum(lv[0:1] * lv[1:2], axis=-1, keepdims=True))
           - jnp.exp(jnp.sum(lv[2:3] * lv[3:4], axis=-1, keepdims=True)) + lam_init)
    q = q_ref[...]
    comps = (0, 1)
    qs = [q[:, c * hd:(c + 1) * hd] for c in comps]
    m = [None, None]
    for j, (k_ref, _, r0) in enumerate(chunks):
        for c in comps:
            s = _dot_nt(qs[c], k_ref[r0:r0 + kc, c * hd:(c + 1) * hd])
            s_ref[c, :, j * kc:(j + 1) * kc] = s
            sm = jnp.maximum(s[:, :LANE], s[:, LANE:])
            m[c] = sm if m[c] is None else jnp.maximum(m[c], sm)
    m = [jnp.max(mc, axis=-1, keepdims=True) for mc in m]
    l = [None, None]
    acc = [None, None]

    def exp_chunk(j, c, row_max):
        e = jnp.exp2(s_ref[c, :, j * kc:(j + 1) * kc] - row_max)
        ep = e[:, :LANE] + e[:, LANE:]
        l[c] = ep if l[c] is None else l[c] + ep
        return e.astype(BF16)

    ahead = 4
    n_chunks = len(chunks)
    ready = {(j, c): exp_chunk(j, c, m[c]) for j in range(min(ahead, n_chunks)) for c in comps}
    for j, (_, v_ref, r0) in enumerate(chunks):
        for c in comps:
            pv = jnp.dot(ready.pop((j, c)), v_ref[r0:r0 + kc, :], preferred_element_type=F32)
            acc[c] = pv if acc[c] is None else acc[c] + pv
            if j + ahead < n_chunks:
                zero = jnp.minimum(jnp.abs(pv[-1:, :1]), 0.0)
                ready[j + ahead, c] = exp_chunk(j + ahead, c, m[c] + zero)
    outs = [acc[c] * (1.0 / jnp.sum(l[c], axis=-1, keepdims=True)) for c in comps]
    o = outs[0] - lam * outs[1]
    ms = jnp.mean(o * o, axis=-1, keepdims=True)
    o = o * lax.rsqrt(ms + NORM_EPS) * g_ref[...] * (1.0 - lam_init)
    o_ref[...] = o.astype(o_ref.dtype)


def _diff_attention(qkv, lam_vecs, subln_g, lam_init, *, latent):
    tq = ATTN_TQ if latent else CTX_LEN
    hw = 2 * DIFF_HEAD_DIM
    k_col = D_MODEL // hw
    v_col = 2 * D_MODEL // hw
    n_q = SEQ if latent else CTX_LEN
    q_row0 = 0 if latent else T_X
    nq = n_q // tq
    in_specs = [
        pl.BlockSpec((4, DIFF_HEAD_DIM), lambda b, h, i: (0, 0)),
        pl.BlockSpec((1, hw), lambda b, h, i: (0, 0)),
        pl.BlockSpec((tq, hw), lambda b, h, i: (q_row0 // tq + b * nq + i, h)),
    ]
    args = [lam_vecs, subln_g.reshape(1, hw), qkv]
    if latent:
        in_specs += [
            pl.BlockSpec((SEQ, hw), lambda b, h, i: (b, k_col + h)),
            pl.BlockSpec((SEQ, hw), lambda b, h, i: (b, v_col + h)),
        ]
        args += [qkv, qkv]
    in_specs += [
        pl.BlockSpec((CTX_LEN, hw), lambda b, h, i: (T_X // CTX_LEN + b, k_col + h)),
        pl.BlockSpec((CTX_LEN, hw), lambda b, h, i: (T_X // CTX_LEN + b, v_col + h)),
    ]
    args += [qkv, qkv]
    return pl.pallas_call(
        functools.partial(_attn_kernel, latent=latent, lam_init=lam_init),
        grid=(BATCH, DIFF_HEADS, nq),
        in_specs=in_specs,
        out_specs=pl.BlockSpec((tq, hw), lambda b, h, i: (b * nq + i, h)),
        out_shape=jax.ShapeDtypeStruct((BATCH * n_q, D_MODEL), BF16),
        scratch_shapes=[pltpu.VMEM((2, tq, (SEQ if latent else 0) + CTX_LEN), F32)],
        compiler_params=_params(("arbitrary",) * 3),
        name="diffattn_latent" if latent else "diffattn_ctx",
    )(*args)


GLA_CTX_CHUNKS = CTX_LEN // GLA_CHUNK
GLA_X_CHUNKS = SEQ // GLA_CHUNK
GLA_STEPS = GLA_CTX_CHUNKS + GLA_X_CHUNKS


def _split3_bf16(a):
    hi = a.astype(BF16)
    r = a - hi.astype(F32)
    mid = r.astype(BF16)
    lo = (r - mid.astype(F32)).astype(BF16)
    return hi, mid, lo


def _gla_scan_kernel(qf_ref, kf_ref, vf_ref, gf_ref, qb_ref, kb_ref, vb_ref, gb_ref, wup_ref, bg_ref, tri_ref,
                     of_ref, ob_ref, st_ref):
    @pl.when(pl.program_id(1) == 0)
    def _():
        st_ref[...] = jnp.zeros_like(st_ref)

    dk, dv = GLA_DK_HEAD, GLA_DV_HEAD
    dirs = (0, 1)
    chains = [(d, h) for h in range(GLA_HEADS) for d in dirs]
    q_refs, k_refs, v_refs, g_refs, o_refs = (qf_ref, qb_ref), (kf_ref, kb_ref), (vf_ref, vb_ref), (gf_ref, gb_ref), (of_ref, ob_ref)
    tri = [tri_ref[d] for d in dirs]
    mask = [t.astype(F32) > 0.0 for t in tri]
    z = [jnp.dot(g_refs[d][...], wup_ref[d], preferred_element_type=F32) + bg_ref[d] for d in dirs]
    g = [(jnp.minimum(zd, 0.0) - jnp.log(1.0 + jnp.exp(-jnp.abs(zd)))) * (1.0 / GLA_TAU) for zd in z]
    parts = [_split3_bf16(gd) for gd in g]
    bcum = [sum(jnp.dot(tri[d], part, preferred_element_type=F32) for part in parts[d]) for d in dirs]
    blast = [jnp.sum(gd, axis=0, keepdims=True) for gd in g]
    q = [q_refs[d][...].astype(F32) * (dk ** -0.5) for d in dirs]
    k = [k_refs[d][...].astype(F32) for d in dirs]
    q_dec = [(q[d] * jnp.exp(bcum[d])).astype(BF16) for d in dirs]
    k_inv = [(k[d] * jnp.exp(-bcum[d])).astype(BF16) for d in dirs]
    k_tail = [(k[d] * jnp.exp(blast[d] - bcum[d])).astype(BF16) for d in dirs]
    decay = [jnp.exp(blast[d]) for d in dirs]
    ks = lambda h: slice(h * dk, (h + 1) * dk)
    vs = lambda h: slice(h * dv, (h + 1) * dv)
    st = {c: st_ref[c[0], c[1]] for c in chains}
    att = {(d, h): _dot_nt(q_dec[d][:, ks(h)], k_inv[d][:, ks(h)]) for d, h in chains}
    inter = {(d, h): _dot_nt(q_dec[d][:, ks(h)], st[d, h].astype(BF16)) for d, h in chains}
    att = {(d, h): jnp.where(mask[d], att[d, h], 0.0).astype(BF16) for d, h in chains}
    v = {(d, h): v_refs[d][:, vs(h)] for d, h in chains}
    upd = {c: lax.dot_general(v[c], k_tail[c[0]][:, ks(c[1])], (((0,), (0,)), ((), ())),
                              preferred_element_type=F32) for c in chains}
    out = {c: inter[c] + jnp.dot(att[c], v[c], preferred_element_type=F32) for c in chains}
    for d, h in chains:
        o_refs[d][:, vs(h)] = out[d, h].astype(o_refs[d].dtype)
        st_ref[d, h] = decay[d][:, ks(h)] * st[d, h] + upd[d, h]


def _gla_scan(proj, gates, wup_pad, b_gate, tri):
    def row_block(d):
        def index(b, s):
            c_idx = s if d == 0 else GLA_CTX_CHUNKS - 1 - s
            x_idx = s - GLA_CTX_CHUNKS if d == 0 else GLA_STEPS - 1 - s
            return jnp.where(s < GLA_CTX_CHUNKS,
                             T_X // GLA_CHUNK + b * GLA_CTX_CHUNKS + c_idx,
                             b * GLA_X_CHUNKS + x_idx)
        return index

    def chunk_specs(d):
        rb = row_block(d)
        return [
            pl.BlockSpec((GLA_CHUNK, GLA_DK), lambda b, s: (rb(b, s), 0)),
            pl.BlockSpec((GLA_CHUNK, GLA_DK), lambda b, s: (rb(b, s), 1)),
            pl.BlockSpec((GLA_CHUNK, GLA_DV), lambda b, s: (rb(b, s), 2 * GLA_DK // GLA_DV)),
            pl.BlockSpec((GLA_CHUNK, LANE), lambda b, s: (rb(b, s), 0)),
        ]

    whole = lambda shape: pl.BlockSpec(shape, lambda b, s: (0,) * len(shape))
    out_sds = jax.ShapeDtypeStruct((T_ALL, GLA_DV), BF16)
    return pl.pallas_call(
        _gla_scan_kernel,
        grid=(BATCH, GLA_STEPS),
        in_specs=chunk_specs(0) + chunk_specs(1) + [
            whole((2, LANE, GLA_DK)), whole((2, 1, GLA_DK)), whole((2, GLA_CHUNK, GLA_CHUNK))],
        out_specs=[pl.BlockSpec((GLA_CHUNK, GLA_DV), lambda b, s: (row_block(0)(b, s), 0)),
                   pl.BlockSpec((GLA_CHUNK, GLA_DV), lambda b, s: (row_block(1)(b, s), 0))],
        out_shape=[out_sds, out_sds],
        scratch_shapes=[pltpu.VMEM((2, GLA_HEADS, GLA_DV_HEAD, GLA_DK_HEAD), F32)],
        compiler_params=_params(("arbitrary",) * 2),
        name="gla_scan",
    )(proj, proj, proj, gates, proj, proj, proj, gates, wup_pad, b_gate, tri)


def _rope_tables():
    hd = DIFF_HEAD_DIM
    pos = np.arange(SEQ)
    row = (pos // GRID_W).astype(np.float32)
    col = (pos % GRID_W).astype(np.float32)
    n_freq = hd // 4
    inv = (ROPE_THETA ** (-np.arange(n_freq, dtype=np.float32) / n_freq)).astype(np.float32)
    ang = np.concatenate([row[:, None] * inv, col[:, None] * inv], axis=-1).astype(np.float64)
    cos, sin = np.cos(ang), np.sin(ang)
    cs = np.concatenate([cos, cos], axis=-1).astype(np.float32)
    sn = np.concatenate([-sin, sin], axis=-1).astype(np.float32)
    return jnp.asarray(cs), jnp.asarray(sn)


def _gla_tri():
    lower = np.tril(np.ones((GLA_CHUNK, GLA_CHUNK), np.float32))
    return jnp.asarray(np.stack([lower, lower.T])).astype(BF16)


def _mixer(i, tok, mod, g_mix, w):
    kind = i % N_MIXERS
    if kind == 0:
        ab_even, ab_odd = _chdft(tok, g_mix, mod, _channel_dft_matrix())
        zx = _seqdft(ab_even, ab_odd, n=SEQ, row0=0)
        zc = _seqdft(ab_even, ab_odd, n=CTX_LEN, row0=T_X)
        return (zx, zc), w["fno_w_out"]
    if kind == 1:
        z = _modmm(tok, g_mix, mod, 0, 1, w["gmlp_w_in"].astype(BF16), epilogue="gelu")
        mixed = _sgu(z, w["gmlp_ln_g"], w["gmlp_ln_b"], w["gmlp_w_s"].astype(BF16), w["gmlp_b_s"].T)
        return mixed, w["gmlp_w_out"]
    if kind == 2:
        lam_init = 0.8 - 0.6 * math.exp(-0.3 * i)
        qkv = _modmm(tok, g_mix, mod, 0, 1, w["diff_w_in"].astype(BF16), epilogue="rope", rope=_rope_tables())
        ox = _diff_attention(qkv, w["diff_lambda"], w["diff_subln_g"], lam_init, latent=True)
        oc = _diff_attention(qkv, w["diff_lambda"], w["diff_subln_g"], lam_init, latent=False)
        return (ox, oc), w["diff_w_out"]
    n_main = 2 * GLA_DK + 2 * GLA_DV
    w_in = w["gla_w_in"]
    proj = _modmm(tok, g_mix, mod, 0, 1, w_in.astype(BF16), n_out=n_main)
    w_gate = jnp.pad(w_in[:, n_main:], ((0, 0), (0, LANE - 2 * GLA_GATE_RANK))).astype(BF16)
    gates = _modmm(tok, g_mix, mod, 0, 1, w_gate)
    wup = w["gla_w_gate_up"]
    wup_pad = jnp.zeros((2, LANE, GLA_DK), F32)
    wup_pad = wup_pad.at[0, :GLA_GATE_RANK].set(wup[0])
    wup_pad = wup_pad.at[1, GLA_GATE_RANK:2 * GLA_GATE_RANK].set(wup[1])
    o_fwd, o_bwd = _gla_scan(proj, gates, wup_pad.astype(BF16), w["gla_b_gate"].reshape(2, 1, GLA_DK), _gla_tri())
    return (o_fwd, o_bwd, proj, w["gla_norm_g"]), w["gla_w_out"]


def _layer(i, tok, mod, w, ffn_w, cast_next=(), g_final=None):
    last = i == DEPTH - 1
    mixed, w_out = _mixer(i, tok, mod, w["g_norm_mix"], w)
    joint = _resmm(mixed, w_out.astype(BF16), tok, mod, 2, n_tiles=NT_X if last else NT_ALL,
                   gla=i % N_MIXERS == 3)
    return _ffn(joint, w["g_norm_ffn"], mod, *ffn_w, with_context=not last, g_final=g_final, cast_next=cast_next)


def kernel(x, c, ctx, c_ctx, w_ada, b_ada, g_norm_mix, g_norm_ffn, w_ffn_in, w_ffn_out, g_final, fno_w_out, gmlp_w_in, gmlp_ln_g, gmlp_ln_b, gmlp_w_s, gmlp_b_s, gmlp_w_out, diff_w_in, diff_lambda, diff_subln_g, diff_w_out, gla_w_in, gla_w_gate_up, gla_b_gate, gla_norm_g, gla_w_out):
    tok = (x.reshape(T_X, D_MODEL), ctx.reshape(T_C, D_MODEL))
    c_rows = jnp.concatenate([c, c_ctx[None, :], jnp.zeros((MOD_ROWS - BATCH - 1, D_MODEL), F32)], axis=0)
    mod_all = _ada_table(c_rows, w_ada, b_ada).reshape(DEPTH, MOD_ROWS * N_ADA, 1, D_MODEL)
    mixer_weights = (
        dict(fno_w_out=fno_w_out),
        dict(gmlp_w_in=gmlp_w_in, gmlp_ln_g=gmlp_ln_g, gmlp_ln_b=gmlp_ln_b, gmlp_w_s=gmlp_w_s,
             gmlp_b_s=gmlp_b_s, gmlp_w_out=gmlp_w_out),
        dict(diff_w_in=diff_w_in, diff_lambda=diff_lambda, diff_subln_g=diff_subln_g, diff_w_out=diff_w_out),
        dict(gla_w_in=gla_w_in, gla_w_gate_up=gla_w_gate_up, gla_b_gate=gla_b_gate, gla_norm_g=gla_norm_g,
             gla_w_out=gla_w_out),
    )
    big_weights = (("fno_w_out",), ("gmlp_w_in", "gmlp_w_out"), ("diff_w_in", "diff_w_out"), ("gla_w_in", "gla_w_out"))
    ffn_w = (w_ffn_in[0].astype(BF16), w_ffn_out[0].astype(BF16))
    precast = {}
    for i in range(DEPTH):
        kind, j = i % N_MIXERS, i // N_MIXERS
        last = i == DEPTH - 1
        w = {name: arr[j] for name, arr in mixer_weights[kind].items()}
        w.update(precast)
        w.update(g_norm_mix=g_norm_mix[i], g_norm_ffn=g_norm_ffn[i])
        cast_next, next_names = (), ()
        if not last:
            next_kind, next_j = (i + 1) % N_MIXERS, (i + 1) // N_MIXERS
            next_names = big_weights[next_kind]
            cast_next = ((w_ffn_in, i + 1), (w_ffn_out, i + 1)) + tuple(
                (mixer_weights[next_kind][name], next_j) for name in next_names)
        latent, context, casts = _layer(i, tok, mod_all[i], w, ffn_w, cast_next=cast_next,
                                        g_final=g_final if last else None)
        tok = (latent, context)
        if not last:
            ffn_w = tuple(casts[:2])
            precast = dict(zip(next_names, casts[2:]))
    return latent.reshape(BATCH, SEQ, D_MODEL)
```

```python
import functools
import math

import numpy as np
import jax
import jax.numpy as jnp
from jax import lax
from jax.experimental import pallas as pl
from jax.experimental.pallas import tpu as pltpu

D_MODEL = 2048
BATCH = 2
SEQ = 4096
DEPTH = 4
GRID_W = 64
CTX_LEN = 256
N_MIXERS = 4
NORM_EPS = 1e-6
D_FF = 4 * D_MODEL
N_ADA = 6
FNO_GROUPS = 4
GMLP_CHUNK = 128
GMLP_HEADS = 16
GMLP_WIDTH = D_MODEL
DIFF_HEADS = 8
DIFF_HEAD_DIM = D_MODEL // DIFF_HEADS // 2
ROPE_THETA = 10000.0
GLA_HEADS = 4
GLA_DK = D_MODEL // 2
GLA_DV = D_MODEL
GLA_DK_HEAD = GLA_DK // GLA_HEADS
GLA_DV_HEAD = GLA_DV // GLA_HEADS
GLA_GATE_RANK = 16
GLA_TAU = 16.0
GLA_CHUNK = 64
LN_EPS = 1e-5

T_X = BATCH * SEQ
T_C = BATCH * CTX_LEN
T_ALL = T_X + T_C
TM = 512
NT_X = T_X // TM
NT_ALL = T_ALL // TM
TILES_PER_BATCH = SEQ // TM
MOD_ROWS = 16
CTX_GROUP = BATCH
LANE = 128
MM_TN = 2048
FFN_TF = 1024
FFN_TF_FINAL = 512
FFN_TM = 1024

BF16 = jnp.bfloat16
F32 = jnp.float32
VMEM_LIMIT = 56 * 1024 * 1024
FFN_VMEM_LIMIT = 60 * 1024 * 1024


def _params(semantics, vmem_limit=VMEM_LIMIT):
    return pltpu.CompilerParams(dimension_semantics=semantics, vmem_limit_bytes=vmem_limit)


def _mod_spec(slot, tile_rows=TM, first_tile=0):
    def index(i, *_):
        group = jnp.minimum((i + first_tile) // (SEQ // tile_rows), CTX_GROUP)
        return group * N_ADA + slot, 0, 0
    return pl.BlockSpec((1, 1, D_MODEL), index)


def _modulated(x, g, shift, scale):
    ms = jnp.mean(x * x, axis=-1, keepdims=True)
    y = x * lax.rsqrt(ms + NORM_EPS) * g
    return y * (1.0 + scale) + shift


def _ada_kernel(c_ref, w_ref, b_ref, o_ref):
    c = c_ref[...]
    s = (c * jax.nn.sigmoid(c)).astype(BF16)
    w = w_ref[0].astype(BF16)
    o_ref[0] = jnp.dot(s, w, preferred_element_type=F32) + b_ref[0]


def _ada_table(c8, w_ada, b_ada):
    tn = 1024
    n_out = N_ADA * D_MODEL
    return pl.pallas_call(
        _ada_kernel,
        grid=(DEPTH, n_out // tn),
        in_specs=[
            pl.BlockSpec((MOD_ROWS, D_MODEL), lambda l, j: (0, 0)),
            pl.BlockSpec((1, D_MODEL, tn), lambda l, j: (l, 0, j)),
            pl.BlockSpec((1, 1, tn), lambda l, j: (l, 0, j)),
        ],
        out_specs=pl.BlockSpec((1, MOD_ROWS, tn), lambda l, j: (l, 0, j)),
        out_shape=jax.ShapeDtypeStruct((DEPTH, MOD_ROWS, n_out), F32),
        compiler_params=_params(("arbitrary", "arbitrary")),
        name="ada_table",
    )(c8, w_ada, b_ada.reshape(DEPTH, 1, n_out))


def _gelu_exact(a):
    return 0.5 * a * (1.0 + lax.erf(a * (1.0 / math.sqrt(2.0))))


def _modmm_kernel(*refs, n_x, epilogue, tn, side):
    x_refs, (g_ref, sh_ref, sc_ref, w_ref, *rest) = refs[:n_x], refs[n_x:]
    if epilogue == "rope":
        cs_ref, sn_ref, o_ref, h_ref = rest
    elif side:
        ws_ref, o_ref, os_ref, h_ref = rest
    else:
        o_ref, h_ref = rest
    i = pl.program_id(0)
    j = pl.program_id(1)

    def finish(acc):
        if epilogue == "gelu":
            o_ref[...] = _gelu_exact(acc).astype(o_ref.dtype)
        elif epilogue == "rope":
            acc = acc * jnp.where(j < D_MODEL // tn, ATTN_Q_SCALE, 1.0)
            rotate = jnp.logical_and(j < (2 * D_MODEL) // tn, i < NT_X)

            @pl.when(rotate)
            def _():
                cs = cs_ref[...]
                sn = sn_ref[...]
                for blk in range(tn // LANE):
                    a = acc[:, blk * LANE:(blk + 1) * LANE]
                    swapped = pltpu.roll(a, LANE // 2, 1)
                    o_ref[:, blk * LANE:(blk + 1) * LANE] = (a * cs + swapped * sn).astype(o_ref.dtype)

            @pl.when(jnp.logical_not(rotate))
            def _():
                o_ref[...] = acc.astype(o_ref.dtype)
        else:
            o_ref[...] = acc.astype(o_ref.dtype)

    @pl.when(j == 0)
    def _():
        h = _modulated(_stream_tile(x_refs), g_ref[...], sh_ref[0], sc_ref[0]).astype(BF16)
        h_ref[...] = h
        finish(jnp.dot(h, w_ref[...], preferred_element_type=F32))
        if side:
            os_ref[...] = jnp.dot(h, ws_ref[...], preferred_element_type=F32).astype(os_ref.dtype)

    @pl.when(j > 0)
    def _():
        finish(jnp.dot(h_ref[...], w_ref[...], preferred_element_type=F32))


def _modmm(x, g, mod, slot_shift, slot_scale, w, *, n_out=None, epilogue="plain", rope=None, w_side=None,
           out_dtype=BF16):
    k = w.shape[0]
    n_out = w.shape[1] if n_out is None else n_out
    tn = min(MM_TN, n_out)
    x_specs, x_args = _stream_operand(x, TM, k, NT_ALL)
    in_specs = x_specs + [
        pl.BlockSpec((1, k), lambda i, j: (0, 0)),
        _mod_spec(slot_shift),
        _mod_spec(slot_scale),
        pl.BlockSpec((k, tn), lambda i, j: (0, j)),
    ]
    args = x_args + [g.reshape(1, k), mod, mod, w]
    if epilogue == "rope":
        pos = lambda i, j: (jnp.where(i < NT_X, i % TILES_PER_BATCH, 0), 0)
        in_specs += [pl.BlockSpec((TM, LANE), pos), pl.BlockSpec((TM, LANE), pos)]
        args += list(rope)
    out_specs = [pl.BlockSpec((TM, tn), lambda i, j: (i, j))]
    out_shape = [jax.ShapeDtypeStruct((T_ALL, n_out), out_dtype)]
    if w_side is not None:
        assert epilogue == "plain" and w_side.shape == (k, LANE)
        in_specs.append(pl.BlockSpec((k, LANE), lambda i, j: (0, 0)))
        args.append(w_side)
        out_specs.append(pl.BlockSpec((TM, LANE), lambda i, j: (i, 0)))
        out_shape.append(jax.ShapeDtypeStruct((T_ALL, LANE), out_dtype))
    results = pl.pallas_call(
        functools.partial(_modmm_kernel, n_x=len(x_args), epilogue=epilogue, tn=tn, side=w_side is not None),
        grid=(NT_ALL, n_out // tn),
        in_specs=in_specs,
        out_specs=out_specs,
        out_shape=out_shape,
        scratch_shapes=[pltpu.VMEM((TM, k), BF16)],
        compiler_params=_params(("arbitrary", "arbitrary")),
        name="modmm_" + epilogue,
    )(*args)
    return results[0] if w_side is None else tuple(results)


def _stream_operand(arr, rows, width, n_tiles):
    if isinstance(arr, tuple) and n_tiles == NT_X:
        arr = arr[0]
    if isinstance(arr, tuple):
        assert arr[1].shape == (rows, width)
        return ([pl.BlockSpec((rows, width), lambda i, *_: (jnp.minimum(i, NT_X - 1), 0)),
                 pl.BlockSpec((rows, width), lambda i, *_: (0, 0))], list(arr))
    return [pl.BlockSpec((rows, width), lambda i, *_: (i, 0))], [arr]


def _stream_tile(refs):
    if len(refs) == 2:
        return jnp.where(pl.program_id(0) < NT_X, refs[0][...], refs[1][...])
    return refs[0][...]


def _gla_gated(of_ref, ob_ref, r_ref, g_ref):
    o = of_ref[...].astype(F32) + ob_ref[...].astype(F32)
    r = r_ref[...].astype(F32)
    gate = r * jax.nn.sigmoid(r)
    dv = GLA_DV_HEAD
    heads = []
    for h in range(GLA_HEADS):
        oh = o[:, h * dv:(h + 1) * dv]
        ms = jnp.mean(oh * oh, axis=-1, keepdims=True)
        heads.append((oh * lax.rsqrt(ms + NORM_EPS) * g_ref[...] * gate[:, h * dv:(h + 1) * dv]).astype(BF16))
    return jnp.concatenate(heads, axis=-1)


def _sgu_gated(u_ref, v_ref, lg_ref, lb_ref, ws_ref, bs_ref, a_ref):
    v = v_ref[...].astype(F32)
    mu = jnp.mean(v, axis=-1, keepdims=True)
    vc = v - mu
    var = jnp.mean(vc * vc, axis=-1, keepdims=True)
    vn = (vc * lax.rsqrt(var + LN_EPS) * lg_ref[...] + lb_ref[...]).astype(BF16)
    we = GMLP_WIDTH // GMLP_HEADS
    for c in range(TM // GMLP_CHUNK):
        r0 = c * GMLP_CHUNK
        for h in range(GMLP_HEADS):
            s = jnp.dot(ws_ref[h], vn[r0:r0 + GMLP_CHUNK, h * we:(h + 1) * we], preferred_element_type=F32)
            s = s + bs_ref[:, h:h + 1]
            u = u_ref[r0:r0 + GMLP_CHUNK, h * we:(h + 1) * we].astype(F32)
            a_ref[r0:r0 + GMLP_CHUNK, h * we:(h + 1) * we] = (u * s).astype(a_ref.dtype)
    return a_ref[...]


def _resmm_kernel(*refs, n_a, n_res, prologue):
    a_refs, refs = refs[:n_a], refs[n_a:]
    w_ref, refs = refs[0], refs[1:]
    res_refs, (gate_ref, o_ref, *scratch) = refs[:n_res], refs[n_res:]
    if prologue == "gla":
        a = _gla_gated(*a_refs)
    elif prologue == "sgu":
        a = _sgu_gated(*a_refs, *scratch)
    else:
        a = _stream_tile(a_refs)
    acc = jnp.dot(a, w_ref[...], preferred_element_type=F32)
    o_ref[...] = _stream_tile(res_refs) + gate_ref[0] * acc


def _resmm(a, w, res, mod, slot_gate, *, n_tiles, prologue=None):
    k, n_out = w.shape
    whole = lambda shape: pl.BlockSpec(shape, lambda i: (0,) * len(shape))
    scratch = []
    if prologue == "gla":
        o_fwd, o_bwd, proj, norm_g = a
        r_col = (2 * GLA_DK + GLA_DV) // GLA_DV
        a_specs = [pl.BlockSpec((TM, GLA_DV), lambda i: (i, 0)),
                   pl.BlockSpec((TM, GLA_DV), lambda i: (i, 0)),
                   pl.BlockSpec((TM, GLA_DV), lambda i: (i, r_col)),
                   whole((1, GLA_DV_HEAD))]
        a_args = [o_fwd, o_bwd, proj, norm_g.reshape(1, -1)]
    elif prologue == "sgu":
        z, ln_g, ln_b, w_s, b_s_t = a
        a_specs = [pl.BlockSpec((TM, GMLP_WIDTH), lambda i: (i, 0)),
                   pl.BlockSpec((TM, GMLP_WIDTH), lambda i: (i, 1)),
                   whole((1, GMLP_WIDTH)), whole((1, GMLP_WIDTH)),
                   whole((GMLP_HEADS, GMLP_CHUNK, GMLP_CHUNK)), whole((GMLP_CHUNK, GMLP_HEADS))]
        a_args = [z, z, ln_g.reshape(1, -1), ln_b.reshape(1, -1), w_s, b_s_t]
        scratch = [pltpu.VMEM((TM, GMLP_WIDTH), BF16)]
    else:
        a_specs, a_args = _stream_operand(a, TM, k, n_tiles)
    res_specs, res_args = _stream_operand(res, TM, n_out, n_tiles)
    return pl.pallas_call(
        functools.partial(_resmm_kernel, n_a=len(a_args), n_res=len(res_args), prologue=prologue),
        grid=(n_tiles,),
        in_specs=a_specs + [pl.BlockSpec((k, n_out), lambda i: (0, 0))] + res_specs + [_mod_spec(slot_gate)],
        out_specs=pl.BlockSpec((TM, n_out), lambda i: (i, 0)),
        out_shape=jax.ShapeDtypeStruct((n_tiles * TM, n_out), F32),
        scratch_shapes=scratch,
        compiler_params=_params(("arbitrary",)),
        name="resmm_" + prologue if prologue else "resmm",
    )(*a_args, w, *res_args, mod)


def _ffn_kernel(x_ref, g_ref, sh_ref, sc_ref, gate_ref, w1_ref, w2_ref, *rest, final_norm, n_casts):
    rest = list(rest)
    gf_ref = rest.pop(0) if final_norm else None
    next_in = [rest.pop(0) for _ in range(n_casts)]
    o_ref = rest.pop(0)
    next_out = [rest.pop(0) for _ in range(n_casts)]
    (h_ref,) = rest
    f = pl.program_id(1)
    for src, dst in zip(next_in, next_out):
        dst[...] = src[0].astype(BF16)

    def hidden_step(h):
        a = jnp.dot(h, w1_ref[...], preferred_element_type=F32)
        a = jnp.square(jnp.maximum(a, 0.0)).astype(BF16)
        return jnp.dot(a, w2_ref[...], preferred_element_type=F32)

    @pl.when(f == 0)
    def _():
        h = _modulated(x_ref[...], g_ref[...], sh_ref[0], sc_ref[0]).astype(BF16)
        h_ref[...] = h
        o_ref[...] = hidden_step(h)

    @pl.when(f > 0)
    def _():
        o_ref[...] += hidden_step(h_ref[...])

    @pl.when(f == pl.num_programs(1) - 1)
    def _():
        y = x_ref[...] + gate_ref[0] * o_ref[...]
        if final_norm:
            ms = jnp.mean(y * y, axis=-1, keepdims=True)
            y = y * lax.rsqrt(ms + NORM_EPS) * gf_ref[...]
        o_ref[...] = y


def _ffn_part(x, g, mod, w1, w2, *, tile_rows, first_tile, n_tiles, g_final=None, cast_next=()):
    final_norm = g_final is not None
    tf = FFN_TF_FINAL if final_norm else FFN_TF
    nf = D_FF // tf
    single = pl.Buffered(1)
    in_specs = [
        pl.BlockSpec((tile_rows, D_MODEL), lambda i, f: (i + first_tile, 0)),
        pl.BlockSpec((1, D_MODEL), lambda i, f: (0, 0)),
        _mod_spec(3, tile_rows, first_tile),
        _mod_spec(4, tile_rows, first_tile),
        _mod_spec(5, tile_rows, first_tile),
        pl.BlockSpec((D_MODEL, tf), lambda i, f: (0, f)),
        pl.BlockSpec((tf, D_MODEL), lambda i, f: (f, 0)),
    ]
    args = [x, g.reshape(1, D_MODEL), mod, mod, mod, w1, w2]
    out_specs = [pl.BlockSpec((tile_rows, D_MODEL), lambda i, f: (i, 0), pipeline_mode=single)]
    out_shape = [jax.ShapeDtypeStruct((n_tiles * tile_rows, D_MODEL), F32)]
    if final_norm:
        in_specs.append(pl.BlockSpec((1, D_MODEL), lambda i, f: (0, 0)))
        args.append(g_final.reshape(1, D_MODEL))
    steps = n_tiles * nf
    for w_all, layer in cast_next:
        rows, cols = w_all.shape[1:]
        slab = rows // steps
        assert slab * steps == rows and slab % 16 == 0
        in_specs.append(pl.BlockSpec((1, slab, cols), lambda i, f, layer=layer: (layer, i * nf + f, 0)))
        args.append(w_all)
        out_specs.append(pl.BlockSpec((slab, cols), lambda i, f: (i * nf + f, 0)))
        out_shape.append(jax.ShapeDtypeStruct((rows, cols), BF16))
    results = pl.pallas_call(
        functools.partial(_ffn_kernel, final_norm=final_norm, n_casts=len(cast_next)),
        grid=(n_tiles, nf),
        in_specs=in_specs,
        out_specs=out_specs,
        out_shape=out_shape,
        scratch_shapes=[pltpu.VMEM((tile_rows, D_MODEL), BF16)],
        compiler_params=_params(("arbitrary", "arbitrary"), FFN_VMEM_LIMIT),
        name="ffn_final" if final_norm else "ffn",
    )(*args)
    return results[0], list(results[1:])


def _ffn(x, g, mod, w1, w2, *, with_context, g_final=None, cast_next=()):
    latent, casts = _ffn_part(x, g, mod, w1, w2, tile_rows=FFN_TM, first_tile=0, n_tiles=T_X // FFN_TM,
                              g_final=g_final, cast_next=cast_next)
    context = None
    if with_context:
        context, _ = _ffn_part(x, g, mod, w1, w2, tile_rows=T_C, first_tile=T_X // T_C, n_tiles=1)
    return latent, context, casts


def _dft_tables(n):
    idx = np.arange(n, dtype=np.int64)
    ang = 2.0 * np.pi * ((idx[:, None] * idx[None, :]) % n).astype(np.float64) / n
    return np.cos(ang) / math.sqrt(n), np.sin(ang) / math.sqrt(n)


def _channel_dft_matrix():
    c, s = _dft_tables(D_MODEL // FNO_GROUPS)
    return jnp.asarray(np.concatenate([c, s], axis=1), F32).astype(BF16)


def _chdft_kernel(*refs, n_x):
    x_refs, (g_ref, sh_ref, sc_ref, w_ref, sel_ref, oe_ref, oo_ref) = refs[:n_x], refs[n_x:]
    dg = D_MODEL // FNO_GROUPS
    w = w_ref[...]
    h_all = _modulated(_stream_tile(x_refs), g_ref[...], sh_ref[0], sc_ref[0]).astype(BF16)
    for parity, o_ref in enumerate((oe_ref, oo_ref)):
        h = jnp.dot(sel_ref[parity], h_all, preferred_element_type=F32).astype(BF16)
        for grp in range(FNO_GROUPS):
            cols = slice(grp * dg, (grp + 1) * dg)
            r = jnp.dot(h[:, cols], w, preferred_element_type=F32)
            o_ref[:, cols] = r[:, :dg].astype(o_ref.dtype)
            o_ref[:, D_MODEL + grp * dg:D_MODEL + (grp + 1) * dg] = r[:, dg:].astype(o_ref.dtype)


def _chdft(x, g, mod, w):
    dg = D_MODEL // FNO_GROUPS
    half = jax.ShapeDtypeStruct((T_ALL // 2, 2 * D_MODEL), BF16)
    x_specs, x_args = _stream_operand(x, TM, D_MODEL, NT_ALL)
    rows = np.arange(TM // 2)
    sel = np.zeros((2, TM // 2, TM), np.float32)
    sel[0, rows, 2 * rows] = 1.0
    sel[1, rows, 2 * rows + 1] = 1.0
    return pl.pallas_call(
        functools.partial(_chdft_kernel, n_x=len(x_args)),
        grid=(NT_ALL,),
        in_specs=x_specs + [
            pl.BlockSpec((1, D_MODEL), lambda i: (0, 0)),
            _mod_spec(0),
            _mod_spec(1),
            pl.BlockSpec((dg, 2 * dg), lambda i: (0, 0)),
            pl.BlockSpec((2, TM // 2, TM), lambda i: (0, 0, 0)),
        ],
        out_specs=[pl.BlockSpec((TM // 2, 2 * D_MODEL), lambda i: (i, 0))] * 2,
        out_shape=[half, half],
        compiler_params=_params(("arbitrary",)),
        name="chdft",
    )(*x_args, g.reshape(1, D_MODEL), mod, mod, w, jnp.asarray(sel).astype(BF16))


def _seq_dft_matrices(n, parity):
    half = n // 2
    kk = 2 * np.arange(half, dtype=np.int64) + parity
    if n <= 512:
        ang = 2.0 * np.pi * ((np.arange(half, dtype=np.int64)[:, None] * kk[None, :]) % n) / n
        return (jnp.asarray(np.cos(ang) / math.sqrt(n), F32).astype(BF16),
                jnp.asarray(-np.sin(ang) / math.sqrt(n), F32).astype(BF16))
    r = 64
    q = half // r
    ang_hi = 2.0 * np.pi * (((r * np.arange(q, dtype=np.int64))[:, None] * kk[None, :]) % n) / n
    ang_lo = 2.0 * np.pi * ((np.arange(r, dtype=np.int64)[:, None] * kk[None, :]) % n) / n
    ch = jnp.asarray(np.cos(ang_hi) / math.sqrt(n), F32)[:, None, :]
    sh = jnp.asarray(np.sin(ang_hi) / math.sqrt(n), F32)[:, None, :]
    cl = jnp.asarray(np.cos(ang_lo), F32)[None, :, :]
    sl = jnp.asarray(np.sin(ang_lo), F32)[None, :, :]
    c = (ch * cl - sh * sl).astype(BF16).reshape(half, half)
    s_neg = (-(sh * cl + ch * sl)).astype(BF16).reshape(half, half)
    return c, s_neg


def _seqdft_kernel(ce_ref, se_ref, co_ref, so_ref, ae_ref, be_ref, ao_ref, bo_ref, o_ref, acc_ref):
    kk = pl.program_id(3)

    @pl.when(kk == 0)
    def _():
        acc_ref[...] = jnp.zeros_like(acc_ref)

    acc_ref[0] += (jnp.dot(ce_ref[...], ae_ref[...], preferred_element_type=F32)
                   + jnp.dot(se_ref[...], be_ref[...], preferred_element_type=F32))
    acc_ref[1] += (jnp.dot(co_ref[...], ao_ref[...], preferred_element_type=F32)
                   + jnp.dot(so_ref[...], bo_ref[...], preferred_element_type=F32))

    @pl.when(kk == pl.num_programs(3) - 1)
    def _():
        o_ref[0] = (acc_ref[0] + acc_ref[1]).astype(o_ref.dtype)
        o_ref[1] = (acc_ref[0] - acc_ref[1]).astype(o_ref.dtype)


def _seqdft(ab_even, ab_odd, *, n, row0):
    half = n // 2
    tmm = min(half, 1024)
    tk = min(half, 512)
    tn = 1024
    nk = half // tk
    col_b = D_MODEL // tn
    mat = pl.BlockSpec((tmm, tk), lambda b, m, j, k: (m, k))
    a_spec = pl.BlockSpec((tk, tn), lambda b, m, j, k: (row0 // 2 // tk + b * nk + k, j))
    b_spec = pl.BlockSpec((tk, tn), lambda b, m, j, k: (row0 // 2 // tk + b * nk + k, col_b + j))
    out = pl.pallas_call(
        _seqdft_kernel,
        grid=(BATCH, half // tmm, D_MODEL // tn, nk),
        in_specs=[mat, mat, mat, mat, a_spec, b_spec, a_spec, b_spec],
        out_specs=pl.BlockSpec((2, tmm, tn), lambda b, m, j, k: (b, m, j)),
        out_shape=jax.ShapeDtypeStruct((BATCH * 2, half, D_MODEL), BF16),
        scratch_shapes=[pltpu.VMEM((2, tmm, tn), F32)],
        compiler_params=_params(("arbitrary",) * 4),
        name="seqdft",
    )(*_seq_dft_matrices(n, 0), *_seq_dft_matrices(n, 1), ab_even, ab_even, ab_odd, ab_odd)
    return out.reshape(BATCH * n, D_MODEL)


def _dot_nt(a, b):
    return lax.dot_general(a, b, (((1,), (1,)), ((), ())), preferred_element_type=F32)


ATTN_KEY_CHUNK = 256
ATTN_TQ = 512
ATTN_Q_SCALE = (DIFF_HEAD_DIM ** -0.5) * math.log2(math.e)


def _attn_kernel(lam_ref, g_ref, q_ref, *rest, latent, lam_init):
    if latent:
        kx_ref, vx_ref, kc_ref, vc_ref, o_ref, s_ref = rest
        chunks = [(kx_ref, vx_ref, r) for r in range(0, SEQ, ATTN_KEY_CHUNK)]
    else:
        kc_ref, vc_ref, o_ref, s_ref = rest
        chunks = []
    chunks += [(kc_ref, vc_ref, r) for r in range(0, CTX_LEN, ATTN_KEY_CHUNK)]
    hd = DIFF_HEAD_DIM
    kc = ATTN_KEY_CHUNK
    lv = lam_ref[...]
    lam = (jnp.exp(jnp.sum(lv[0:1] * lv[1:2], axis=-1, keepdims=True))
           - jnp.exp(jnp.sum(lv[2:3] * lv[3:4], axis=-1, keepdims=True)) + lam_init)
    q = q_ref[...]
    comps = (0, 1)
    qs = [q[:, c * hd:(c + 1) * hd] for c in comps]
    m = [None, None]
    for j, (k_ref, _, r0) in enumerate(chunks):
        for c in comps:
            s = _dot_nt(qs[c], k_ref[r0:r0 + kc, c * hd:(c + 1) * hd])
            s_ref[c, :, j * kc:(j + 1) * kc] = s
            sm = jnp.maximum(s[:, :LANE], s[:, LANE:])
            m[c] = sm if m[c] is None else jnp.maximum(m[c], sm)
    m = [jnp.max(mc, axis=-1, keepdims=True) for mc in m]
    l = [None, None]
    acc = [None, None]

    def exp_chunk(j, c, row_max):
        e = jnp.exp2(s_ref[c, :, j * kc:(j + 1) * kc] - row_max)
        ep = e[:, :LANE] + e[:, LANE:]
        l[c] = ep if l[c] is None else l[c] + ep
        return e.astype(BF16)

    ahead = 4
    n_chunks = len(chunks)
    ready = {(j, c): exp_chunk(j, c, m[c]) for j in range(min(ahead, n_chunks)) for c in comps}
    for j, (_, v_ref, r0) in enumerate(chunks):
        for c in comps:
            pv = jnp.dot(ready.pop((j, c)), v_ref[r0:r0 + kc, :], preferred_element_type=F32)
            acc[c] = pv if acc[c] is None else acc[c] + pv
            if j + ahead < n_chunks:
                zero = jnp.minimum(jnp.abs(pv[-1:, :1]), 0.0)
                ready[j + ahead, c] = exp_chunk(j + ahead, c, m[c] + zero)
    outs = [acc[c] * (1.0 / jnp.sum(l[c], axis=-1, keepdims=True)) for c in comps]
    o = outs[0] - lam * outs[1]
    ms = jnp.mean(o * o, axis=-1, keepdims=True)
    o = o * lax.rsqrt(ms + NORM_EPS) * g_ref[...] * (1.0 - lam_init)
    o_ref[...] = o.astype(o_ref.dtype)


def _diff_attention(qkv, lam_vecs, subln_g, lam_init, *, latent):
    tq = ATTN_TQ if latent else CTX_LEN
    hw = 2 * DIFF_HEAD_DIM
    k_col = D_MODEL // hw
    v_col = 2 * D_MODEL // hw
    n_q = SEQ if latent else CTX_LEN
    q_row0 = 0 if latent else T_X
    nq = n_q // tq
    in_specs = [
        pl.BlockSpec((4, DIFF_HEAD_DIM), lambda b, h, i: (0, 0)),
        pl.BlockSpec((1, hw), lambda b, h, i: (0, 0)),
        pl.BlockSpec((tq, hw), lambda b, h, i: (q_row0 // tq + b * nq + i, h)),
    ]
    args = [lam_vecs, subln_g.reshape(1, hw), qkv]
    if latent:
        in_specs += [
            pl.BlockSpec((SEQ, hw), lambda b, h, i: (b, k_col + h)),
            pl.BlockSpec((SEQ, hw), lambda b, h, i: (b, v_col + h)),
        ]
        args += [qkv, qkv]
    in_specs += [
        pl.BlockSpec((CTX_LEN, hw), lambda b, h, i: (T_X // CTX_LEN + b, k_col + h)),
        pl.BlockSpec((CTX_LEN, hw), lambda b, h, i: (T_X // CTX_LEN + b, v_col + h)),
    ]
    args += [qkv, qkv]
    return pl.pallas_call(
        functools.partial(_attn_kernel, latent=latent, lam_init=lam_init),
        grid=(BATCH, DIFF_HEADS, nq),
        in_specs=in_specs,
        out_specs=pl.BlockSpec((tq, hw), lambda b, h, i: (b * nq + i, h)),
        out_shape=jax.ShapeDtypeStruct((BATCH * n_q, D_MODEL), BF16),
        scratch_shapes=[pltpu.VMEM((2, tq, (SEQ if latent else 0) + CTX_LEN), F32)],
        compiler_params=_params(("arbitrary",) * 3),
        name="diffattn_latent" if latent else "diffattn_ctx",
    )(*args)


GLA_CTX_CHUNKS = CTX_LEN // GLA_CHUNK
GLA_X_CHUNKS = SEQ // GLA_CHUNK
GLA_STEPS = GLA_CTX_CHUNKS + GLA_X_CHUNKS


def _split3_bf16(a):
    hi = a.astype(BF16)
    r = a - hi.astype(F32)
    mid = r.astype(BF16)
    lo = (r - mid.astype(F32)).astype(BF16)
    return hi, mid, lo


def _gla_scan_kernel(qf_ref, kf_ref, vf_ref, gf_ref, qb_ref, kb_ref, vb_ref, gb_ref, wup_ref, bg_ref, tri_ref,
                     of_ref, ob_ref, st_ref):
    @pl.when(pl.program_id(1) == 0)
    def _():
        st_ref[...] = jnp.zeros_like(st_ref)

    dk, dv = GLA_DK_HEAD, GLA_DV_HEAD
    dirs = (0, 1)
    chains = [(d, h) for h in range(GLA_HEADS) for d in dirs]
    q_refs, k_refs, v_refs, g_refs, o_refs = (qf_ref, qb_ref), (kf_ref, kb_ref), (vf_ref, vb_ref), (gf_ref, gb_ref), (of_ref, ob_ref)
    tri = [tri_ref[d] for d in dirs]
    mask = [t.astype(F32) > 0.0 for t in tri]
    z = [jnp.dot(g_refs[d][...], wup_ref[d], preferred_element_type=F32) + bg_ref[d] for d in dirs]
    g = [(jnp.minimum(zd, 0.0) - jnp.log(1.0 + jnp.exp(-jnp.abs(zd)))) * (1.0 / GLA_TAU) for zd in z]
    parts = [_split3_bf16(gd) for gd in g]
    bcum = [sum(jnp.dot(tri[d], part, preferred_element_type=F32) for part in parts[d]) for d in dirs]
    blast = [jnp.sum(gd, axis=0, keepdims=True) for gd in g]
    q = [q_refs[d][...].astype(F32) * (dk ** -0.5) for d in dirs]
    k = [k_refs[d][...].astype(F32) for d in dirs]
    q_dec = [(q[d] * jnp.exp(bcum[d])).astype(BF16) for d in dirs]
    k_inv = [(k[d] * jnp.exp(-bcum[d])).astype(BF16) for d in dirs]
    k_tail = [(k[d] * jnp.exp(blast[d] - bcum[d])).astype(BF16) for d in dirs]
    decay = [jnp.exp(blast[d]) for d in dirs]
    ks = lambda h: slice(h * dk, (h + 1) * dk)
    vs = lambda h: slice(h * dv, (h + 1) * dv)
    st = {c: st_ref[c[0], c[1]] for c in chains}
    att = {(d, h): _dot_nt(q_dec[d][:, ks(h)], k_inv[d][:, ks(h)]) for d, h in chains}
    inter = {(d, h): _dot_nt(q_dec[d][:, ks(h)], st[d, h].astype(BF16)) for d, h in chains}
    att = {(d, h): jnp.where(mask[d], att[d, h], 0.0).astype(BF16) for d, h in chains}
    v = {(d, h): v_refs[d][:, vs(h)] for d, h in chains}
    upd = {c: lax.dot_general(v[c], k_tail[c[0]][:, ks(c[1])], (((0,), (0,)), ((), ())),
                              preferred_element_type=F32) for c in chains}
    out = {c: inter[c] + jnp.dot(att[c], v[c], preferred_element_type=F32) for c in chains}
    for d, h in chains:
        o_refs[d][:, vs(h)] = out[d, h].astype(o_refs[d].dtype)
        st_ref[d, h] = decay[d][:, ks(h)] * st[d, h] + upd[d, h]


def _gla_scan(proj, gates, wup_pad, b_gate, tri):
    def row_block(d):
        def index(b, s):
            c_idx = s if d == 0 else GLA_CTX_CHUNKS - 1 - s
            x_idx = s - GLA_CTX_CHUNKS if d == 0 else GLA_STEPS - 1 - s
            return jnp.where(s < GLA_CTX_CHUNKS,
                             T_X // GLA_CHUNK + b * GLA_CTX_CHUNKS + c_idx,
                             b * GLA_X_CHUNKS + x_idx)
        return index

    def chunk_specs(d):
        rb = row_block(d)
        return [
            pl.BlockSpec((GLA_CHUNK, GLA_DK), lambda b, s: (rb(b, s), 0)),
            pl.BlockSpec((GLA_CHUNK, GLA_DK), lambda b, s: (rb(b, s), 1)),
            pl.BlockSpec((GLA_CHUNK, GLA_DV), lambda b, s: (rb(b, s), 2 * GLA_DK // GLA_DV)),
            pl.BlockSpec((GLA_CHUNK, LANE), lambda b, s: (rb(b, s), 0)),
        ]

    whole = lambda shape: pl.BlockSpec(shape, lambda b, s: (0,) * len(shape))
    out_sds = jax.ShapeDtypeStruct((T_ALL, GLA_DV), BF16)
    return pl.pallas_call(
        _gla_scan_kernel,
        grid=(BATCH, GLA_STEPS),
        in_specs=chunk_specs(0) + chunk_specs(1) + [
            whole((2, LANE, GLA_DK)), whole((2, 1, GLA_DK)), whole((2, GLA_CHUNK, GLA_CHUNK))],
        out_specs=[pl.BlockSpec((GLA_CHUNK, GLA_DV), lambda b, s: (row_block(0)(b, s), 0)),
                   pl.BlockSpec((GLA_CHUNK, GLA_DV), lambda b, s: (row_block(1)(b, s), 0))],
        out_shape=[out_sds, out_sds],
        scratch_shapes=[pltpu.VMEM((2, GLA_HEADS, GLA_DV_HEAD, GLA_DK_HEAD), F32)],
        compiler_params=_params(("arbitrary",) * 2),
        name="gla_scan",
    )(proj, proj, proj, gates, proj, proj, proj, gates, wup_pad, b_gate, tri)


def _rope_tables():
    hd = DIFF_HEAD_DIM
    pos = np.arange(SEQ)
    row = (pos // GRID_W).astype(np.float32)
    col = (pos % GRID_W).astype(np.float32)
    n_freq = hd // 4
    inv = (ROPE_THETA ** (-np.arange(n_freq, dtype=np.float32) / n_freq)).astype(np.float32)
    ang = np.concatenate([row[:, None] * inv, col[:, None] * inv], axis=-1).astype(np.float64)
    cos, sin = np.cos(ang), np.sin(ang)
    cs = np.concatenate([cos, cos], axis=-1).astype(np.float32)
    sn = np.concatenate([-sin, sin], axis=-1).astype(np.float32)
    return jnp.asarray(cs), jnp.asarray(sn)


def _gla_tri():
    lower = np.tril(np.ones((GLA_CHUNK, GLA_CHUNK), np.float32))
    return jnp.asarray(np.stack([lower, lower.T])).astype(BF16)


def _mixer(i, tok, mod, g_mix, w):
    kind = i % N_MIXERS
    if kind == 0:
        ab_even, ab_odd = _chdft(tok, g_mix, mod, _channel_dft_matrix())
        zx = _seqdft(ab_even, ab_odd, n=SEQ, row0=0)
        zc = _seqdft(ab_even, ab_odd, n=CTX_LEN, row0=T_X)
        return (zx, zc), w["fno_w_out"]
    if kind == 1:
        z = _modmm(tok, g_mix, mod, 0, 1, w["gmlp_w_in"].astype(BF16), epilogue="gelu")
        return (z, w["gmlp_ln_g"], w["gmlp_ln_b"], w["gmlp_w_s"].astype(BF16), w["gmlp_b_s"].T), w["gmlp_w_out"]
    if kind == 2:
        lam_init = 0.8 - 0.6 * math.exp(-0.3 * i)
        qkv = _modmm(tok, g_mix, mod, 0, 1, w["diff_w_in"].astype(BF16), epilogue="rope", rope=_rope_tables())
        ox = _diff_attention(qkv, w["diff_lambda"], w["diff_subln_g"], lam_init, latent=True)
        oc = _diff_attention(qkv, w["diff_lambda"], w["diff_subln_g"], lam_init, latent=False)
        return (ox, oc), w["diff_w_out"]
    n_main = 2 * GLA_DK + 2 * GLA_DV
    w_in = w["gla_w_in"]
    w_gate = jnp.pad(w_in[:, n_main:], ((0, 0), (0, LANE - 2 * GLA_GATE_RANK))).astype(BF16)
    proj, gates = _modmm(tok, g_mix, mod, 0, 1, w_in.astype(BF16), n_out=n_main, w_side=w_gate)
    wup = w["gla_w_gate_up"]
    wup_pad = jnp.zeros((2, LANE, GLA_DK), F32)
    wup_pad = wup_pad.at[0, :GLA_GATE_RANK].set(wup[0])
    wup_pad = wup_pad.at[1, GLA_GATE_RANK:2 * GLA_GATE_RANK].set(wup[1])
    o_fwd, o_bwd = _gla_scan(proj, gates, wup_pad.astype(BF16), w["gla_b_gate"].reshape(2, 1, GLA_DK), _gla_tri())
    return (o_fwd, o_bwd, proj, w["gla_norm_g"]), w["gla_w_out"]


def _layer(i, tok, mod, w, ffn_w, cast_next=(), g_final=None):
    last = i == DEPTH - 1
    mixed, w_out = _mixer(i, tok, mod, w["g_norm_mix"], w)
    joint = _resmm(mixed, w_out.astype(BF16), tok, mod, 2, n_tiles=NT_X if last else NT_ALL,
                   prologue={1: "sgu", 3: "gla"}.get(i % N_MIXERS))
    return _ffn(joint, w["g_norm_ffn"], mod, *ffn_w, with_context=not last, g_final=g_final, cast_next=cast_next)


def kernel(x, c, ctx, c_ctx, w_ada, b_ada, g_norm_mix, g_norm_ffn, w_ffn_in, w_ffn_out, g_final, fno_w_out, gmlp_w_in, gmlp_ln_g, gmlp_ln_b, gmlp_w_s, gmlp_b_s, gmlp_w_out, diff_w_in, diff_lambda, diff_subln_g, diff_w_out, gla_w_in, gla_w_gate_up, gla_b_gate, gla_norm_g, gla_w_out):
    tok = (x.reshape(T_X, D_MODEL), ctx.reshape(T_C, D_MODEL))
    c_rows = jnp.concatenate([c, c_ctx[None, :], jnp.zeros((MOD_ROWS - BATCH - 1, D_MODEL), F32)], axis=0)
    mod_all = _ada_table(c_rows, w_ada, b_ada).reshape(DEPTH, MOD_ROWS * N_ADA, 1, D_MODEL)
    mixer_weights = (
        dict(fno_w_out=fno_w_out),
        dict(gmlp_w_in=gmlp_w_in, gmlp_ln_g=gmlp_ln_g, gmlp_ln_b=gmlp_ln_b, gmlp_w_s=gmlp_w_s,
             gmlp_b_s=gmlp_b_s, gmlp_w_out=gmlp_w_out),
        dict(diff_w_in=diff_w_in, diff_lambda=diff_lambda, diff_subln_g=diff_subln_g, diff_w_out=diff_w_out),
        dict(gla_w_in=gla_w_in, gla_w_gate_up=gla_w_gate_up, gla_b_gate=gla_b_gate, gla_norm_g=gla_norm_g,
             gla_w_out=gla_w_out),
    )
    big_weights = (("fno_w_out",), ("gmlp_w_in", "gmlp_w_out"), ("diff_w_in", "diff_w_out"), ("gla_w_in", "gla_w_out"))
    ffn_w = (w_ffn_in[0].astype(BF16), w_ffn_out[0].astype(BF16))
    precast = {}
    for i in range(DEPTH):
        kind, j = i % N_MIXERS, i // N_MIXERS
        last = i == DEPTH - 1
        w = {name: arr[j] for name, arr in mixer_weights[kind].items()}
        w.update(precast)
        w.update(g_norm_mix=g_norm_mix[i], g_norm_ffn=g_norm_ffn[i])
        cast_next, next_names = (), ()
        if not last:
            next_kind, next_j = (i + 1) % N_MIXERS, (i + 1) // N_MIXERS
            next_names = big_weights[next_kind]
            cast_next = ((w_ffn_in, i + 1), (w_ffn_out, i + 1)) + tuple(
                (mixer_weights[next_kind][name], next_j) for name in next_names)
        latent, context, casts = _layer(i, tok, mod_all[i], w, ffn_w, cast_next=cast_next,
                                        g_final=g_final if last else None)
        tok = (latent, context)
        if not last:
            ffn_w = tuple(casts[:2])
            precast = dict(zip(next_names, casts[2:]))
    return latent.reshape(BATCH, SEQ, D_MODEL)
```

```python
import functools
import math

import numpy as np
import jax
import jax.numpy as jnp
from jax import lax
from jax.experimental import pallas as pl
from jax.experimental.pallas import tpu as pltpu

D_MODEL = 2048
BATCH = 2
SEQ = 4096
DEPTH = 4
GRID_W = 64
CTX_LEN = 256
N_MIXERS = 4
NORM_EPS = 1e-6
D_FF = 4 * D_MODEL
N_ADA = 6
FNO_GROUPS = 4
GMLP_CHUNK = 128
GMLP_HEADS = 16
GMLP_WIDTH = D_MODEL
DIFF_HEADS = 8
DIFF_HEAD_DIM = D_MODEL // DIFF_HEADS // 2
ROPE_THETA = 10000.0
GLA_HEADS = 4
GLA_DK = D_MODEL // 2
GLA_DV = D_MODEL
GLA_DK_HEAD = GLA_DK // GLA_HEADS
GLA_DV_HEAD = GLA_DV // GLA_HEADS
GLA_GATE_RANK = 16
GLA_TAU = 16.0
GLA_CHUNK = 64
LN_EPS = 1e-5

T_X = BATCH * SEQ
T_C = BATCH * CTX_LEN
T_ALL = T_X + T_C
TM = 512
NT_X = T_X // TM
NT_ALL = T_ALL // TM
TILES_PER_BATCH = SEQ // TM
MOD_ROWS = 16
CTX_GROUP = BATCH
LANE = 128
MM_TN = 2048
FFN_TF = 1024
FFN_TM = 1024

BF16 = jnp.bfloat16
F32 = jnp.float32
VMEM_LIMIT = 56 * 1024 * 1024
FFN_VMEM_LIMIT = 60 * 1024 * 1024


def _params(semantics, vmem_limit=VMEM_LIMIT):
    return pltpu.CompilerParams(dimension_semantics=semantics, vmem_limit_bytes=vmem_limit)


def _mod_spec(slot, tile_rows=TM, first_tile=0):
    def index(i, *_):
        group = jnp.minimum((i + first_tile) // (SEQ // tile_rows), CTX_GROUP)
        return group * N_ADA + slot, 0, 0
    return pl.BlockSpec((1, 1, D_MODEL), index)


def _modulated(x, g, shift, scale):
    ms = jnp.mean(x * x, axis=-1, keepdims=True)
    y = x * lax.rsqrt(ms + NORM_EPS) * g
    return y * (1.0 + scale) + shift


def _ada_kernel(c_ref, w_ref, b_ref, o_ref):
    c = c_ref[...]
    s = (c * jax.nn.sigmoid(c)).astype(BF16)
    w = w_ref[0].astype(BF16)
    o_ref[0] = jnp.dot(s, w, preferred_element_type=F32) + b_ref[0]


def _ada_table(c8, w_ada, b_ada):
    tn = 1024
    n_out = N_ADA * D_MODEL
    return pl.pallas_call(
        _ada_kernel,
        grid=(DEPTH, n_out // tn),
        in_specs=[
            pl.BlockSpec((MOD_ROWS, D_MODEL), lambda l, j: (0, 0)),
            pl.BlockSpec((1, D_MODEL, tn), lambda l, j: (l, 0, j)),
            pl.BlockSpec((1, 1, tn), lambda l, j: (l, 0, j)),
        ],
        out_specs=pl.BlockSpec((1, MOD_ROWS, tn), lambda l, j: (l, 0, j)),
        out_shape=jax.ShapeDtypeStruct((DEPTH, MOD_ROWS, n_out), F32),
        compiler_params=_params(("arbitrary", "arbitrary")),
        name="ada_table",
    )(c8, w_ada, b_ada.reshape(DEPTH, 1, n_out))


def _gelu_exact(a):
    return 0.5 * a * (1.0 + lax.erf(a * (1.0 / math.sqrt(2.0))))


def _modmm_kernel(*refs, n_x, epilogue, tn, n_steps, side):
    x_refs, (g_ref, sh_ref, sc_ref, w_ref, *rest) = refs[:n_x], refs[n_x:]
    if epilogue == "rope":
        cs_ref, sn_ref, o_ref, h_ref = rest
    elif side:
        ws_ref, o_ref, os_ref, h_ref = rest
    else:
        o_ref, h_ref = rest
    j = pl.program_id(1)

    def column_step(step):
        if step == 0:
            h = _modulated(_stream_tile(x_refs), g_ref[...], sh_ref[0], sc_ref[0]).astype(BF16)
            h_ref[...] = h
        else:
            h = h_ref[...]
        acc = jnp.dot(h, w_ref[...], preferred_element_type=F32)
        if epilogue == "gelu":
            o_ref[...] = _gelu_exact(acc).astype(o_ref.dtype)
        elif epilogue == "rope" and step < (2 * D_MODEL) // tn:
            if step < D_MODEL // tn:
                acc = acc * ATTN_Q_SCALE
            cs = cs_ref[...]
            sn = sn_ref[...]
            for blk in range(tn // LANE):
                a = acc[:, blk * LANE:(blk + 1) * LANE]
                swapped = pltpu.roll(a, LANE // 2, 1)
                o_ref[:, blk * LANE:(blk + 1) * LANE] = (a * cs + swapped * sn).astype(o_ref.dtype)
        else:
            o_ref[...] = acc.astype(o_ref.dtype)
        if step == 0 and side:
            os_ref[...] = jnp.dot(h, ws_ref[...], preferred_element_type=F32).astype(os_ref.dtype)

    if epilogue == "rope":
        for step in range(n_steps):
            pl.when(j == step)(functools.partial(column_step, step))
    else:
        pl.when(j == 0)(functools.partial(column_step, 0))
        pl.when(j > 0)(functools.partial(column_step, 1))


def _modmm(x, g, mod, slot_shift, slot_scale, w, *, n_out=None, epilogue="plain", rope=None, w_side=None,
           out_dtype=BF16):
    k = w.shape[0]
    n_out = w.shape[1] if n_out is None else n_out
    tn = min(MM_TN, n_out)
    x_specs, x_args = _stream_operand(x, TM, k, NT_ALL)
    in_specs = x_specs + [
        pl.BlockSpec((1, k), lambda i, j: (0, 0)),
        _mod_spec(slot_shift),
        _mod_spec(slot_scale),
        pl.BlockSpec((k, tn), lambda i, j: (0, j)),
    ]
    args = x_args + [g.reshape(1, k), mod, mod, w]
    if epilogue == "rope":
        pos = lambda i, j: (jnp.where(i < NT_X, i % TILES_PER_BATCH, TILES_PER_BATCH), 0)
        in_specs += [pl.BlockSpec((TM, LANE), pos), pl.BlockSpec((TM, LANE), pos)]
        args += list(rope)
    out_specs = [pl.BlockSpec((TM, tn), lambda i, j: (i, j))]
    out_shape = [jax.ShapeDtypeStruct((T_ALL, n_out), out_dtype)]
    if w_side is not None:
        assert epilogue == "plain" and w_side.shape == (k, LANE)
        in_specs.append(pl.BlockSpec((k, LANE), lambda i, j: (0, 0)))
        args.append(w_side)
        out_specs.append(pl.BlockSpec((TM, LANE), lambda i, j: (i, 0)))
        out_shape.append(jax.ShapeDtypeStruct((T_ALL, LANE), out_dtype))
    results = pl.pallas_call(
        functools.partial(_modmm_kernel, n_x=len(x_args), epilogue=epilogue, tn=tn, n_steps=n_out // tn,
                          side=w_side is not None),
        grid=(NT_ALL, n_out // tn),
        in_specs=in_specs,
        out_specs=out_specs,
        out_shape=out_shape,
        scratch_shapes=[pltpu.VMEM((TM, k), BF16)],
        compiler_params=_params(("arbitrary", "arbitrary")),
        name="modmm_" + epilogue,
    )(*args)
    return results[0] if w_side is None else tuple(results)


def _stream_operand(arr, rows, width, n_tiles):
    if isinstance(arr, tuple) and n_tiles == NT_X:
        arr = arr[0]
    if isinstance(arr, tuple):
        assert arr[1].shape == (rows, width)
        return ([pl.BlockSpec((rows, width), lambda i, *_: (jnp.minimum(i, NT_X - 1), 0)),
                 pl.BlockSpec((rows, width), lambda i, *_: (0, 0))], list(arr))
    return [pl.BlockSpec((rows, width), lambda i, *_: (i, 0))], [arr]


def _stream_tile(refs):
    if len(refs) == 2:
        return jnp.where(pl.program_id(0) < NT_X, refs[0][...], refs[1][...])
    return refs[0][...]


def _gla_gated(of_ref, ob_ref, r_ref, g_ref):
    o = of_ref[...].astype(F32) + ob_ref[...].astype(F32)
    r = r_ref[...].astype(F32)
    gate = r * jax.nn.sigmoid(r)
    dv = GLA_DV_HEAD
    heads = []
    for h in range(GLA_HEADS):
        oh = o[:, h * dv:(h + 1) * dv]
        ms = jnp.mean(oh * oh, axis=-1, keepdims=True)
        heads.append((oh * lax.rsqrt(ms + NORM_EPS) * g_ref[...] * gate[:, h * dv:(h + 1) * dv]).astype(BF16))
    return jnp.concatenate(heads, axis=-1)


def _sgu_gated(u_ref, v_ref, lg_ref, lb_ref, ws_ref, bs_ref, a_ref):
    v = v_ref[...].astype(F32)
    mu = jnp.mean(v, axis=-1, keepdims=True)
    vc = v - mu
    var = jnp.mean(vc * vc, axis=-1, keepdims=True)
    vn = (vc * lax.rsqrt(var + LN_EPS) * lg_ref[...] + lb_ref[...]).astype(BF16)
    we = GMLP_WIDTH // GMLP_HEADS
    for c in range(TM // GMLP_CHUNK):
        r0 = c * GMLP_CHUNK
        for h in range(GMLP_HEADS):
            s = jnp.dot(ws_ref[h], vn[r0:r0 + GMLP_CHUNK, h * we:(h + 1) * we], preferred_element_type=F32)
            s = s + bs_ref[:, h:h + 1]
            u = u_ref[r0:r0 + GMLP_CHUNK, h * we:(h + 1) * we].astype(F32)
            a_ref[r0:r0 + GMLP_CHUNK, h * we:(h + 1) * we] = (u * s).astype(a_ref.dtype)
    return a_ref[...]


def _resmm_kernel(*refs, n_a, n_res, prologue):
    a_refs, refs = refs[:n_a], refs[n_a:]
    w_ref, refs = refs[0], refs[1:]
    res_refs, (gate_ref, o_ref, *scratch) = refs[:n_res], refs[n_res:]
    if prologue == "gla":
        a = _gla_gated(*a_refs)
    elif prologue == "sgu":
        a = _sgu_gated(*a_refs, *scratch)
    else:
        a = _stream_tile(a_refs)
    acc = jnp.dot(a, w_ref[...], preferred_element_type=F32)
    o_ref[...] = _stream_tile(res_refs) + gate_ref[0] * acc


def _resmm(a, w, res, mod, slot_gate, *, n_tiles, prologue=None):
    k, n_out = w.shape
    whole = lambda shape: pl.BlockSpec(shape, lambda i: (0,) * len(shape))
    scratch = []
    if prologue == "gla":
        o_fwd, o_bwd, proj, norm_g = a
        r_col = (2 * GLA_DK + GLA_DV) // GLA_DV
        a_specs = [pl.BlockSpec((TM, GLA_DV), lambda i: (i, 0)),
                   pl.BlockSpec((TM, GLA_DV), lambda i: (i, 0)),
                   pl.BlockSpec((TM, GLA_DV), lambda i: (i, r_col)),
                   whole((1, GLA_DV_HEAD))]
        a_args = [o_fwd, o_bwd, proj, norm_g.reshape(1, -1)]
    elif prologue == "sgu":
        z, ln_g, ln_b, w_s, b_s_t = a
        a_specs = [pl.BlockSpec((TM, GMLP_WIDTH), lambda i: (i, 0)),
                   pl.BlockSpec((TM, GMLP_WIDTH), lambda i: (i, 1)),
                   whole((1, GMLP_WIDTH)), whole((1, GMLP_WIDTH)),
                   whole((GMLP_HEADS, GMLP_CHUNK, GMLP_CHUNK)), whole((GMLP_CHUNK, GMLP_HEADS))]
        a_args = [z, z, ln_g.reshape(1, -1), ln_b.reshape(1, -1), w_s, b_s_t]
        scratch = [pltpu.VMEM((TM, GMLP_WIDTH), BF16)]
    else:
        a_specs, a_args = _stream_operand(a, TM, k, n_tiles)
    res_specs, res_args = _stream_operand(res, TM, n_out, n_tiles)
    return pl.pallas_call(
        functools.partial(_resmm_kernel, n_a=len(a_args), n_res=len(res_args), prologue=prologue),
        grid=(n_tiles,),
        in_specs=a_specs + [pl.BlockSpec((k, n_out), lambda i: (0, 0))] + res_specs + [_mod_spec(slot_gate)],
        out_specs=pl.BlockSpec((TM, n_out), lambda i: (i, 0)),
        out_shape=jax.ShapeDtypeStruct((n_tiles * TM, n_out), F32),
        scratch_shapes=scratch,
        compiler_params=_params(("arbitrary",)),
        name="resmm_" + prologue if prologue else "resmm",
    )(*a_args, w, *res_args, mod)


def _ffn_kernel(x_ref, g_ref, sh_ref, sc_ref, gate_ref, w1_ref, w2_ref, *rest, final_norm, n_casts):
    rest = list(rest)
    gf_ref = rest.pop(0) if final_norm else None
    next_in = [rest.pop(0) for _ in range(n_casts)]
    o_ref = rest.pop(0)
    next_out = [rest.pop(0) for _ in range(n_casts)]
    (h_ref,) = rest
    f = pl.program_id(1)
    for src, dst in zip(next_in, next_out):
        dst[...] = src[0].astype(BF16)

    def hidden_step(h):
        a = jnp.dot(h, w1_ref[...], preferred_element_type=F32)
        a = jnp.square(jnp.maximum(a, 0.0)).astype(BF16)
        return jnp.dot(a, w2_ref[...], preferred_element_type=F32)

    @pl.when(f == 0)
    def _():
        h = _modulated(x_ref[...], g_ref[...], sh_ref[0], sc_ref[0]).astype(BF16)
        h_ref[...] = h
        o_ref[...] = hidden_step(h)

    @pl.when(f > 0)
    def _():
        o_ref[...] += hidden_step(h_ref[...])

    @pl.when(f == pl.num_programs(1) - 1)
    def _():
        y = x_ref[...] + gate_ref[0] * o_ref[...]
        if final_norm:
            ms = jnp.mean(y * y, axis=-1, keepdims=True)
            y = y * lax.rsqrt(ms + NORM_EPS) * gf_ref[...]
        o_ref[...] = y


def _ffn_part(x, g, mod, w1, w2, *, tile_rows, first_tile, n_tiles, g_final=None, cast_next=()):
    final_norm = g_final is not None
    tf = FFN_TF
    nf = D_FF // tf
    single = pl.Buffered(1)
    in_specs = [
        pl.BlockSpec((tile_rows, D_MODEL), lambda i, f: (i + first_tile, 0)),
        pl.BlockSpec((1, D_MODEL), lambda i, f: (0, 0)),
        _mod_spec(3, tile_rows, first_tile),
        _mod_spec(4, tile_rows, first_tile),
        _mod_spec(5, tile_rows, first_tile),
        pl.BlockSpec((D_MODEL, tf), lambda i, f: (0, f)),
        pl.BlockSpec((tf, D_MODEL), lambda i, f: (f, 0)),
    ]
    args = [x, g.reshape(1, D_MODEL), mod, mod, mod, w1, w2]
    out_specs = [pl.BlockSpec((tile_rows, D_MODEL), lambda i, f: (i, 0), pipeline_mode=single)]
    out_shape = [jax.ShapeDtypeStruct((n_tiles * tile_rows, D_MODEL), F32)]
    if final_norm:
        in_specs.append(pl.BlockSpec((1, D_MODEL), lambda i, f: (0, 0)))
        args.append(g_final.reshape(1, D_MODEL))
    steps = n_tiles * nf
    for w_all, layer in cast_next:
        rows, cols = w_all.shape[1:]
        slab = rows // steps
        assert slab * steps == rows and slab % 16 == 0
        in_specs.append(pl.BlockSpec((1, slab, cols), lambda i, f, layer=layer: (layer, i * nf + f, 0)))
        args.append(w_all)
        out_specs.append(pl.BlockSpec((slab, cols), lambda i, f: (i * nf + f, 0)))
        out_shape.append(jax.ShapeDtypeStruct((rows, cols), BF16))
    results = pl.pallas_call(
        functools.partial(_ffn_kernel, final_norm=final_norm, n_casts=len(cast_next)),
        grid=(n_tiles, nf),
        in_specs=in_specs,
        out_specs=out_specs,
        out_shape=out_shape,
        scratch_shapes=[pltpu.VMEM((tile_rows, D_MODEL), BF16)],
        compiler_params=_params(("arbitrary", "arbitrary"), FFN_VMEM_LIMIT),
        name="ffn_final" if final_norm else "ffn",
    )(*args)
    return results[0], list(results[1:])


def _ffn(x, g, mod, w1, w2, *, with_context, g_final=None, cast_next=()):
    latent, casts = _ffn_part(x, g, mod, w1, w2, tile_rows=FFN_TM, first_tile=0, n_tiles=T_X // FFN_TM,
                              g_final=g_final, cast_next=cast_next)
    context = None
    if with_context:
        context, _ = _ffn_part(x, g, mod, w1, w2, tile_rows=T_C, first_tile=T_X // T_C, n_tiles=1)
    return latent, context, casts


def _dft_tables(n):
    idx = np.arange(n, dtype=np.int64)
    ang = 2.0 * np.pi * ((idx[:, None] * idx[None, :]) % n).astype(np.float64) / n
    return np.cos(ang) / math.sqrt(n), np.sin(ang) / math.sqrt(n)


def _channel_dft_matrix():
    c, s = _dft_tables(D_MODEL // FNO_GROUPS)
    return jnp.asarray(np.concatenate([c, s], axis=1), F32).astype(BF16)


def _chdft_kernel(*refs, n_x):
    x_refs, (g_ref, sh_ref, sc_ref, w_ref, sel_ref, oe_ref, oo_ref) = refs[:n_x], refs[n_x:]
    dg = D_MODEL // FNO_GROUPS
    w = w_ref[...]
    h_all = _modulated(_stream_tile(x_refs), g_ref[...], sh_ref[0], sc_ref[0]).astype(BF16)
    for parity, o_ref in enumerate((oe_ref, oo_ref)):
        h = jnp.dot(sel_ref[parity], h_all, preferred_element_type=F32).astype(BF16)
        for grp in range(FNO_GROUPS):
            cols = slice(grp * dg, (grp + 1) * dg)
            r = jnp.dot(h[:, cols], w, preferred_element_type=F32)
            o_ref[:, cols] = r[:, :dg].astype(o_ref.dtype)
            o_ref[:, D_MODEL + grp * dg:D_MODEL + (grp + 1) * dg] = r[:, dg:].astype(o_ref.dtype)


def _chdft(x, g, mod, w):
    dg = D_MODEL // FNO_GROUPS
    half = jax.ShapeDtypeStruct((T_ALL // 2, 2 * D_MODEL), BF16)
    x_specs, x_args = _stream_operand(x, TM, D_MODEL, NT_ALL)
    rows = np.arange(TM // 2)
    sel = np.zeros((2, TM // 2, TM), np.float32)
    sel[0, rows, 2 * rows] = 1.0
    sel[1, rows, 2 * rows + 1] = 1.0
    return pl.pallas_call(
        functools.partial(_chdft_kernel, n_x=len(x_args)),
        grid=(NT_ALL,),
        in_specs=x_specs + [
            pl.BlockSpec((1, D_MODEL), lambda i: (0, 0)),
            _mod_spec(0),
            _mod_spec(1),
            pl.BlockSpec((dg, 2 * dg), lambda i: (0, 0)),
            pl.BlockSpec((2, TM // 2, TM), lambda i: (0, 0, 0)),
        ],
        out_specs=[pl.BlockSpec((TM // 2, 2 * D_MODEL), lambda i: (i, 0))] * 2,
        out_shape=[half, half],
        compiler_params=_params(("arbitrary",)),
        name="chdft",
    )(*x_args, g.reshape(1, D_MODEL), mod, mod, w, jnp.asarray(sel).astype(BF16))


def _seq_dft_matrices(n, parity):
    half = n // 2
    kk = 2 * np.arange(half, dtype=np.int64) + parity
    if n <= 512:
        ang = 2.0 * np.pi * ((np.arange(half, dtype=np.int64)[:, None] * kk[None, :]) % n) / n
        return (jnp.asarray(np.cos(ang) / math.sqrt(n), F32).astype(BF16),
                jnp.asarray(-np.sin(ang) / math.sqrt(n), F32).astype(BF16))
    r = 64
    q = half // r
    ang_hi = 2.0 * np.pi * (((r * np.arange(q, dtype=np.int64))[:, None] * kk[None, :]) % n) / n
    ang_lo = 2.0 * np.pi * ((np.arange(r, dtype=np.int64)[:, None] * kk[None, :]) % n) / n
    ch = jnp.asarray(np.cos(ang_hi) / math.sqrt(n), F32)[:, None, :]
    sh = jnp.asarray(np.sin(ang_hi) / math.sqrt(n), F32)[:, None, :]
    cl = jnp.asarray(np.cos(ang_lo), F32)[None, :, :]
    sl = jnp.asarray(np.sin(ang_lo), F32)[None, :, :]
    c = (ch * cl - sh * sl).astype(BF16).reshape(half, half)
    s_neg = (-(sh * cl + ch * sl)).astype(BF16).reshape(half, half)
    return c, s_neg


def _seqdft_kernel(ce_ref, se_ref, co_ref, so_ref, ae_ref, be_ref, ao_ref, bo_ref, o_ref, acc_ref):
    kk = pl.program_id(3)

    @pl.when(kk == 0)
    def _():
        acc_ref[...] = jnp.zeros_like(acc_ref)

    acc_ref[0] += (jnp.dot(ce_ref[...], ae_ref[...], preferred_element_type=F32)
                   + jnp.dot(se_ref[...], be_ref[...], preferred_element_type=F32))
    acc_ref[1] += (jnp.dot(co_ref[...], ao_ref[...], preferred_element_type=F32)
                   + jnp.dot(so_ref[...], bo_ref[...], preferred_element_type=F32))

    @pl.when(kk == pl.num_programs(3) - 1)
    def _():
        o_ref[0] = (acc_ref[0] + acc_ref[1]).astype(o_ref.dtype)
        o_ref[1] = (acc_ref[0] - acc_ref[1]).astype(o_ref.dtype)


def _seqdft(ab_even, ab_odd, *, n, row0):
    half = n // 2
    tmm = min(half, 1024)
    tk = min(half, 512)
    tn = 1024
    nk = half // tk
    col_b = D_MODEL // tn
    mat = pl.BlockSpec((tmm, tk), lambda b, m, j, k: (m, k))
    a_spec = pl.BlockSpec((tk, tn), lambda b, m, j, k: (row0 // 2 // tk + b * nk + k, j))
    b_spec = pl.BlockSpec((tk, tn), lambda b, m, j, k: (row0 // 2 // tk + b * nk + k, col_b + j))
    out = pl.pallas_call(
        _seqdft_kernel,
        grid=(BATCH, half // tmm, D_MODEL // tn, nk),
        in_specs=[mat, mat, mat, mat, a_spec, b_spec, a_spec, b_spec],
        out_specs=pl.BlockSpec((2, tmm, tn), lambda b, m, j, k: (b, m, j)),
        out_shape=jax.ShapeDtypeStruct((BATCH * 2, half, D_MODEL), BF16),
        scratch_shapes=[pltpu.VMEM((2, tmm, tn), F32)],
        compiler_params=_params(("arbitrary",) * 4),
        name="seqdft",
    )(*_seq_dft_matrices(n, 0), *_seq_dft_matrices(n, 1), ab_even, ab_even, ab_odd, ab_odd)
    return out.reshape(BATCH * n, D_MODEL)


def _dot_nt(a, b):
    return lax.dot_general(a, b, (((1,), (1,)), ((), ())), preferred_element_type=F32)


ATTN_KEY_CHUNK = 256
ATTN_TQ = 512
ATTN_Q_SCALE = (DIFF_HEAD_DIM ** -0.5) * math.log2(math.e)


def _attn_kernel(lam_ref, g_ref, q_ref, *rest, latent, lam_init):
    if latent:
        kx_ref, vx_ref, kc_ref, vc_ref, o_ref, s_ref = rest
        chunks = [(kx_ref, vx_ref, r) for r in range(0, SEQ, ATTN_KEY_CHUNK)]
    else:
        kc_ref, vc_ref, o_ref, s_ref = rest
        chunks = []
    chunks += [(kc_ref, vc_ref, r) for r in range(0, CTX_LEN, ATTN_KEY_CHUNK)]
    hd = DIFF_HEAD_DIM
    kc = ATTN_KEY_CHUNK
    lv = lam_ref[...]
    lam = (jnp.exp(jnp.sum(lv[0:1] * lv[1:2], axis=-1, keepdims=True))
           - jnp.exp(jnp.sum(lv[2:3] * lv[3:4], axis=-1, keepdims=True)) + lam_init)
    q = q_ref[...]
    comps = (0, 1)
    qs = [q[:, c * hd:(c + 1) * hd] for c in comps]
    m = [None, None]
    for j, (k_ref, _, r0) in enumerate(chunks):
        for c in comps:
            s = _dot_nt(qs[c], k_ref[r0:r0 + kc, c * hd:(c + 1) * hd])
            s_ref[c, :, j * kc:(j + 1) * kc] = s
            sm = jnp.maximum(s[:, :LANE], s[:, LANE:])
            m[c] = sm if m[c] is None else jnp.maximum(m[c], sm)
    m = [jnp.max(mc, axis=-1, keepdims=True) for mc in m]
    l = [None, None]
    acc = [None, None]

    def exp_chunk(j, c, row_max):
        e = jnp.exp2(s_ref[c, :, j * kc:(j + 1) * kc] - row_max)
        ep = e[:, :LANE] + e[:, LANE:]
        l[c] = ep if l[c] is None else l[c] + ep
        return e.astype(BF16)

    ahead = 4
    n_chunks = len(chunks)
    ready = {(j, c): exp_chunk(j, c, m[c]) for j in range(min(ahead, n_chunks)) for c in comps}
    for j, (_, v_ref, r0) in enumerate(chunks):
        for c in comps:
            pv = jnp.dot(ready.pop((j, c)), v_ref[r0:r0 + kc, :], preferred_element_type=F32)
            acc[c] = pv if acc[c] is None else acc[c] + pv
            if j + ahead < n_chunks:
                zero = jnp.minimum(jnp.abs(pv[-1:, :1]), 0.0)
                ready[j + ahead, c] = exp_chunk(j + ahead, c, m[c] + zero)
    outs = [acc[c] * (1.0 / jnp.sum(l[c], axis=-1, keepdims=True)) for c in comps]
    o = outs[0] - lam * outs[1]
    ms = jnp.mean(o * o, axis=-1, keepdims=True)
    o = o * lax.rsqrt(ms + NORM_EPS) * g_ref[...] * (1.0 - lam_init)
    o_ref[...] = o.astype(o_ref.dtype)


def _diff_attention(qkv, lam_vecs, subln_g, lam_init, *, latent):
    tq = ATTN_TQ if latent else CTX_LEN
    hw = 2 * DIFF_HEAD_DIM
    k_col = D_MODEL // hw
    v_col = 2 * D_MODEL // hw
    n_q = SEQ if latent else CTX_LEN
    q_row0 = 0 if latent else T_X
    nq = n_q // tq
    in_specs = [
        pl.BlockSpec((4, DIFF_HEAD_DIM), lambda b, h, i: (0, 0)),
        pl.BlockSpec((1, hw), lambda b, h, i: (0, 0)),
        pl.BlockSpec((tq, hw), lambda b, h, i: (q_row0 // tq + b * nq + i, h)),
    ]
    args = [lam_vecs, subln_g.reshape(1, hw), qkv]
    if latent:
        in_specs += [
            pl.BlockSpec((SEQ, hw), lambda b, h, i: (b, k_col + h)),
            pl.BlockSpec((SEQ, hw), lambda b, h, i: (b, v_col + h)),
        ]
        args += [qkv, qkv]
    in_specs += [
        pl.BlockSpec((CTX_LEN, hw), lambda b, h, i: (T_X // CTX_LEN + b, k_col + h)),
        pl.BlockSpec((CTX_LEN, hw), lambda b, h, i: (T_X // CTX_LEN + b, v_col + h)),
    ]
    args += [qkv, qkv]
    return pl.pallas_call(
        functools.partial(_attn_kernel, latent=latent, lam_init=lam_init),
        grid=(BATCH, DIFF_HEADS, nq),
        in_specs=in_specs,
        out_specs=pl.BlockSpec((tq, hw), lambda b, h, i: (b * nq + i, h)),
        out_shape=jax.ShapeDtypeStruct((BATCH * n_q, D_MODEL), BF16),
        scratch_shapes=[pltpu.VMEM((2, tq, (SEQ if latent else 0) + CTX_LEN), F32)],
        compiler_params=_params(("arbitrary",) * 3),
        name="diffattn_latent" if latent else "diffattn_ctx",
    )(*args)


GLA_CTX_CHUNKS = CTX_LEN // GLA_CHUNK
GLA_X_CHUNKS = SEQ // GLA_CHUNK
GLA_STEPS = GLA_CTX_CHUNKS + GLA_X_CHUNKS


def _split3_bf16(a):
    hi = a.astype(BF16)
    r = a - hi.astype(F32)
    mid = r.astype(BF16)
    lo = (r - mid.astype(F32)).astype(BF16)
    return hi, mid, lo


def _gla_scan_kernel(qf_ref, kf_ref, vf_ref, gf_ref, qb_ref, kb_ref, vb_ref, gb_ref, wup_ref, bg_ref, tri_ref,
                     of_ref, ob_ref, st_ref):
    @pl.when(pl.program_id(1) == 0)
    def _():
        st_ref[...] = jnp.zeros_like(st_ref)

    dk, dv = GLA_DK_HEAD, GLA_DV_HEAD
    dirs = (0, 1)
    chains = [(d, h) for h in range(GLA_HEADS) for d in dirs]
    q_refs, k_refs, v_refs, g_refs, o_refs = (qf_ref, qb_ref), (kf_ref, kb_ref), (vf_ref, vb_ref), (gf_ref, gb_ref), (of_ref, ob_ref)
    tri = [tri_ref[d] for d in dirs]
    mask = [t.astype(F32) > 0.0 for t in tri]
    z = [jnp.dot(g_refs[d][...], wup_ref[d], preferred_element_type=F32) + bg_ref[d] for d in dirs]
    g = [(jnp.minimum(zd, 0.0) - jnp.log(1.0 + jnp.exp(-jnp.abs(zd)))) * (1.0 / GLA_TAU) for zd in z]
    parts = [_split3_bf16(gd) for gd in g]
    bcum = [sum(jnp.dot(tri[d], part, preferred_element_type=F32) for part in parts[d]) for d in dirs]
    blast = [jnp.sum(gd, axis=0, keepdims=True) for gd in g]
    q = [q_refs[d][...].astype(F32) * (dk ** -0.5) for d in dirs]
    k = [k_refs[d][...].astype(F32) for d in dirs]
    q_dec = [(q[d] * jnp.exp(bcum[d])).astype(BF16) for d in dirs]
    k_inv = [(k[d] * jnp.exp(-bcum[d])).astype(BF16) for d in dirs]
    k_tail = [(k[d] * jnp.exp(blast[d] - bcum[d])).astype(BF16) for d in dirs]
    decay = [jnp.exp(blast[d]) for d in dirs]
    ks = lambda h: slice(h * dk, (h + 1) * dk)
    vs = lambda h: slice(h * dv, (h + 1) * dv)
    st = {c: st_ref[c[0], c[1]] for c in chains}
    att = {(d, h): _dot_nt(q_dec[d][:, ks(h)], k_inv[d][:, ks(h)]) for d, h in chains}
    inter = {(d, h): _dot_nt(q_dec[d][:, ks(h)], st[d, h].astype(BF16)) for d, h in chains}
    att = {(d, h): jnp.where(mask[d], att[d, h], 0.0).astype(BF16) for d, h in chains}
    v = {(d, h): v_refs[d][:, vs(h)] for d, h in chains}
    upd = {c: lax.dot_general(v[c], k_tail[c[0]][:, ks(c[1])], (((0,), (0,)), ((), ())),
                              preferred_element_type=F32) for c in chains}
    out = {c: inter[c] + jnp.dot(att[c], v[c], preferred_element_type=F32) for c in chains}
    for d, h in chains:
        o_refs[d][:, vs(h)] = out[d, h].astype(o_refs[d].dtype)
        st_ref[d, h] = decay[d][:, ks(h)] * st[d, h] + upd[d, h]


def _gla_scan(proj, gates, wup_pad, b_gate, tri):
    def row_block(d):
        def index(b, s):
            c_idx = s if d == 0 else GLA_CTX_CHUNKS - 1 - s
            x_idx = s - GLA_CTX_CHUNKS if d == 0 else GLA_STEPS - 1 - s
            return jnp.where(s < GLA_CTX_CHUNKS,
                             T_X // GLA_CHUNK + b * GLA_CTX_CHUNKS + c_idx,
                             b * GLA_X_CHUNKS + x_idx)
        return index

    def chunk_specs(d):
        rb = row_block(d)
        return [
            pl.BlockSpec((GLA_CHUNK, GLA_DK), lambda b, s: (rb(b, s), 0)),
            pl.BlockSpec((GLA_CHUNK, GLA_DK), lambda b, s: (rb(b, s), 1)),
            pl.BlockSpec((GLA_CHUNK, GLA_DV), lambda b, s: (rb(b, s), 2 * GLA_DK // GLA_DV)),
            pl.BlockSpec((GLA_CHUNK, LANE), lambda b, s: (rb(b, s), 0)),
        ]

    whole = lambda shape: pl.BlockSpec(shape, lambda b, s: (0,) * len(shape))
    out_sds = jax.ShapeDtypeStruct((T_ALL, GLA_DV), BF16)
    return pl.pallas_call(
        _gla_scan_kernel,
        grid=(BATCH, GLA_STEPS),
        in_specs=chunk_specs(0) + chunk_specs(1) + [
            whole((2, LANE, GLA_DK)), whole((2, 1, GLA_DK)), whole((2, GLA_CHUNK, GLA_CHUNK))],
        out_specs=[pl.BlockSpec((GLA_CHUNK, GLA_DV), lambda b, s: (row_block(0)(b, s), 0)),
                   pl.BlockSpec((GLA_CHUNK, GLA_DV), lambda b, s: (row_block(1)(b, s), 0))],
        out_shape=[out_sds, out_sds],
        scratch_shapes=[pltpu.VMEM((2, GLA_HEADS, GLA_DV_HEAD, GLA_DK_HEAD), F32)],
        compiler_params=_params(("arbitrary",) * 2),
        name="gla_scan",
    )(proj, proj, proj, gates, proj, proj, proj, gates, wup_pad, b_gate, tri)


def _rope_tables():
    hd = DIFF_HEAD_DIM
    pos = np.arange(SEQ)
    row = (pos // GRID_W).astype(np.float32)
    col = (pos % GRID_W).astype(np.float32)
    n_freq = hd // 4
    inv = (ROPE_THETA ** (-np.arange(n_freq, dtype=np.float32) / n_freq)).astype(np.float32)
    ang = np.concatenate([row[:, None] * inv, col[:, None] * inv], axis=-1).astype(np.float64)
    cos, sin = np.cos(ang), np.sin(ang)
    cs = np.concatenate([cos, cos], axis=-1).astype(np.float32)
    sn = np.concatenate([-sin, sin], axis=-1).astype(np.float32)
    cs = np.concatenate([cs, np.ones((TM, 2 * cos.shape[1]), np.float32)], axis=0)
    sn = np.concatenate([sn, np.zeros((TM, 2 * cos.shape[1]), np.float32)], axis=0)
    return jnp.asarray(cs), jnp.asarray(sn)


def _gla_tri():
    lower = np.tril(np.ones((GLA_CHUNK, GLA_CHUNK), np.float32))
    return jnp.asarray(np.stack([lower, lower.T])).astype(BF16)


def _mixer(i, tok, mod, g_mix, w):
    kind = i % N_MIXERS
    if kind == 0:
        ab_even, ab_odd = _chdft(tok, g_mix, mod, _channel_dft_matrix())
        zx = _seqdft(ab_even, ab_odd, n=SEQ, row0=0)
        zc = _seqdft(ab_even, ab_odd, n=CTX_LEN, row0=T_X)
        return (zx, zc), w["fno_w_out"]
    if kind == 1:
        z = _modmm(tok, g_mix, mod, 0, 1, w["gmlp_w_in"].astype(BF16), epilogue="gelu")
        return (z, w["gmlp_ln_g"], w["gmlp_ln_b"], w["gmlp_w_s"].astype(BF16), w["gmlp_b_s"].T), w["gmlp_w_out"]
    if kind == 2:
        lam_init = 0.8 - 0.6 * math.exp(-0.3 * i)
        qkv = _modmm(tok, g_mix, mod, 0, 1, w["diff_w_in"].astype(BF16), epilogue="rope", rope=_rope_tables())
        ox = _diff_attention(qkv, w["diff_lambda"], w["diff_subln_g"], lam_init, latent=True)
        oc = _diff_attention(qkv, w["diff_lambda"], w["diff_subln_g"], lam_init, latent=False)
        return (ox, oc), w["diff_w_out"]
    n_main = 2 * GLA_DK + 2 * GLA_DV
    w_in = w["gla_w_in"]
    w_gate = jnp.pad(w_in[:, n_main:], ((0, 0), (0, LANE - 2 * GLA_GATE_RANK))).astype(BF16)
    proj, gates = _modmm(tok, g_mix, mod, 0, 1, w_in.astype(BF16), n_out=n_main, w_side=w_gate)
    wup = w["gla_w_gate_up"]
    wup_pad = jnp.zeros((2, LANE, GLA_DK), F32)
    wup_pad = wup_pad.at[0, :GLA_GATE_RANK].set(wup[0])
    wup_pad = wup_pad.at[1, GLA_GATE_RANK:2 * GLA_GATE_RANK].set(wup[1])
    o_fwd, o_bwd = _gla_scan(proj, gates, wup_pad.astype(BF16), w["gla_b_gate"].reshape(2, 1, GLA_DK), _gla_tri())
    return (o_fwd, o_bwd, proj, w["gla_norm_g"]), w["gla_w_out"]


def _layer(i, tok, mod, w, ffn_w, cast_next=(), g_final=None):
    last = i == DEPTH - 1
    mixed, w_out = _mixer(i, tok, mod, w["g_norm_mix"], w)
    joint = _resmm(mixed, w_out.astype(BF16), tok, mod, 2, n_tiles=NT_X if last else NT_ALL,
                   prologue={1: "sgu", 3: "gla"}.get(i % N_MIXERS))
    return _ffn(joint, w["g_norm_ffn"], mod, *ffn_w, with_context=not last, g_final=g_final, cast_next=cast_next)


def kernel(x, c, ctx, c_ctx, w_ada, b_ada, g_norm_mix, g_norm_ffn, w_ffn_in, w_ffn_out, g_final, fno_w_out, gmlp_w_in, gmlp_ln_g, gmlp_ln_b, gmlp_w_s, gmlp_b_s, gmlp_w_out, diff_w_in, diff_lambda, diff_subln_g, diff_w_out, gla_w_in, gla_w_gate_up, gla_b_gate, gla_norm_g, gla_w_out):
    tok = (x.reshape(T_X, D_MODEL), ctx.reshape(T_C, D_MODEL))
    c_rows = jnp.concatenate([c, c_ctx[None, :], jnp.zeros((MOD_ROWS - BATCH - 1, D_MODEL), F32)], axis=0)
    mod_all = _ada_table(c_rows, w_ada, b_ada).reshape(DEPTH, MOD_ROWS * N_ADA, 1, D_MODEL)
    mixer_weights = (
        dict(fno_w_out=fno_w_out),
        dict(gmlp_w_in=gmlp_w_in, gmlp_ln_g=gmlp_ln_g, gmlp_ln_b=gmlp_ln_b, gmlp_w_s=gmlp_w_s,
             gmlp_b_s=gmlp_b_s, gmlp_w_out=gmlp_w_out),
        dict(diff_w_in=diff_w_in, diff_lambda=diff_lambda, diff_subln_g=diff_subln_g, diff_w_out=diff_w_out),
        dict(gla_w_in=gla_w_in, gla_w_gate_up=gla_w_gate_up, gla_b_gate=gla_b_gate, gla_norm_g=gla_norm_g,
             gla_w_out=gla_w_out),
    )
    big_weights = (("fno_w_out",), ("gmlp_w_in", "gmlp_w_out"), ("diff_w_in", "diff_w_out"), ("gla_w_in", "gla_w_out"))
    ffn_w = (w_ffn_in[0].astype(BF16), w_ffn_out[0].astype(BF16))
    precast = {}
    for i in range(DEPTH):
        kind, j = i % N_MIXERS, i // N_MIXERS
        last = i == DEPTH - 1
        w = {name: arr[j] for name, arr in mixer_weights[kind].items()}
        w.update(precast)
        w.update(g_norm_mix=g_norm_mix[i], g_norm_ffn=g_norm_ffn[i])
        cast_next, next_names = (), ()
        if not last:
            next_kind, next_j = (i + 1) % N_MIXERS, (i + 1) // N_MIXERS
            next_names = big_weights[next_kind]
            cast_next = ((w_ffn_in, i + 1), (w_ffn_out, i + 1)) + tuple(
                (mixer_weights[next_kind][name], next_j) for name in next_names)
        latent, context, casts = _layer(i, tok, mod_all[i], w, ffn_w, cast_next=cast_next,
                                        g_final=g_final if last else None)
        tok = (latent, context)
        if not last:
            ffn_w = tuple(casts[:2])
            precast = dict(zip(next_names, casts[2:]))
    return latent.reshape(BATCH, SEQ, D_MODEL)
```

```python
import functools
import math

import numpy as np
import jax
import jax.numpy as jnp
from jax import lax
from jax.experimental import pallas as pl
from jax.experimental.pallas import tpu as pltpu

D_MODEL = 2048
BATCH = 2
SEQ = 4096
DEPTH = 4
GRID_W = 64
CTX_LEN = 256
N_MIXERS = 4
NORM_EPS = 1e-6
D_FF = 4 * D_MODEL
N_ADA = 6
FNO_GROUPS = 4
GMLP_CHUNK = 128
GMLP_HEADS = 16
GMLP_WIDTH = D_MODEL
DIFF_HEADS = 8
DIFF_HEAD_DIM = D_MODEL // DIFF_HEADS // 2
ROPE_THETA = 10000.0
GLA_HEADS = 4
GLA_DK = D_MODEL // 2
GLA_DV = D_MODEL
GLA_DK_HEAD = GLA_DK // GLA_HEADS
GLA_DV_HEAD = GLA_DV // GLA_HEADS
GLA_GATE_RANK = 16
GLA_TAU = 16.0
GLA_CHUNK = 64
GLA_IN_MAIN = 2 * GLA_DK + 2 * GLA_DV
LN_EPS = 1e-5

T_X = BATCH * SEQ
T_C = BATCH * CTX_LEN
T_ALL = T_X + T_C
TM = 512
NT_X = T_X // TM
NT_ALL = T_ALL // TM
TILES_PER_BATCH = SEQ // TM
MOD_ROWS = 16
CTX_GROUP = BATCH
LANE = 128
MM_TN = 2048
FFN_TF = 1024
FFN_TM = 1024

BF16 = jnp.bfloat16
F32 = jnp.float32
VMEM_LIMIT = 56 * 1024 * 1024
FFN_VMEM_LIMIT = 60 * 1024 * 1024


def _params(semantics, vmem_limit=VMEM_LIMIT):
    return pltpu.CompilerParams(dimension_semantics=semantics, vmem_limit_bytes=vmem_limit)


def _mod_spec(slot, tile_rows=TM, first_tile=0):
    def index(i, *_):
        group = jnp.minimum((i + first_tile) // (SEQ // tile_rows), CTX_GROUP)
        return group * N_ADA + slot, 0, 0
    return pl.BlockSpec((1, 1, D_MODEL), index)


def _modulated(x, g, shift, scale):
    ms = jnp.mean(x * x, axis=-1, keepdims=True)
    y = x * lax.rsqrt(ms + NORM_EPS) * g
    return y * (1.0 + scale) + shift


def _ada_kernel(c_ref, w_ref, b_ref, o_ref):
    c = c_ref[...]
    s = (c * jax.nn.sigmoid(c)).astype(BF16)
    w = w_ref[0].astype(BF16)
    o_ref[0] = jnp.dot(s, w, preferred_element_type=F32) + b_ref[0]


def _ada_table(c8, w_ada, b_ada):
    tn = 1024
    n_out = N_ADA * D_MODEL
    return pl.pallas_call(
        _ada_kernel,
        grid=(DEPTH, n_out // tn),
        in_specs=[
            pl.BlockSpec((MOD_ROWS, D_MODEL), lambda l, j: (0, 0)),
            pl.BlockSpec((1, D_MODEL, tn), lambda l, j: (l, 0, j)),
            pl.BlockSpec((1, 1, tn), lambda l, j: (l, 0, j)),
        ],
        out_specs=pl.BlockSpec((1, MOD_ROWS, tn), lambda l, j: (l, 0, j)),
        out_shape=jax.ShapeDtypeStruct((DEPTH, MOD_ROWS, n_out), F32),
        compiler_params=_params(("arbitrary", "arbitrary")),
        name="ada_table",
    )(c8, w_ada, b_ada.reshape(DEPTH, 1, n_out))


def _gelu_exact(a):
    return 0.5 * a * (1.0 + lax.erf(a * (1.0 / math.sqrt(2.0))))


def _modmm_kernel(*refs, n_x, epilogue, tn, n_steps, side):
    x_refs, (g_ref, sh_ref, sc_ref, w_ref, *rest) = refs[:n_x], refs[n_x:]
    if epilogue == "rope":
        cs_ref, sn_ref, o_ref, h_ref = rest
    elif side:
        ws_ref, o_ref, os_ref, h_ref = rest
    else:
        o_ref, h_ref = rest
    j = pl.program_id(1)

    def column_step(step):
        if step == 0:
            h = _modulated(_stream_tile(x_refs), g_ref[...], sh_ref[0], sc_ref[0]).astype(BF16)
            h_ref[...] = h
        else:
            h = h_ref[...]
        acc = jnp.dot(h, w_ref[...], preferred_element_type=F32)
        if epilogue == "gelu":
            o_ref[...] = _gelu_exact(acc).astype(o_ref.dtype)
        elif epilogue == "rope" and step < (2 * D_MODEL) // tn:
            if step < D_MODEL // tn:
                acc = acc * ATTN_Q_SCALE
            cs = cs_ref[...]
            sn = sn_ref[...]
            for blk in range(tn // LANE):
                a = acc[:, blk * LANE:(blk + 1) * LANE]
                swapped = pltpu.roll(a, LANE // 2, 1)
                o_ref[:, blk * LANE:(blk + 1) * LANE] = (a * cs + swapped * sn).astype(o_ref.dtype)
        else:
            o_ref[...] = acc.astype(o_ref.dtype)
        if step == 0 and side:
            os_ref[...] = jnp.dot(h, ws_ref[...], preferred_element_type=F32).astype(os_ref.dtype)

    if epilogue == "rope":
        for step in range(n_steps):
            pl.when(j == step)(functools.partial(column_step, step))
    else:
        pl.when(j == 0)(functools.partial(column_step, 0))
        pl.when(j > 0)(functools.partial(column_step, 1))


def _modmm(x, g, mod, slot_shift, slot_scale, w, *, n_out=None, epilogue="plain", rope=None, w_side=None,
           out_dtype=BF16):
    k = w.shape[0]
    n_out = w.shape[1] if n_out is None else n_out
    tn = min(MM_TN, n_out)
    x_specs, x_args = _stream_operand(x, TM, k, NT_ALL)
    in_specs = x_specs + [
        pl.BlockSpec((1, k), lambda i, j: (0, 0)),
        _mod_spec(slot_shift),
        _mod_spec(slot_scale),
        pl.BlockSpec((k, tn), lambda i, j: (0, j)),
    ]
    args = x_args + [g.reshape(1, k), mod, mod, w]
    if epilogue == "rope":
        pos = lambda i, j: (jnp.where(i < NT_X, i % TILES_PER_BATCH, TILES_PER_BATCH), 0)
        in_specs += [pl.BlockSpec((TM, LANE), pos), pl.BlockSpec((TM, LANE), pos)]
        args += list(rope)
    out_specs = [pl.BlockSpec((TM, tn), lambda i, j: (i, j))]
    out_shape = [jax.ShapeDtypeStruct((T_ALL, n_out), out_dtype)]
    if w_side is not None:
        assert epilogue == "plain" and w_side.shape == (k, LANE)
        in_specs.append(pl.BlockSpec((k, LANE), lambda i, j: (0, 0)))
        args.append(w_side)
        out_specs.append(pl.BlockSpec((TM, LANE), lambda i, j: (i, 0)))
        out_shape.append(jax.ShapeDtypeStruct((T_ALL, LANE), out_dtype))
    results = pl.pallas_call(
        functools.partial(_modmm_kernel, n_x=len(x_args), epilogue=epilogue, tn=tn, n_steps=n_out // tn,
                          side=w_side is not None),
        grid=(NT_ALL, n_out // tn),
        in_specs=in_specs,
        out_specs=out_specs,
        out_shape=out_shape,
        scratch_shapes=[pltpu.VMEM((TM, k), BF16)],
        compiler_params=_params(("arbitrary", "arbitrary")),
        name="modmm_" + epilogue,
    )(*args)
    return results[0] if w_side is None else tuple(results)


def _stream_operand(arr, rows, width, n_tiles):
    if isinstance(arr, tuple) and n_tiles == NT_X:
        arr = arr[0]
    if isinstance(arr, tuple):
        assert arr[1].shape == (rows, width)
        return ([pl.BlockSpec((rows, width), lambda i, *_: (jnp.minimum(i, NT_X - 1), 0)),
                 pl.BlockSpec((rows, width), lambda i, *_: (0, 0))], list(arr))
    return [pl.BlockSpec((rows, width), lambda i, *_: (i, 0))], [arr]


def _stream_tile(refs):
    if len(refs) == 2:
        return jnp.where(pl.program_id(0) < NT_X, refs[0][...], refs[1][...])
    return refs[0][...]


def _gla_gated(of_ref, ob_ref, r_ref, g_ref):
    o = of_ref[...].astype(F32) + ob_ref[...].astype(F32)
    r = r_ref[...].astype(F32)
    gate = r * jax.nn.sigmoid(r)
    dv = GLA_DV_HEAD
    heads = []
    for h in range(GLA_HEADS):
        oh = o[:, h * dv:(h + 1) * dv]
        ms = jnp.mean(oh * oh, axis=-1, keepdims=True)
        heads.append((oh * lax.rsqrt(ms + NORM_EPS) * g_ref[...] * gate[:, h * dv:(h + 1) * dv]).astype(BF16))
    return jnp.concatenate(heads, axis=-1)


def _sgu_gated(u_ref, v_ref, lg_ref, lb_ref, ws_ref, bs_ref, a_ref):
    v = v_ref[...].astype(F32)
    mu = jnp.mean(v, axis=-1, keepdims=True)
    vc = v - mu
    var = jnp.mean(vc * vc, axis=-1, keepdims=True)
    vn = (vc * lax.rsqrt(var + LN_EPS) * lg_ref[...] + lb_ref[...]).astype(BF16)
    we = GMLP_WIDTH // GMLP_HEADS
    for c in range(TM // GMLP_CHUNK):
        r0 = c * GMLP_CHUNK
        for h in range(GMLP_HEADS):
            s = jnp.dot(ws_ref[h], vn[r0:r0 + GMLP_CHUNK, h * we:(h + 1) * we], preferred_element_type=F32)
            s = s + bs_ref[:, h:h + 1]
            u = u_ref[r0:r0 + GMLP_CHUNK, h * we:(h + 1) * we].astype(F32)
            a_ref[r0:r0 + GMLP_CHUNK, h * we:(h + 1) * we] = (u * s).astype(a_ref.dtype)
    return a_ref[...]


def _resmm_kernel(*refs, n_a, n_res, prologue):
    a_refs, refs = refs[:n_a], refs[n_a:]
    w_ref, refs = refs[0], refs[1:]
    res_refs, (gate_ref, o_ref, *scratch) = refs[:n_res], refs[n_res:]
    if prologue == "gla":
        a = _gla_gated(*a_refs)
    elif prologue == "sgu":
        a = _sgu_gated(*a_refs, *scratch)
    else:
        a = _stream_tile(a_refs)
    acc = jnp.dot(a, w_ref[...], preferred_element_type=F32)
    o_ref[...] = _stream_tile(res_refs) + gate_ref[0] * acc


def _resmm(a, w, res, mod, slot_gate, *, n_tiles, prologue=None):
    k, n_out = w.shape
    whole = lambda shape: pl.BlockSpec(shape, lambda i: (0,) * len(shape))
    scratch = []
    if prologue == "gla":
        o_fwd, o_bwd, proj, norm_g = a
        r_col = (2 * GLA_DK + GLA_DV) // GLA_DV
        a_specs = [pl.BlockSpec((TM, GLA_DV), lambda i: (i, 0)),
                   pl.BlockSpec((TM, GLA_DV), lambda i: (i, 0)),
                   pl.BlockSpec((TM, GLA_DV), lambda i: (i, r_col)),
                   whole((1, GLA_DV_HEAD))]
        a_args = [o_fwd, o_bwd, proj, norm_g.reshape(1, -1)]
    elif prologue == "sgu":
        z, ln_g, ln_b, w_s, b_s_t = a
        a_specs = [pl.BlockSpec((TM, GMLP_WIDTH), lambda i: (i, 0)),
                   pl.BlockSpec((TM, GMLP_WIDTH), lambda i: (i, 1)),
                   whole((1, GMLP_WIDTH)), whole((1, GMLP_WIDTH)),
                   whole((GMLP_HEADS, GMLP_CHUNK, GMLP_CHUNK)), whole((GMLP_CHUNK, GMLP_HEADS))]
        a_args = [z, z, ln_g.reshape(1, -1), ln_b.reshape(1, -1), w_s, b_s_t]
        scratch = [pltpu.VMEM((TM, GMLP_WIDTH), BF16)]
    else:
        a_specs, a_args = _stream_operand(a, TM, k, n_tiles)
    res_specs, res_args = _stream_operand(res, TM, n_out, n_tiles)
    return pl.pallas_call(
        functools.partial(_resmm_kernel, n_a=len(a_args), n_res=len(res_args), prologue=prologue),
        grid=(n_tiles,),
        in_specs=a_specs + [pl.BlockSpec((k, n_out), lambda i: (0, 0))] + res_specs + [_mod_spec(slot_gate)],
        out_specs=pl.BlockSpec((TM, n_out), lambda i: (i, 0)),
        out_shape=jax.ShapeDtypeStruct((n_tiles * TM, n_out), F32),
        scratch_shapes=scratch,
        compiler_params=_params(("arbitrary",)),
        name="resmm_" + prologue if prologue else "resmm",
    )(*a_args, w, *res_args, mod)


def _ffn_kernel(x_ref, g_ref, sh_ref, sc_ref, gate_ref, w1_ref, w2_ref, *rest, final_norm, n_casts):
    rest = list(rest)
    gf_ref = rest.pop(0) if final_norm else None
    next_in = [rest.pop(0) for _ in range(n_casts)]
    o_ref = rest.pop(0)
    next_out = [rest.pop(0) for _ in range(n_casts)]
    (h_ref,) = rest
    f = pl.program_id(1)
    for src, dst in zip(next_in, next_out):
        dst[...] = src[0].astype(BF16)

    def hidden_step(h):
        a = jnp.dot(h, w1_ref[...], preferred_element_type=F32)
        a = jnp.square(jnp.maximum(a, 0.0)).astype(BF16)
        return jnp.dot(a, w2_ref[...], preferred_element_type=F32)

    @pl.when(f == 0)
    def _():
        h = _modulated(x_ref[...], g_ref[...], sh_ref[0], sc_ref[0]).astype(BF16)
        h_ref[...] = h
        o_ref[...] = hidden_step(h)

    @pl.when(f > 0)
    def _():
        o_ref[...] += hidden_step(h_ref[...])

    @pl.when(f == pl.num_programs(1) - 1)
    def _():
        y = x_ref[...] + gate_ref[0] * o_ref[...]
        if final_norm:
            ms = jnp.mean(y * y, axis=-1, keepdims=True)
            y = y * lax.rsqrt(ms + NORM_EPS) * gf_ref[...]
        o_ref[...] = y


def _ffn_part(x, g, mod, w1, w2, *, tile_rows, first_tile, n_tiles, g_final=None, cast_next=()):
    final_norm = g_final is not None
    tf = FFN_TF
    nf = D_FF // tf
    single = pl.Buffered(1)
    in_specs = [
        pl.BlockSpec((tile_rows, D_MODEL), lambda i, f: (i + first_tile, 0)),
        pl.BlockSpec((1, D_MODEL), lambda i, f: (0, 0)),
        _mod_spec(3, tile_rows, first_tile),
        _mod_spec(4, tile_rows, first_tile),
        _mod_spec(5, tile_rows, first_tile),
        pl.BlockSpec((D_MODEL, tf), lambda i, f: (0, f)),
        pl.BlockSpec((tf, D_MODEL), lambda i, f: (f, 0)),
    ]
    args = [x, g.reshape(1, D_MODEL), mod, mod, mod, w1, w2]
    out_specs = [pl.BlockSpec((tile_rows, D_MODEL), lambda i, f: (i, 0), pipeline_mode=single)]
    out_shape = [jax.ShapeDtypeStruct((n_tiles * tile_rows, D_MODEL), F32)]
    if final_norm:
        in_specs.append(pl.BlockSpec((1, D_MODEL), lambda i, f: (0, 0)))
        args.append(g_final.reshape(1, D_MODEL))
    steps = n_tiles * nf
    for w_all, layer in cast_next:
        rows, cols = w_all.shape[1:]
        slab = rows // steps
        assert slab * steps == rows and slab % 16 == 0
        in_specs.append(pl.BlockSpec((1, slab, cols), lambda i, f, layer=layer: (layer, i * nf + f, 0)))
        args.append(w_all)
        out_specs.append(pl.BlockSpec((slab, cols), lambda i, f: (i * nf + f, 0)))
        out_shape.append(jax.ShapeDtypeStruct((rows, cols), BF16))
    results = pl.pallas_call(
        functools.partial(_ffn_kernel, final_norm=final_norm, n_casts=len(cast_next)),
        grid=(n_tiles, nf),
        in_specs=in_specs,
        out_specs=out_specs,
        out_shape=out_shape,
        scratch_shapes=[pltpu.VMEM((tile_rows, D_MODEL), BF16)],
        compiler_params=_params(("arbitrary", "arbitrary"), FFN_VMEM_LIMIT),
        name="ffn_final" if final_norm else "ffn",
    )(*args)
    return results[0], list(results[1:])


def _ffn(x, g, mod, w1, w2, *, with_context, g_final=None, cast_next=()):
    latent, casts = _ffn_part(x, g, mod, w1, w2, tile_rows=FFN_TM, first_tile=0, n_tiles=T_X // FFN_TM,
                              g_final=g_final, cast_next=cast_next)
    context = None
    if with_context:
        context, _ = _ffn_part(x, g, mod, w1, w2, tile_rows=T_C, first_tile=T_X // T_C, n_tiles=1)
    return latent, context, casts


def _dft_tables(n):
    idx = np.arange(n, dtype=np.int64)
    ang = 2.0 * np.pi * ((idx[:, None] * idx[None, :]) % n).astype(np.float64) / n
    return np.cos(ang) / math.sqrt(n), np.sin(ang) / math.sqrt(n)


def _channel_dft_matrix():
    c, s = _dft_tables(D_MODEL // FNO_GROUPS)
    return jnp.asarray(np.concatenate([c, s], axis=1), F32).astype(BF16)


def _chdft_kernel(*refs, n_x):
    x_refs, (g_ref, sh_ref, sc_ref, w_ref, sel_ref, oe_ref, oo_ref) = refs[:n_x], refs[n_x:]
    dg = D_MODEL // FNO_GROUPS
    w = w_ref[...]
    h_all = _modulated(_stream_tile(x_refs), g_ref[...], sh_ref[0], sc_ref[0]).astype(BF16)
    for parity, o_ref in enumerate((oe_ref, oo_ref)):
        h = jnp.dot(sel_ref[parity], h_all, preferred_element_type=F32).astype(BF16)
        for grp in range(FNO_GROUPS):
            cols = slice(grp * dg, (grp + 1) * dg)
            r = jnp.dot(h[:, cols], w, preferred_element_type=F32)
            o_ref[:, cols] = r[:, :dg].astype(o_ref.dtype)
            o_ref[:, D_MODEL + grp * dg:D_MODEL + (grp + 1) * dg] = r[:, dg:].astype(o_ref.dtype)


def _chdft(x, g, mod, w):
    dg = D_MODEL // FNO_GROUPS
    half = jax.ShapeDtypeStruct((T_ALL // 2, 2 * D_MODEL), BF16)
    x_specs, x_args = _stream_operand(x, TM, D_MODEL, NT_ALL)
    rows = np.arange(TM // 2)
    sel = np.zeros((2, TM // 2, TM), np.float32)
    sel[0, rows, 2 * rows] = 1.0
    sel[1, rows, 2 * rows + 1] = 1.0
    return pl.pallas_call(
        functools.partial(_chdft_kernel, n_x=len(x_args)),
        grid=(NT_ALL,),
        in_specs=x_specs + [
            pl.BlockSpec((1, D_MODEL), lambda i: (0, 0)),
            _mod_spec(0),
            _mod_spec(1),
            pl.BlockSpec((dg, 2 * dg), lambda i: (0, 0)),
            pl.BlockSpec((2, TM // 2, TM), lambda i: (0, 0, 0)),
        ],
        out_specs=[pl.BlockSpec((TM // 2, 2 * D_MODEL), lambda i: (i, 0))] * 2,
        out_shape=[half, half],
        compiler_params=_params(("arbitrary",)),
        name="chdft",
    )(*x_args, g.reshape(1, D_MODEL), mod, mod, w, jnp.asarray(sel).astype(BF16))


def _seq_dft_matrices(n, parity):
    half = n // 2
    kk = 2 * np.arange(half, dtype=np.int64) + parity
    if n <= 512:
        ang = 2.0 * np.pi * ((np.arange(half, dtype=np.int64)[:, None] * kk[None, :]) % n) / n
        return (jnp.asarray(np.cos(ang) / math.sqrt(n), F32).astype(BF16),
                jnp.asarray(-np.sin(ang) / math.sqrt(n), F32).astype(BF16))
    r = 64
    q = half // r
    ang_hi = 2.0 * np.pi * (((r * np.arange(q, dtype=np.int64))[:, None] * kk[None, :]) % n) / n
    ang_lo = 2.0 * np.pi * ((np.arange(r, dtype=np.int64)[:, None] * kk[None, :]) % n) / n
    ch = jnp.asarray(np.cos(ang_hi) / math.sqrt(n), F32)[:, None, :]
    sh = jnp.asarray(np.sin(ang_hi) / math.sqrt(n), F32)[:, None, :]
    cl = jnp.asarray(np.cos(ang_lo), F32)[None, :, :]
    sl = jnp.asarray(np.sin(ang_lo), F32)[None, :, :]
    c = (ch * cl - sh * sl).astype(BF16).reshape(half, half)
    s_neg = (-(sh * cl + ch * sl)).astype(BF16).reshape(half, half)
    return c, s_neg


def _seqdft_kernel(ce_ref, se_ref, co_ref, so_ref, ae_ref, be_ref, ao_ref, bo_ref, o_ref, acc_ref):
    kk = pl.program_id(3)

    @pl.when(kk == 0)
    def _():
        acc_ref[...] = jnp.zeros_like(acc_ref)

    acc_ref[0] += (jnp.dot(ce_ref[...], ae_ref[...], preferred_element_type=F32)
                   + jnp.dot(se_ref[...], be_ref[...], preferred_element_type=F32))
    acc_ref[1] += (jnp.dot(co_ref[...], ao_ref[...], preferred_element_type=F32)
                   + jnp.dot(so_ref[...], bo_ref[...], preferred_element_type=F32))

    @pl.when(kk == pl.num_programs(3) - 1)
    def _():
        o_ref[0] = (acc_ref[0] + acc_ref[1]).astype(o_ref.dtype)
        o_ref[1] = (acc_ref[0] - acc_ref[1]).astype(o_ref.dtype)


def _seqdft(ab_even, ab_odd, *, n, row0):
    half = n // 2
    tmm = min(half, 1024)
    tk = min(half, 1024)
    tn = 1024
    nk = half // tk
    col_b = D_MODEL // tn
    mat = pl.BlockSpec((tmm, tk), lambda b, m, j, k: (m, k))
    a_spec = pl.BlockSpec((tk, tn), lambda b, m, j, k: (row0 // 2 // tk + b * nk + k, j))
    b_spec = pl.BlockSpec((tk, tn), lambda b, m, j, k: (row0 // 2 // tk + b * nk + k, col_b + j))
    out = pl.pallas_call(
        _seqdft_kernel,
        grid=(BATCH, half // tmm, D_MODEL // tn, nk),
        in_specs=[mat, mat, mat, mat, a_spec, b_spec, a_spec, b_spec],
        out_specs=pl.BlockSpec((2, tmm, tn), lambda b, m, j, k: (b, m, j)),
        out_shape=jax.ShapeDtypeStruct((BATCH * 2, half, D_MODEL), BF16),
        scratch_shapes=[pltpu.VMEM((2, tmm, tn), F32)],
        compiler_params=_params(("arbitrary",) * 4),
        name="seqdft",
    )(*_seq_dft_matrices(n, 0), *_seq_dft_matrices(n, 1), ab_even, ab_even, ab_odd, ab_odd)
    return out.reshape(BATCH * n, D_MODEL)


def _dot_nt(a, b):
    return lax.dot_general(a, b, (((1,), (1,)), ((), ())), preferred_element_type=F32)


ATTN_KEY_CHUNK = 256
ATTN_TQ = 512
ATTN_Q_SCALE = (DIFF_HEAD_DIM ** -0.5) * math.log2(math.e)


def _attn_kernel(lam_ref, g_ref, q_ref, *rest, latent, lam_init):
    if latent:
        kx_ref, vx_ref, kc_ref, vc_ref, o_ref, s_ref = rest
        chunks = [(kx_ref, vx_ref, r) for r in range(0, SEQ, ATTN_KEY_CHUNK)]
    else:
        kc_ref, vc_ref, o_ref, s_ref = rest
        chunks = []
    chunks += [(kc_ref, vc_ref, r) for r in range(0, CTX_LEN, ATTN_KEY_CHUNK)]
    hd = DIFF_HEAD_DIM
    kc = ATTN_KEY_CHUNK
    lv = lam_ref[...]
    lam = (jnp.exp(jnp.sum(lv[0:1] * lv[1:2], axis=-1, keepdims=True))
           - jnp.exp(jnp.sum(lv[2:3] * lv[3:4], axis=-1, keepdims=True)) + lam_init)
    q = q_ref[...]
    comps = (0, 1)
    qs = [q[:, c * hd:(c + 1) * hd] for c in comps]
    m = [None, None]
    for j, (k_ref, _, r0) in enumerate(chunks):
        for c in comps:
            s = _dot_nt(qs[c], k_ref[r0:r0 + kc, c * hd:(c + 1) * hd])
            s_ref[c, :, j * kc:(j + 1) * kc] = s
            sm = jnp.maximum(s[:, :LANE], s[:, LANE:])
            m[c] = sm if m[c] is None else jnp.maximum(m[c], sm)
    m = [jnp.max(mc, axis=-1, keepdims=True) for mc in m]
    l = [None, None]
    acc = [None, None]

    def exp_chunk(j, c, row_max):
        e = jnp.exp2(s_ref[c, :, j * kc:(j + 1) * kc] - row_max)
        ep = e[:, :LANE] + e[:, LANE:]
        l[c] = ep if l[c] is None else l[c] + ep
        return e.astype(BF16)

    ahead = 4
    n_chunks = len(chunks)
    ready = {(j, c): exp_chunk(j, c, m[c]) for j in range(min(ahead, n_chunks)) for c in comps}
    for j, (_, v_ref, r0) in enumerate(chunks):
        for c in comps:
            pv = jnp.dot(ready.pop((j, c)), v_ref[r0:r0 + kc, :], preferred_element_type=F32)
            acc[c] = pv if acc[c] is None else acc[c] + pv
            if j + ahead < n_chunks:
                zero = jnp.minimum(jnp.abs(pv[-1:, :1]), 0.0)
                ready[j + ahead, c] = exp_chunk(j + ahead, c, m[c] + zero)
    outs = [acc[c] * (1.0 / jnp.sum(l[c], axis=-1, keepdims=True)) for c in comps]
    o = outs[0] - lam * outs[1]
    ms = jnp.mean(o * o, axis=-1, keepdims=True)
    o = o * lax.rsqrt(ms + NORM_EPS) * g_ref[...] * (1.0 - lam_init)
    o_ref[...] = o.astype(o_ref.dtype)


def _diff_attention(qkv, lam_vecs, subln_g, lam_init, *, latent):
    tq = ATTN_TQ if latent else CTX_LEN
    hw = 2 * DIFF_HEAD_DIM
    k_col = D_MODEL // hw
    v_col = 2 * D_MODEL // hw
    n_q = SEQ if latent else CTX_LEN
    q_row0 = 0 if latent else T_X
    nq = n_q // tq
    in_specs = [
        pl.BlockSpec((4, DIFF_HEAD_DIM), lambda b, h, i: (0, 0)),
        pl.BlockSpec((1, hw), lambda b, h, i: (0, 0)),
        pl.BlockSpec((tq, hw), lambda b, h, i: (q_row0 // tq + b * nq + i, h)),
    ]
    args = [lam_vecs, subln_g.reshape(1, hw), qkv]
    if latent:
        in_specs += [
            pl.BlockSpec((SEQ, hw), lambda b, h, i: (b, k_col + h)),
            pl.BlockSpec((SEQ, hw), lambda b, h, i: (b, v_col + h)),
        ]
        args += [qkv, qkv]
    in_specs += [
        pl.BlockSpec((CTX_LEN, hw), lambda b, h, i: (T_X // CTX_LEN + b, k_col + h)),
        pl.BlockSpec((CTX_LEN, hw), lambda b, h, i: (T_X // CTX_LEN + b, v_col + h)),
    ]
    args += [qkv, qkv]
    return pl.pallas_call(
        functools.partial(_attn_kernel, latent=latent, lam_init=lam_init),
        grid=(BATCH, DIFF_HEADS, nq),
        in_specs=in_specs,
        out_specs=pl.BlockSpec((tq, hw), lambda b, h, i: (b * nq + i, h)),
        out_shape=jax.ShapeDtypeStruct((BATCH * n_q, D_MODEL), BF16),
        scratch_shapes=[pltpu.VMEM((2, tq, (SEQ if latent else 0) + CTX_LEN), F32)],
        compiler_params=_params(("arbitrary",) * 3),
        name="diffattn_latent" if latent else "diffattn_ctx",
    )(*args)


GLA_CTX_CHUNKS = CTX_LEN // GLA_CHUNK
GLA_X_CHUNKS = SEQ // GLA_CHUNK
GLA_STEPS = GLA_CTX_CHUNKS + GLA_X_CHUNKS


def _split3_bf16(a):
    hi = a.astype(BF16)
    r = a - hi.astype(F32)
    mid = r.astype(BF16)
    lo = (r - mid.astype(F32)).astype(BF16)
    return hi, mid, lo


def _gla_scan_kernel(qf_ref, kf_ref, vf_ref, gf_ref, qb_ref, kb_ref, vb_ref, gb_ref, wup_ref, bg_ref, tri_ref,
                     of_ref, ob_ref, st_ref):
    @pl.when(pl.program_id(1) == 0)
    def _():
        st_ref[...] = jnp.zeros_like(st_ref)

    dk, dv = GLA_DK_HEAD, GLA_DV_HEAD
    dirs = (0, 1)
    chains = [(d, h) for h in range(GLA_HEADS) for d in dirs]
    q_refs, k_refs, v_refs, g_refs, o_refs = (qf_ref, qb_ref), (kf_ref, kb_ref), (vf_ref, vb_ref), (gf_ref, gb_ref), (of_ref, ob_ref)
    tri = [tri_ref[d] for d in dirs]
    mask = [t.astype(F32) > 0.0 for t in tri]
    z = [jnp.dot(g_refs[d][...], wup_ref[d], preferred_element_type=F32) + bg_ref[d] for d in dirs]
    g = [(jnp.minimum(zd, 0.0) - jnp.log(1.0 + jnp.exp(-jnp.abs(zd)))) * (1.0 / GLA_TAU) for zd in z]
    parts = [_split3_bf16(gd) for gd in g]
    bcum = [sum(jnp.dot(tri[d], part, preferred_element_type=F32) for part in parts[d]) for d in dirs]
    blast = [jnp.sum(gd, axis=0, keepdims=True) for gd in g]
    q = [q_refs[d][...].astype(F32) * (dk ** -0.5) for d in dirs]
    k = [k_refs[d][...].astype(F32) for d in dirs]
    q_dec = [(q[d] * jnp.exp(bcum[d])).astype(BF16) for d in dirs]
    k_inv = [(k[d] * jnp.exp(-bcum[d])).astype(BF16) for d in dirs]
    k_tail = [(k[d] * jnp.exp(blast[d] - bcum[d])).astype(BF16) for d in dirs]
    decay = [jnp.exp(blast[d]) for d in dirs]
    ks = lambda h: slice(h * dk, (h + 1) * dk)
    vs = lambda h: slice(h * dv, (h + 1) * dv)
    st = {c: st_ref[c[0], c[1]] for c in chains}
    att = {(d, h): _dot_nt(q_dec[d][:, ks(h)], k_inv[d][:, ks(h)]) for d, h in chains}
    inter = {(d, h): _dot_nt(q_dec[d][:, ks(h)], st[d, h].astype(BF16)) for d, h in chains}
    att = {(d, h): jnp.where(mask[d], att[d, h], 0.0).astype(BF16) for d, h in chains}
    v = {(d, h): v_refs[d][:, vs(h)] for d, h in chains}
    upd = {c: lax.dot_general(v[c], k_tail[c[0]][:, ks(c[1])], (((0,), (0,)), ((), ())),
                              preferred_element_type=F32) for c in chains}
    out = {c: inter[c] + jnp.dot(att[c], v[c], preferred_element_type=F32) for c in chains}
    for d, h in chains:
        o_refs[d][:, vs(h)] = out[d, h].astype(o_refs[d].dtype)
        st_ref[d, h] = decay[d][:, ks(h)] * st[d, h] + upd[d, h]


def _gla_scan(proj, gates, wup_pad, b_gate, tri):
    def row_block(d):
        def index(b, s):
            c_idx = s if d == 0 else GLA_CTX_CHUNKS - 1 - s
            x_idx = s - GLA_CTX_CHUNKS if d == 0 else GLA_STEPS - 1 - s
            return jnp.where(s < GLA_CTX_CHUNKS,
                             T_X // GLA_CHUNK + b * GLA_CTX_CHUNKS + c_idx,
                             b * GLA_X_CHUNKS + x_idx)
        return index

    def chunk_specs(d):
        rb = row_block(d)
        return [
            pl.BlockSpec((GLA_CHUNK, GLA_DK), lambda b, s: (rb(b, s), 0)),
            pl.BlockSpec((GLA_CHUNK, GLA_DK), lambda b, s: (rb(b, s), 1)),
            pl.BlockSpec((GLA_CHUNK, GLA_DV), lambda b, s: (rb(b, s), 2 * GLA_DK // GLA_DV)),
            pl.BlockSpec((GLA_CHUNK, LANE), lambda b, s: (rb(b, s), 0)),
        ]

    whole = lambda shape: pl.BlockSpec(shape, lambda b, s: (0,) * len(shape))
    out_sds = jax.ShapeDtypeStruct((T_ALL, GLA_DV), BF16)
    return pl.pallas_call(
        _gla_scan_kernel,
        grid=(BATCH, GLA_STEPS),
        in_specs=chunk_specs(0) + chunk_specs(1) + [
            whole((2, LANE, GLA_DK)), whole((2, 1, GLA_DK)), whole((2, GLA_CHUNK, GLA_CHUNK))],
        out_specs=[pl.BlockSpec((GLA_CHUNK, GLA_DV), lambda b, s: (row_block(0)(b, s), 0)),
                   pl.BlockSpec((GLA_CHUNK, GLA_DV), lambda b, s: (row_block(1)(b, s), 0))],
        out_shape=[out_sds, out_sds],
        scratch_shapes=[pltpu.VMEM((2, GLA_HEADS, GLA_DV_HEAD, GLA_DK_HEAD), F32)],
        compiler_params=_params(("arbitrary",) * 2),
        name="gla_scan",
    )(proj, proj, proj, gates, proj, proj, proj, gates, wup_pad, b_gate, tri)


def _rope_tables():
    hd = DIFF_HEAD_DIM
    pos = np.arange(SEQ)
    row = (pos // GRID_W).astype(np.float32)
    col = (pos % GRID_W).astype(np.float32)
    n_freq = hd // 4
    inv = (ROPE_THETA ** (-np.arange(n_freq, dtype=np.float32) / n_freq)).astype(np.float32)
    ang = np.concatenate([row[:, None] * inv, col[:, None] * inv], axis=-1).astype(np.float64)
    cos, sin = np.cos(ang), np.sin(ang)
    cs = np.concatenate([cos, cos], axis=-1).astype(np.float32)
    sn = np.concatenate([-sin, sin], axis=-1).astype(np.float32)
    cs = np.concatenate([cs, np.ones((TM, 2 * cos.shape[1]), np.float32)], axis=0)
    sn = np.concatenate([sn, np.zeros((TM, 2 * cos.shape[1]), np.float32)], axis=0)
    return jnp.asarray(cs), jnp.asarray(sn)


def _gla_tri():
    lower = np.tril(np.ones((GLA_CHUNK, GLA_CHUNK), np.float32))
    return jnp.asarray(np.stack([lower, lower.T])).astype(BF16)


def _mixer(i, tok, mod, g_mix, w):
    kind = i % N_MIXERS
    if kind == 0:
        ab_even, ab_odd = _chdft(tok, g_mix, mod, _channel_dft_matrix())
        zx = _seqdft(ab_even, ab_odd, n=SEQ, row0=0)
        zc = _seqdft(ab_even, ab_odd, n=CTX_LEN, row0=T_X)
        return (zx, zc), w["fno_w_out"]
    if kind == 1:
        z = _modmm(tok, g_mix, mod, 0, 1, w["gmlp_w_in"].astype(BF16), epilogue="gelu")
        return (z, w["gmlp_ln_g"], w["gmlp_ln_b"], w["gmlp_w_s"].astype(BF16), w["gmlp_b_s"].T), w["gmlp_w_out"]
    if kind == 2:
        lam_init = 0.8 - 0.6 * math.exp(-0.3 * i)
        qkv = _modmm(tok, g_mix, mod, 0, 1, w["diff_w_in"].astype(BF16), epilogue="rope", rope=_rope_tables())
        ox = _diff_attention(qkv, w["diff_lambda"], w["diff_subln_g"], lam_init, latent=True)
        oc = _diff_attention(qkv, w["diff_lambda"], w["diff_subln_g"], lam_init, latent=False)
        return (ox, oc), w["diff_w_out"]
    w_gate = jnp.pad(w["gla_w_gate"], ((0, 0), (0, LANE - 2 * GLA_GATE_RANK))).astype(BF16)
    proj, gates = _modmm(tok, g_mix, mod, 0, 1, w["gla_w_main"].astype(BF16), w_side=w_gate)
    wup = w["gla_w_gate_up"]
    wup_pad = jnp.zeros((2, LANE, GLA_DK), F32)
    wup_pad = wup_pad.at[0, :GLA_GATE_RANK].set(wup[0])
    wup_pad = wup_pad.at[1, GLA_GATE_RANK:2 * GLA_GATE_RANK].set(wup[1])
    o_fwd, o_bwd = _gla_scan(proj, gates, wup_pad.astype(BF16), w["gla_b_gate"].reshape(2, 1, GLA_DK), _gla_tri())
    return (o_fwd, o_bwd, proj, w["gla_norm_g"]), w["gla_w_out"]


def _layer(i, tok, mod, w, ffn_w, cast_next=(), g_final=None):
    last = i == DEPTH - 1
    mixed, w_out = _mixer(i, tok, mod, w["g_norm_mix"], w)
    joint = _resmm(mixed, w_out.astype(BF16), tok, mod, 2, n_tiles=NT_X if last else NT_ALL,
                   prologue={1: "sgu", 3: "gla"}.get(i % N_MIXERS))
    return _ffn(joint, w["g_norm_ffn"], mod, *ffn_w, with_context=not last, g_final=g_final, cast_next=cast_next)


def kernel(x, c, ctx, c_ctx, w_ada, b_ada, g_norm_mix, g_norm_ffn, w_ffn_in, w_ffn_out, g_final, fno_w_out, gmlp_w_in, gmlp_ln_g, gmlp_ln_b, gmlp_w_s, gmlp_b_s, gmlp_w_out, diff_w_in, diff_lambda, diff_subln_g, diff_w_out, gla_w_in, gla_w_gate_up, gla_b_gate, gla_norm_g, gla_w_out):
    tok = (x.reshape(T_X, D_MODEL), ctx.reshape(T_C, D_MODEL))
    c_rows = jnp.concatenate([c, c_ctx[None, :], jnp.zeros((MOD_ROWS - BATCH - 1, D_MODEL), F32)], axis=0)
    mod_all = _ada_table(c_rows, w_ada, b_ada).reshape(DEPTH, MOD_ROWS * N_ADA, 1, D_MODEL)
    mixer_weights = (
        dict(fno_w_out=fno_w_out),
        dict(gmlp_w_in=gmlp_w_in, gmlp_ln_g=gmlp_ln_g, gmlp_ln_b=gmlp_ln_b, gmlp_w_s=gmlp_w_s,
             gmlp_b_s=gmlp_b_s, gmlp_w_out=gmlp_w_out),
        dict(diff_w_in=diff_w_in, diff_lambda=diff_lambda, diff_subln_g=diff_subln_g, diff_w_out=diff_w_out),
        dict(gla_w_main=gla_w_in[:, :, :GLA_IN_MAIN], gla_w_gate=gla_w_in[:, :, GLA_IN_MAIN:],
             gla_w_gate_up=gla_w_gate_up, gla_b_gate=gla_b_gate, gla_norm_g=gla_norm_g, gla_w_out=gla_w_out),
    )
    big_weights = (("fno_w_out",), ("gmlp_w_in", "gmlp_w_out"), ("diff_w_in", "diff_w_out"), ("gla_w_main", "gla_w_out"))
    ffn_w = (w_ffn_in[0].astype(BF16), w_ffn_out[0].astype(BF16))
    precast = {}
    for i in range(DEPTH):
        kind, j = i % N_MIXERS, i // N_MIXERS
        last = i == DEPTH - 1
        w = {name: arr[j] for name, arr in mixer_weights[kind].items()}
        w.update(precast)
        w.update(g_norm_mix=g_norm_mix[i], g_norm_ffn=g_norm_ffn[i])
        cast_next, next_names = (), ()
        if not last:
            next_kind, next_j = (i + 1) % N_MIXERS, (i + 1) // N_MIXERS
            next_names = big_weights[next_kind]
            cast_next = ((w_ffn_in, i + 1), (w_ffn_out, i + 1)) + tuple(
                (mixer_weights[next_kind][name], next_j) for name in next_names)
        latent, context, casts = _layer(i, tok, mod_all[i], w, ffn_w, cast_next=cast_next,
                                        g_final=g_final if last else None)
        tok = (latent, context)
        if not last:
            ffn_w = tuple(casts[:2])
            precast = dict(zip(next_names, casts[2:]))
    return latent.reshape(BATCH, SEQ, D_MODEL)
```

```python
import functools
import math

import numpy as np
import jax
import jax.numpy as jnp
from jax import lax
from jax.experimental import pallas as pl
from jax.experimental.pallas import tpu as pltpu

D_MODEL = 2048
BATCH = 2
SEQ = 4096
DEPTH = 4
GRID_W = 64
CTX_LEN = 256
N_MIXERS = 4
NORM_EPS = 1e-6
D_FF = 4 * D_MODEL
N_ADA = 6
FNO_GROUPS = 4
GMLP_CHUNK = 128
GMLP_HEADS = 16
GMLP_WIDTH = D_MODEL
DIFF_HEADS = 8
DIFF_HEAD_DIM = D_MODEL // DIFF_HEADS // 2
ROPE_THETA = 10000.0
GLA_HEADS = 4
GLA_DK = D_MODEL // 2
GLA_DV = D_MODEL
GLA_DK_HEAD = GLA_DK // GLA_HEADS
GLA_DV_HEAD = GLA_DV // GLA_HEADS
GLA_GATE_RANK = 16
GLA_TAU = 16.0
GLA_CHUNK = 64
LN_EPS = 1e-5

T_X = BATCH * SEQ
T_C = BATCH * CTX_LEN
T_ALL = T_X + T_C
TM = 512
NT_X = T_X // TM
NT_ALL = T_ALL // TM
TILES_PER_BATCH = SEQ // TM
MOD_ROWS = 16
CTX_GROUP = BATCH
LANE = 128
MM_TN = 2048
FFN_TF = 1024
FFN_TM = 1024

BF16 = jnp.bfloat16
F32 = jnp.float32
VMEM_LIMIT = 56 * 1024 * 1024
FFN_VMEM_LIMIT = 60 * 1024 * 1024


def _params(semantics, vmem_limit=VMEM_LIMIT):
    return pltpu.CompilerParams(dimension_semantics=semantics, vmem_limit_bytes=vmem_limit)


def _mod_spec(slot, tile_rows=TM, first_tile=0):
    def index(i, *_):
        group = jnp.minimum((i + first_tile) // (SEQ // tile_rows), CTX_GROUP)
        return group * N_ADA + slot, 0, 0
    return pl.BlockSpec((1, 1, D_MODEL), index)


def _modulated(x, g, shift, scale):
    ms = jnp.mean(x * x, axis=-1, keepdims=True)
    y = x * lax.rsqrt(ms + NORM_EPS) * g
    return y * (1.0 + scale) + shift


def _ada_kernel(c_ref, w_ref, b_ref, o_ref):
    c = c_ref[...]
    s = (c * jax.nn.sigmoid(c)).astype(BF16)
    w = w_ref[0].astype(BF16)
    o_ref[0] = jnp.dot(s, w, preferred_element_type=F32) + b_ref[0]


def _ada_table(c8, w_ada, b_ada):
    tn = 1024
    n_out = N_ADA * D_MODEL
    return pl.pallas_call(
        _ada_kernel,
        grid=(DEPTH, n_out // tn),
        in_specs=[
            pl.BlockSpec((MOD_ROWS, D_MODEL), lambda l, j: (0, 0)),
            pl.BlockSpec((1, D_MODEL, tn), lambda l, j: (l, 0, j)),
            pl.BlockSpec((1, 1, tn), lambda l, j: (l, 0, j)),
        ],
        out_specs=pl.BlockSpec((1, MOD_ROWS, tn), lambda l, j: (l, 0, j)),
        out_shape=jax.ShapeDtypeStruct((DEPTH, MOD_ROWS, n_out), F32),
        compiler_params=_params(("arbitrary", "arbitrary")),
        name="ada_table",
    )(c8, w_ada, b_ada.reshape(DEPTH, 1, n_out))


def _gelu_exact(a):
    return 0.5 * a * (1.0 + lax.erf(a * (1.0 / math.sqrt(2.0))))


def _modmm_kernel(*refs, n_x, epilogue, tn, n_steps, side):
    x_refs, (g_ref, sh_ref, sc_ref, w_ref, *rest) = refs[:n_x], refs[n_x:]
    if epilogue == "rope":
        cs_ref, sn_ref, o_ref, h_ref = rest
    elif side:
        ws_ref, o_ref, os_ref, h_ref = rest
    else:
        o_ref, h_ref = rest
    j = pl.program_id(1)

    def column_step(step):
        if step == 0:
            h = _modulated(_stream_tile(x_refs), g_ref[...], sh_ref[0], sc_ref[0]).astype(BF16)
            h_ref[...] = h
        else:
            h = h_ref[...]
        acc = jnp.dot(h, w_ref[...], preferred_element_type=F32)
        if epilogue == "gelu":
            o_ref[...] = _gelu_exact(acc).astype(o_ref.dtype)
        elif epilogue == "rope" and step < (2 * D_MODEL) // tn:
            if step < D_MODEL // tn:
                acc = acc * ATTN_Q_SCALE
            cs = cs_ref[...]
            sn = sn_ref[...]
            for blk in range(tn // LANE):
                a = acc[:, blk * LANE:(blk + 1) * LANE]
                swapped = pltpu.roll(a, LANE // 2, 1)
                o_ref[:, blk * LANE:(blk + 1) * LANE] = (a * cs + swapped * sn).astype(o_ref.dtype)
        else:
            o_ref[...] = acc.astype(o_ref.dtype)
        if step == 0 and side:
            os_ref[...] = jnp.dot(h, ws_ref[...], preferred_element_type=F32).astype(os_ref.dtype)

    if epilogue == "rope":
        for step in range(n_steps):
            pl.when(j == step)(functools.partial(column_step, step))
    else:
        pl.when(j == 0)(functools.partial(column_step, 0))
        pl.when(j > 0)(functools.partial(column_step, 1))


def _modmm(x, g, mod, slot_shift, slot_scale, w, *, n_out=None, epilogue="plain", rope=None, w_side=None,
           out_dtype=BF16):
    k = w.shape[0]
    n_out = w.shape[1] if n_out is None else n_out
    tn = min(MM_TN, n_out)
    x_specs, x_args = _stream_operand(x, TM, k, NT_ALL)
    in_specs = x_specs + [
        pl.BlockSpec((1, k), lambda i, j: (0, 0)),
        _mod_spec(slot_shift),
        _mod_spec(slot_scale),
        pl.BlockSpec((k, tn), lambda i, j: (0, j)),
    ]
    args = x_args + [g.reshape(1, k), mod, mod, w]
    if epilogue == "rope":
        pos = lambda i, j: (jnp.where(i < NT_X, i % TILES_PER_BATCH, TILES_PER_BATCH), 0)
        in_specs += [pl.BlockSpec((TM, LANE), pos), pl.BlockSpec((TM, LANE), pos)]
        args += list(rope)
    out_specs = [pl.BlockSpec((TM, tn), lambda i, j: (i, j))]
    out_shape = [jax.ShapeDtypeStruct((T_ALL, n_out), out_dtype)]
    if w_side is not None:
        assert epilogue == "plain" and w_side.shape == (k, LANE)
        in_specs.append(pl.BlockSpec((k, LANE), lambda i, j: (0, 0)))
        args.append(w_side)
        out_specs.append(pl.BlockSpec((TM, LANE), lambda i, j: (i, 0)))
        out_shape.append(jax.ShapeDtypeStruct((T_ALL, LANE), out_dtype))
    results = pl.pallas_call(
        functools.partial(_modmm_kernel, n_x=len(x_args), epilogue=epilogue, tn=tn, n_steps=n_out // tn,
                          side=w_side is not None),
        grid=(NT_ALL, n_out // tn),
        in_specs=in_specs,
        out_specs=out_specs,
        out_shape=out_shape,
        scratch_shapes=[pltpu.VMEM((TM, k), BF16)],
        compiler_params=_params(("arbitrary", "arbitrary")),
        name="modmm_" + epilogue,
    )(*args)
    return results[0] if w_side is None else tuple(results)


def _stream_operand(arr, rows, width, n_tiles):
    if isinstance(arr, tuple) and n_tiles == NT_X:
        arr = arr[0]
    if isinstance(arr, tuple):
        assert arr[1].shape == (rows, width)
        return ([pl.BlockSpec((rows, width), lambda i, *_: (jnp.minimum(i, NT_X - 1), 0)),
                 pl.BlockSpec((rows, width), lambda i, *_: (0, 0))], list(arr))
    return [pl.BlockSpec((rows, width), lambda i, *_: (i, 0))], [arr]


def _stream_tile(refs):
    if len(refs) == 2:
        return jnp.where(pl.program_id(0) < NT_X, refs[0][...], refs[1][...])
    return refs[0][...]


def _gla_gated(of_ref, ob_ref, r_ref, g_ref):
    o = of_ref[...].astype(F32) + ob_ref[...].astype(F32)
    r = r_ref[...].astype(F32)
    gate = r * jax.nn.sigmoid(r)
    dv = GLA_DV_HEAD
    heads = []
    for h in range(GLA_HEADS):
        oh = o[:, h * dv:(h + 1) * dv]
        ms = jnp.mean(oh * oh, axis=-1, keepdims=True)
        heads.append((oh * lax.rsqrt(ms + NORM_EPS) * g_ref[...] * gate[:, h * dv:(h + 1) * dv]).astype(BF16))
    return jnp.concatenate(heads, axis=-1)


def _sgu_gated(u_ref, v_ref, lg_ref, lb_ref, ws_ref, bs_ref, a_ref):
    v = v_ref[...].astype(F32)
    mu = jnp.mean(v, axis=-1, keepdims=True)
    vc = v - mu
    var = jnp.mean(vc * vc, axis=-1, keepdims=True)
    vn = (vc * lax.rsqrt(var + LN_EPS) * lg_ref[...] + lb_ref[...]).astype(BF16)
    we = GMLP_WIDTH // GMLP_HEADS
    for c in range(TM // GMLP_CHUNK):
        r0 = c * GMLP_CHUNK
        for h in range(GMLP_HEADS):
            s = jnp.dot(ws_ref[h], vn[r0:r0 + GMLP_CHUNK, h * we:(h + 1) * we], preferred_element_type=F32)
            s = s + bs_ref[:, h:h + 1]
            u = u_ref[r0:r0 + GMLP_CHUNK, h * we:(h + 1) * we].astype(F32)
            a_ref[r0:r0 + GMLP_CHUNK, h * we:(h + 1) * we] = (u * s).astype(a_ref.dtype)
    return a_ref[...]


def _resmm_kernel(*refs, n_a, n_res, prologue):
    a_refs, refs = refs[:n_a], refs[n_a:]
    w_ref, refs = refs[0], refs[1:]
    res_refs, (gate_ref, o_ref, *scratch) = refs[:n_res], refs[n_res:]
    if prologue == "gla":
        a = _gla_gated(*a_refs)
    elif prologue == "sgu":
        a = _sgu_gated(*a_refs, *scratch)
    else:
        a = _stream_tile(a_refs)
    acc = jnp.dot(a, w_ref[...], preferred_element_type=F32)
    o_ref[...] = _stream_tile(res_refs) + gate_ref[0] * acc


def _resmm(a, w, res, mod, slot_gate, *, n_tiles, prologue=None):
    k, n_out = w.shape
    whole = lambda shape: pl.BlockSpec(shape, lambda i: (0,) * len(shape))
    scratch = []
    if prologue == "gla":
        o_fwd, o_bwd, proj, norm_g = a
        r_col = (2 * GLA_DK + GLA_DV) // GLA_DV
        a_specs = [pl.BlockSpec((TM, GLA_DV), lambda i: (i, 0)),
                   pl.BlockSpec((TM, GLA_DV), lambda i: (i, 0)),
                   pl.BlockSpec((TM, GLA_DV), lambda i: (i, r_col)),
                   whole((1, GLA_DV_HEAD))]
        a_args = [o_fwd, o_bwd, proj, norm_g.reshape(1, -1)]
    elif prologue == "sgu":
        z, ln_g, ln_b, w_s, b_s_t = a
        a_specs = [pl.BlockSpec((TM, GMLP_WIDTH), lambda i: (i, 0)),
                   pl.BlockSpec((TM, GMLP_WIDTH), lambda i: (i, 1)),
                   whole((1, GMLP_WIDTH)), whole((1, GMLP_WIDTH)),
                   whole((GMLP_HEADS, GMLP_CHUNK, GMLP_CHUNK)), whole((GMLP_CHUNK, GMLP_HEADS))]
        a_args = [z, z, ln_g.reshape(1, -1), ln_b.reshape(1, -1), w_s, b_s_t]
        scratch = [pltpu.VMEM((TM, GMLP_WIDTH), BF16)]
    else:
        a_specs, a_args = _stream_operand(a, TM, k, n_tiles)
    res_specs, res_args = _stream_operand(res, TM, n_out, n_tiles)
    return pl.pallas_call(
        functools.partial(_resmm_kernel, n_a=len(a_args), n_res=len(res_args), prologue=prologue),
        grid=(n_tiles,),
        in_specs=a_specs + [pl.BlockSpec((k, n_out), lambda i: (0, 0))] + res_specs + [_mod_spec(slot_gate)],
        out_specs=pl.BlockSpec((TM, n_out), lambda i: (i, 0)),
        out_shape=jax.ShapeDtypeStruct((n_tiles * TM, n_out), F32),
        scratch_shapes=scratch,
        compiler_params=_params(("arbitrary",)),
        name="resmm_" + prologue if prologue else "resmm",
    )(*a_args, w, *res_args, mod)


def _ffn_kernel(x_ref, g_ref, sh_ref, sc_ref, gate_ref, w1_ref, w2_ref, *rest, final_norm, n_casts):
    rest = list(rest)
    gf_ref = rest.pop(0) if final_norm else None
    next_in = [rest.pop(0) for _ in range(n_casts)]
    o_ref = rest.pop(0)
    next_out = [rest.pop(0) for _ in range(n_casts)]
    (h_ref,) = rest
    f = pl.program_id(1)
    for src, dst in zip(next_in, next_out):
        dst[...] = src[0].astype(BF16)

    def hidden_step(h):
        a = jnp.dot(h, w1_ref[...], preferred_element_type=F32)
        a = jnp.square(jnp.maximum(a, 0.0)).astype(BF16)
        return jnp.dot(a, w2_ref[...], preferred_element_type=F32)

    @pl.when(f == 0)
    def _():
        h = _modulated(x_ref[...], g_ref[...], sh_ref[0], sc_ref[0]).astype(BF16)
        h_ref[...] = h
        o_ref[...] = hidden_step(h)

    @pl.when(f > 0)
    def _():
        o_ref[...] += hidden_step(h_ref[...])

    @pl.when(f == pl.num_programs(1) - 1)
    def _():
        y = x_ref[...] + gate_ref[0] * o_ref[...]
        if final_norm:
            ms = jnp.mean(y * y, axis=-1, keepdims=True)
            y = y * lax.rsqrt(ms + NORM_EPS) * gf_ref[...]
        o_ref[...] = y


def _ffn_part(x, g, mod, w1, w2, *, tile_rows, first_tile, n_tiles, g_final=None, cast_next=()):
    final_norm = g_final is not None
    tf = FFN_TF
    nf = D_FF // tf
    single = pl.Buffered(1)
    in_specs = [
        pl.BlockSpec((tile_rows, D_MODEL), lambda i, f: (i + first_tile, 0)),
        pl.BlockSpec((1, D_MODEL), lambda i, f: (0, 0)),
        _mod_spec(3, tile_rows, first_tile),
        _mod_spec(4, tile_rows, first_tile),
        _mod_spec(5, tile_rows, first_tile),
        pl.BlockSpec((D_MODEL, tf), lambda i, f: (0, f)),
        pl.BlockSpec((tf, D_MODEL), lambda i, f: (f, 0)),
    ]
    args = [x, g.reshape(1, D_MODEL), mod, mod, mod, w1, w2]
    out_specs = [pl.BlockSpec((tile_rows, D_MODEL), lambda i, f: (i, 0), pipeline_mode=single)]
    out_shape = [jax.ShapeDtypeStruct((n_tiles * tile_rows, D_MODEL), F32)]
    if final_norm:
        in_specs.append(pl.BlockSpec((1, D_MODEL), lambda i, f: (0, 0)))
        args.append(g_final.reshape(1, D_MODEL))
    steps = n_tiles * nf
    for w_all, layer in cast_next:
        rows, cols = w_all.shape[1:]
        slab = rows // steps
        assert slab * steps == rows and slab % 16 == 0
        in_specs.append(pl.BlockSpec((1, slab, cols), lambda i, f, layer=layer: (layer, i * nf + f, 0)))
        args.append(w_all)
        out_specs.append(pl.BlockSpec((slab, cols), lambda i, f: (i * nf + f, 0)))
        out_shape.append(jax.ShapeDtypeStruct((rows, cols), BF16))
    results = pl.pallas_call(
        functools.partial(_ffn_kernel, final_norm=final_norm, n_casts=len(cast_next)),
        grid=(n_tiles, nf),
        in_specs=in_specs,
        out_specs=out_specs,
        out_shape=out_shape,
        scratch_shapes=[pltpu.VMEM((tile_rows, D_MODEL), BF16)],
        compiler_params=_params(("arbitrary", "arbitrary"), FFN_VMEM_LIMIT),
        name="ffn_final" if final_norm else "ffn",
    )(*args)
    return results[0], list(results[1:])


def _ffn(x, g, mod, w1, w2, *, with_context, g_final=None, cast_next=()):
    latent, casts = _ffn_part(x, g, mod, w1, w2, tile_rows=FFN_TM, first_tile=0, n_tiles=T_X // FFN_TM,
                              g_final=g_final, cast_next=cast_next)
    context = None
    if with_context:
        context, _ = _ffn_part(x, g, mod, w1, w2, tile_rows=T_C, first_tile=T_X // T_C, n_tiles=1)
    return latent, context, casts


def _dft_tables(n):
    idx = np.arange(n, dtype=np.int64)
    ang = 2.0 * np.pi * ((idx[:, None] * idx[None, :]) % n).astype(np.float64) / n
    return np.cos(ang) / math.sqrt(n), np.sin(ang) / math.sqrt(n)


def _channel_dft_matrix():
    c, s = _dft_tables(D_MODEL // FNO_GROUPS)
    return jnp.asarray(np.concatenate([c, s], axis=1), F32).astype(BF16)


def _chdft_kernel(*refs, n_x):
    x_refs, (g_ref, sh_ref, sc_ref, w_ref, sel_ref, oe_ref, oo_ref) = refs[:n_x], refs[n_x:]
    dg = D_MODEL // FNO_GROUPS
    w = w_ref[...]
    h_all = _modulated(_stream_tile(x_refs), g_ref[...], sh_ref[0], sc_ref[0]).astype(BF16)
    for parity, o_ref in enumerate((oe_ref, oo_ref)):
        h = jnp.dot(sel_ref[parity], h_all, preferred_element_type=F32).astype(BF16)
        for grp in range(FNO_GROUPS):
            cols = slice(grp * dg, (grp + 1) * dg)
            r = jnp.dot(h[:, cols], w, preferred_element_type=F32)
            o_ref[:, cols] = r[:, :dg].astype(o_ref.dtype)
            o_ref[:, D_MODEL + grp * dg:D_MODEL + (grp + 1) * dg] = r[:, dg:].astype(o_ref.dtype)


def _chdft(x, g, mod, w):
    dg = D_MODEL // FNO_GROUPS
    half = jax.ShapeDtypeStruct((T_ALL // 2, 2 * D_MODEL), BF16)
    x_specs, x_args = _stream_operand(x, TM, D_MODEL, NT_ALL)
    rows = np.arange(TM // 2)
    sel = np.zeros((2, TM // 2, TM), np.float32)
    sel[0, rows, 2 * rows] = 1.0
    sel[1, rows, 2 * rows + 1] = 1.0
    return pl.pallas_call(
        functools.partial(_chdft_kernel, n_x=len(x_args)),
        grid=(NT_ALL,),
        in_specs=x_specs + [
            pl.BlockSpec((1, D_MODEL), lambda i: (0, 0)),
            _mod_spec(0),
            _mod_spec(1),
            pl.BlockSpec((dg, 2 * dg), lambda i: (0, 0)),
            pl.BlockSpec((2, TM // 2, TM), lambda i: (0, 0, 0)),
        ],
        out_specs=[pl.BlockSpec((TM // 2, 2 * D_MODEL), lambda i: (i, 0))] * 2,
        out_shape=[half, half],
        compiler_params=_params(("arbitrary",)),
        name="chdft",
    )(*x_args, g.reshape(1, D_MODEL), mod, mod, w, jnp.asarray(sel).astype(BF16))


def _seq_dft_matrices(n, parity):
    half = n // 2
    kk = 2 * np.arange(half, dtype=np.int64) + parity
    if n <= 512:
        ang = 2.0 * np.pi * ((np.arange(half, dtype=np.int64)[:, None] * kk[None, :]) % n) / n
        return (jnp.asarray(np.cos(ang) / math.sqrt(n), F32).astype(BF16),
                jnp.asarray(-np.sin(ang) / math.sqrt(n), F32).astype(BF16))
    r = 64
    q = half // r
    ang_hi = 2.0 * np.pi * (((r * np.arange(q, dtype=np.int64))[:, None] * kk[None, :]) % n) / n
    ang_lo = 2.0 * np.pi * ((np.arange(r, dtype=np.int64)[:, None] * kk[None, :]) % n) / n
    ch = jnp.asarray(np.cos(ang_hi) / math.sqrt(n), F32)[:, None, :]
    sh = jnp.asarray(np.sin(ang_hi) / math.sqrt(n), F32)[:, None, :]
    cl = jnp.asarray(np.cos(ang_lo), F32)[None, :, :]
    sl = jnp.asarray(np.sin(ang_lo), F32)[None, :, :]
    c = (ch * cl - sh * sl).astype(BF16).reshape(half, half)
    s_neg = (-(sh * cl + ch * sl)).astype(BF16).reshape(half, half)
    return c, s_neg


def _seqdft_kernel(ce_ref, se_ref, co_ref, so_ref, ae_ref, be_ref, ao_ref, bo_ref, o_ref, acc_ref):
    kk = pl.program_id(3)

    @pl.when(kk == 0)
    def _():
        acc_ref[...] = jnp.zeros_like(acc_ref)

    acc_ref[0] += (jnp.dot(ce_ref[...], ae_ref[...], preferred_element_type=F32)
                   + jnp.dot(se_ref[...], be_ref[...], preferred_element_type=F32))
    acc_ref[1] += (jnp.dot(co_ref[...], ao_ref[...], preferred_element_type=F32)
                   + jnp.dot(so_ref[...], bo_ref[...], preferred_element_type=F32))

    @pl.when(kk == pl.num_programs(3) - 1)
    def _():
        o_ref[0] = (acc_ref[0] + acc_ref[1]).astype(o_ref.dtype)
        o_ref[1] = (acc_ref[0] - acc_ref[1]).astype(o_ref.dtype)


def _seqdft(ab_even, ab_odd, *, n, row0):
    half = n // 2
    tmm = min(half, 1024)
    tk = min(half, 512)
    tn = 1024
    nk = half // tk
    col_b = D_MODEL // tn
    mat = pl.BlockSpec((tmm, tk), lambda b, m, j, k: (m, k))
    a_spec = pl.BlockSpec((tk, tn), lambda b, m, j, k: (row0 // 2 // tk + b * nk + k, j))
    b_spec = pl.BlockSpec((tk, tn), lambda b, m, j, k: (row0 // 2 // tk + b * nk + k, col_b + j))
    out = pl.pallas_call(
        _seqdft_kernel,
        grid=(BATCH, half // tmm, D_MODEL // tn, nk),
        in_specs=[mat, mat, mat, mat, a_spec, b_spec, a_spec, b_spec],
        out_specs=pl.BlockSpec((2, tmm, tn), lambda b, m, j, k: (b, m, j)),
        out_shape=jax.ShapeDtypeStruct((BATCH * 2, half, D_MODEL), BF16),
        scratch_shapes=[pltpu.VMEM((2, tmm, tn), F32)],
        compiler_params=_params(("arbitrary",) * 4),
        name="seqdft",
    )(*_seq_dft_matrices(n, 0), *_seq_dft_matrices(n, 1), ab_even, ab_even, ab_odd, ab_odd)
    return out.reshape(BATCH * n, D_MODEL)


def _dot_nt(a, b):
    return lax.dot_general(a, b, (((1,), (1,)), ((), ())), preferred_element_type=F32)


ATTN_KEY_CHUNK = 256
ATTN_TQ = 512
ATTN_Q_SCALE = (DIFF_HEAD_DIM ** -0.5) * math.log2(math.e)


def _attn_kernel(lam_ref, g_ref, q_ref, *rest, latent, lam_init):
    if latent:
        kx_ref, vx_ref, kc_ref, vc_ref, o_ref, s_ref = rest
        chunks = [(kx_ref, vx_ref, r) for r in range(0, SEQ, ATTN_KEY_CHUNK)]
    else:
        kc_ref, vc_ref, o_ref, s_ref = rest
        chunks = []
    chunks += [(kc_ref, vc_ref, r) for r in range(0, CTX_LEN, ATTN_KEY_CHUNK)]
    hd = DIFF_HEAD_DIM
    kc = ATTN_KEY_CHUNK
    lv = lam_ref[...]
    lam = (jnp.exp(jnp.sum(lv[0:1] * lv[1:2], axis=-1, keepdims=True))
           - jnp.exp(jnp.sum(lv[2:3] * lv[3:4], axis=-1, keepdims=True)) + lam_init)
    q = q_ref[...]
    comps = (0, 1)
    qs = [q[:, c * hd:(c + 1) * hd] for c in comps]
    m = [None, None]
    for j, (k_ref, _, r0) in enumerate(chunks):
        for c in comps:
            s = _dot_nt(qs[c], k_ref[r0:r0 + kc, c * hd:(c + 1) * hd])
            s_ref[c, :, j * kc:(j + 1) * kc] = s
            sm = jnp.maximum(s[:, :LANE], s[:, LANE:])
            m[c] = sm if m[c] is None else jnp.maximum(m[c], sm)
    m = [jnp.max(mc, axis=-1, keepdims=True) for mc in m]
    l = [None, None]
    acc = [None, None]

    def exp_chunk(j, c, row_max):
        e = jnp.exp2(s_ref[c, :, j * kc:(j + 1) * kc] - row_max)
        ep = e[:, :LANE] + e[:, LANE:]
        l[c] = ep if l[c] is None else l[c] + ep
        return e.astype(BF16)

    ahead = 3
    n_chunks = len(chunks)
    ready = {(j, c): exp_chunk(j, c, m[c]) for j in range(min(ahead, n_chunks)) for c in comps}
    for j, (_, v_ref, r0) in enumerate(chunks):
        for c in comps:
            pv = jnp.dot(ready.pop((j, c)), v_ref[r0:r0 + kc, :], preferred_element_type=F32)
            acc[c] = pv if acc[c] is None else acc[c] + pv
            if j + ahead < n_chunks:
                zero = jnp.minimum(jnp.abs(pv[-1:, :1]), 0.0)
                ready[j + ahead, c] = exp_chunk(j + ahead, c, m[c] + zero)
    outs = [acc[c] * (1.0 / jnp.sum(l[c], axis=-1, keepdims=True)) for c in comps]
    o = outs[0] - lam * outs[1]
    ms = jnp.mean(o * o, axis=-1, keepdims=True)
    o = o * lax.rsqrt(ms + NORM_EPS) * g_ref[...] * (1.0 - lam_init)
    o_ref[...] = o.astype(o_ref.dtype)


def _diff_attention(qkv, lam_vecs, subln_g, lam_init, *, latent):
    tq = ATTN_TQ if latent else CTX_LEN
    hw = 2 * DIFF_HEAD_DIM
    k_col = D_MODEL // hw
    v_col = 2 * D_MODEL // hw
    n_q = SEQ if latent else CTX_LEN
    q_row0 = 0 if latent else T_X
    nq = n_q // tq
    in_specs = [
        pl.BlockSpec((4, DIFF_HEAD_DIM), lambda b, h, i: (0, 0)),
        pl.BlockSpec((1, hw), lambda b, h, i: (0, 0)),
        pl.BlockSpec((tq, hw), lambda b, h, i: (q_row0 // tq + b * nq + i, h)),
    ]
    args = [lam_vecs, subln_g.reshape(1, hw), qkv]
    if latent:
        in_specs += [
            pl.BlockSpec((SEQ, hw), lambda b, h, i: (b, k_col + h)),
            pl.BlockSpec((SEQ, hw), lambda b, h, i: (b, v_col + h)),
        ]
        args += [qkv, qkv]
    in_specs += [
        pl.BlockSpec((CTX_LEN, hw), lambda b, h, i: (T_X // CTX_LEN + b, k_col + h)),
        pl.BlockSpec((CTX_LEN, hw), lambda b, h, i: (T_X // CTX_LEN + b, v_col + h)),
    ]
    args += [qkv, qkv]
    return pl.pallas_call(
        functools.partial(_attn_kernel, latent=latent, lam_init=lam_init),
        grid=(BATCH, DIFF_HEADS, nq),
        in_specs=in_specs,
        out_specs=pl.BlockSpec((tq, hw), lambda b, h, i: (b * nq + i, h)),
        out_shape=jax.ShapeDtypeStruct((BATCH * n_q, D_MODEL), BF16),
        scratch_shapes=[pltpu.VMEM((2, tq, (SEQ if latent else 0) + CTX_LEN), F32)],
        compiler_params=_params(("arbitrary",) * 3),
        name="diffattn_latent" if latent else "diffattn_ctx",
    )(*args)


GLA_CTX_CHUNKS = CTX_LEN // GLA_CHUNK
GLA_X_CHUNKS = SEQ // GLA_CHUNK
GLA_STEPS = GLA_CTX_CHUNKS + GLA_X_CHUNKS


def _split3_bf16(a):
    hi = a.astype(BF16)
    r = a - hi.astype(F32)
    mid = r.astype(BF16)
    lo = (r - mid.astype(F32)).astype(BF16)
    return hi, mid, lo


def _gla_scan_kernel(qf_ref, kf_ref, vf_ref, gf_ref, qb_ref, kb_ref, vb_ref, gb_ref, wup_ref, bg_ref, tri_ref,
                     of_ref, ob_ref, st_ref):
    @pl.when(pl.program_id(1) == 0)
    def _():
        st_ref[...] = jnp.zeros_like(st_ref)

    dk, dv = GLA_DK_HEAD, GLA_DV_HEAD
    dirs = (0, 1)
    chains = [(d, h) for h in range(GLA_HEADS) for d in dirs]
    q_refs, k_refs, v_refs, g_refs, o_refs = (qf_ref, qb_ref), (kf_ref, kb_ref), (vf_ref, vb_ref), (gf_ref, gb_ref), (of_ref, ob_ref)
    tri = [tri_ref[d] for d in dirs]
    mask = [t.astype(F32) > 0.0 for t in tri]
    z = [jnp.dot(g_refs[d][...], wup_ref[d], preferred_element_type=F32) + bg_ref[d] for d in dirs]
    g = [(jnp.minimum(zd, 0.0) - jnp.log(1.0 + jnp.exp(-jnp.abs(zd)))) * (1.0 / GLA_TAU) for zd in z]
    parts = [_split3_bf16(gd) for gd in g]
    bcum = [sum(jnp.dot(tri[d], part, preferred_element_type=F32) for part in parts[d]) for d in dirs]
    blast = [jnp.sum(gd, axis=0, keepdims=True) for gd in g]
    q = [q_refs[d][...].astype(F32) * (dk ** -0.5) for d in dirs]
    k = [k_refs[d][...].astype(F32) for d in dirs]
    q_dec = [(q[d] * jnp.exp(bcum[d])).astype(BF16) for d in dirs]
    k_inv = [(k[d] * jnp.exp(-bcum[d])).astype(BF16) for d in dirs]
    k_tail = [(k[d] * jnp.exp(blast[d] - bcum[d])).astype(BF16) for d in dirs]
    decay = [jnp.exp(blast[d]) for d in dirs]
    ks = lambda h: slice(h * dk, (h + 1) * dk)
    vs = lambda h: slice(h * dv, (h + 1) * dv)
    st = {c: st_ref[c[0], c[1]] for c in chains}
    att = {(d, h): _dot_nt(q_dec[d][:, ks(h)], k_inv[d][:, ks(h)]) for d, h in chains}
    inter = {(d, h): _dot_nt(q_dec[d][:, ks(h)], st[d, h].astype(BF16)) for d, h in chains}
    att = {(d, h): jnp.where(mask[d], att[d, h], 0.0).astype(BF16) for d, h in chains}
    v = {(d, h): v_refs[d][:, vs(h)] for d, h in chains}
    upd = {c: lax.dot_general(v[c], k_tail[c[0]][:, ks(c[1])], (((0,), (0,)), ((), ())),
                              preferred_element_type=F32) for c in chains}
    out = {c: inter[c] + jnp.dot(att[c], v[c], preferred_element_type=F32) for c in chains}
    for d, h in chains:
        o_refs[d][:, vs(h)] = out[d, h].astype(o_refs[d].dtype)
        st_ref[d, h] = decay[d][:, ks(h)] * st[d, h] + upd[d, h]


def _gla_scan(proj, gates, wup_pad, b_gate, tri):
    def row_block(d):
        def index(b, s):
            c_idx = s if d == 0 else GLA_CTX_CHUNKS - 1 - s
            x_idx = s - GLA_CTX_CHUNKS if d == 0 else GLA_STEPS - 1 - s
            return jnp.where(s < GLA_CTX_CHUNKS,
                             T_X // GLA_CHUNK + b * GLA_CTX_CHUNKS + c_idx,
                             b * GLA_X_CHUNKS + x_idx)
        return index

    def chunk_specs(d):
        rb = row_block(d)
        return [
            pl.BlockSpec((GLA_CHUNK, GLA_DK), lambda b, s: (rb(b, s), 0)),
            pl.BlockSpec((GLA_CHUNK, GLA_DK), lambda b, s: (rb(b, s), 1)),
            pl.BlockSpec((GLA_CHUNK, GLA_DV), lambda b, s: (rb(b, s), 2 * GLA_DK // GLA_DV)),
            pl.BlockSpec((GLA_CHUNK, LANE), lambda b, s: (rb(b, s), 0)),
        ]

    whole = lambda shape: pl.BlockSpec(shape, lambda b, s: (0,) * len(shape))
    out_sds = jax.ShapeDtypeStruct((T_ALL, GLA_DV), BF16)
    return pl.pallas_call(
        _gla_scan_kernel,
        grid=(BATCH, GLA_STEPS),
        in_specs=chunk_specs(0) + chunk_specs(1) + [
            whole((2, LANE, GLA_DK)), whole((2, 1, GLA_DK)), whole((2, GLA_CHUNK, GLA_CHUNK))],
        out_specs=[pl.BlockSpec((GLA_CHUNK, GLA_DV), lambda b, s: (row_block(0)(b, s), 0)),
                   pl.BlockSpec((GLA_CHUNK, GLA_DV), lambda b, s: (row_block(1)(b, s), 0))],
        out_shape=[out_sds, out_sds],
        scratch_shapes=[pltpu.VMEM((2, GLA_HEADS, GLA_DV_HEAD, GLA_DK_HEAD), F32)],
        compiler_params=_params(("arbitrary",) * 2),
        name="gla_scan",
    )(proj, proj, proj, gates, proj, proj, proj, gates, wup_pad, b_gate, tri)


def _rope_tables():
    hd = DIFF_HEAD_DIM
    pos = np.arange(SEQ)
    row = (pos // GRID_W).astype(np.float32)
    col = (pos % GRID_W).astype(np.float32)
    n_freq = hd // 4
    inv = (ROPE_THETA ** (-np.arange(n_freq, dtype=np.float32) / n_freq)).astype(np.float32)
    ang = np.concatenate([row[:, None] * inv, col[:, None] * inv], axis=-1).astype(np.float64)
    cos, sin = np.cos(ang), np.sin(ang)
    cs = np.concatenate([cos, cos], axis=-1).astype(np.float32)
    sn = np.concatenate([-sin, sin], axis=-1).astype(np.float32)
    cs = np.concatenate([cs, np.ones((TM, 2 * cos.shape[1]), np.float32)], axis=0)
    sn = np.concatenate([sn, np.zeros((TM, 2 * cos.shape[1]), np.float32)], axis=0)
    return jnp.asarray(cs), jnp.asarray(sn)


def _gla_tri():
    lower = np.tril(np.ones((GLA_CHUNK, GLA_CHUNK), np.float32))
    return jnp.asarray(np.stack([lower, lower.T])).astype(BF16)


def _mixer(i, tok, mod, g_mix, w):
    kind = i % N_MIXERS
    if kind == 0:
        ab_even, ab_odd = _chdft(tok, g_mix, mod, _channel_dft_matrix())
        zx = _seqdft(ab_even, ab_odd, n=SEQ, row0=0)
        zc = _seqdft(ab_even, ab_odd, n=CTX_LEN, row0=T_X)
        return (zx, zc), w["fno_w_out"]
    if kind == 1:
        z = _modmm(tok, g_mix, mod, 0, 1, w["gmlp_w_in"].astype(BF16), epilogue="gelu")
        return (z, w["gmlp_ln_g"], w["gmlp_ln_b"], w["gmlp_w_s"].astype(BF16), w["gmlp_b_s"].T), w["gmlp_w_out"]
    if kind == 2:
        lam_init = 0.8 - 0.6 * math.exp(-0.3 * i)
        qkv = _modmm(tok, g_mix, mod, 0, 1, w["diff_w_in"].astype(BF16), epilogue="rope", rope=_rope_tables())
        ox = _diff_attention(qkv, w["diff_lambda"], w["diff_subln_g"], lam_init, latent=True)
        oc = _diff_attention(qkv, w["diff_lambda"], w["diff_subln_g"], lam_init, latent=False)
        return (ox, oc), w["diff_w_out"]
    n_main = 2 * GLA_DK + 2 * GLA_DV
    w_in = w["gla_w_in"]
    w_gate = jnp.pad(w_in[:, n_main:], ((0, 0), (0, LANE - 2 * GLA_GATE_RANK))).astype(BF16)
    proj, gates = _modmm(tok, g_mix, mod, 0, 1, w_in.astype(BF16), n_out=n_main, w_side=w_gate)
    wup = w["gla_w_gate_up"]
    wup_pad = jnp.zeros((2, LANE, GLA_DK), F32)
    wup_pad = wup_pad.at[0, :GLA_GATE_RANK].set(wup[0])
    wup_pad = wup_pad.at[1, GLA_GATE_RANK:2 * GLA_GATE_RANK].set(wup[1])
    o_fwd, o_bwd = _gla_scan(proj, gates, wup_pad.astype(BF16), w["gla_b_gate"].reshape(2, 1, GLA_DK), _gla_tri())
    return (o_fwd, o_bwd, proj, w["gla_norm_g"]), w["gla_w_out"]


def _layer(i, tok, mod, w, ffn_w, cast_next=(), g_final=None):
    last = i == DEPTH - 1
    mixed, w_out = _mixer(i, tok, mod, w["g_norm_mix"], w)
    joint = _resmm(mixed, w_out.astype(BF16), tok, mod, 2, n_tiles=NT_X if last else NT_ALL,
                   prologue={1: "sgu", 3: "gla"}.get(i % N_MIXERS))
    return _ffn(joint, w["g_norm_ffn"], mod, *ffn_w, with_context=not last, g_final=g_final, cast_next=cast_next)


def kernel(x, c, ctx, c_ctx, w_ada, b_ada, g_norm_mix, g_norm_ffn, w_ffn_in, w_ffn_out, g_final, fno_w_out, gmlp_w_in, gmlp_ln_g, gmlp_ln_b, gmlp_w_s, gmlp_b_s, gmlp_w_out, diff_w_in, diff_lambda, diff_subln_g, diff_w_out, gla_w_in, gla_w_gate_up, gla_b_gate, gla_norm_g, gla_w_out):
    tok = (x.reshape(T_X, D_MODEL), ctx.reshape(T_C, D_MODEL))
    c_rows = jnp.concatenate([c, c_ctx[None, :], jnp.zeros((MOD_ROWS - BATCH - 1, D_MODEL), F32)], axis=0)
    mod_all = _ada_table(c_rows, w_ada, b_ada).reshape(DEPTH, MOD_ROWS * N_ADA, 1, D_MODEL)
    mixer_weights = (
        dict(fno_w_out=fno_w_out),
        dict(gmlp_w_in=gmlp_w_in, gmlp_ln_g=gmlp_ln_g, gmlp_ln_b=gmlp_ln_b, gmlp_w_s=gmlp_w_s,
             gmlp_b_s=gmlp_b_s, gmlp_w_out=gmlp_w_out),
        dict(diff_w_in=diff_w_in, diff_lambda=diff_lambda, diff_subln_g=diff_subln_g, diff_w_out=diff_w_out),
        dict(gla_w_in=gla_w_in, gla_w_gate_up=gla_w_gate_up, gla_b_gate=gla_b_gate, gla_norm_g=gla_norm_g,
             gla_w_out=gla_w_out),
    )
    big_weights = (("fno_w_out",), ("gmlp_w_in", "gmlp_w_out"), ("diff_w_in", "diff_w_out"), ("gla_w_in", "gla_w_out"))
    ffn_w = (w_ffn_in[0].astype(BF16), w_ffn_out[0].astype(BF16))
    precast = {}
    for i in range(DEPTH):
        kind, j = i % N_MIXERS, i // N_MIXERS
        last = i == DEPTH - 1
        w = {name: arr[j] for name, arr in mixer_weights[kind].items()}
        w.update(precast)
        w.update(g_norm_mix=g_norm_mix[i], g_norm_ffn=g_norm_ffn[i])
        cast_next, next_names = (), ()
        if not last:
            next_kind, next_j = (i + 1) % N_MIXERS, (i + 1) // N_MIXERS
            next_names = big_weights[next_kind]
            cast_next = ((w_ffn_in, i + 1), (w_ffn_out, i + 1)) + tuple(
                (mixer_weights[next_kind][name], next_j) for name in next_names)
        latent, context, casts = _layer(i, tok, mod_all[i], w, ffn_w, cast_next=cast_next,
                                        g_final=g_final if last else None)
        tok = (latent, context)
        if not last:
            ffn_w = tuple(casts[:2])
            precast = dict(zip(next_names, casts[2:]))
    return latent.reshape(BATCH, SEQ, D_MODEL)
```
